```python
import math
import jax
import jax.numpy as jnp
from jax import lax
import numpy as np

D_MODEL = 1024
BATCH = 32
SEQ = 2048
DEPTH = 2

GRID_W = 64
CTX_LEN = 256
CONV_K = 3
RMS_EPS = 1e-6

GDN_HEADS = 4
GDN_DK = 128
GDN_DV = 128
GDN_CHUNK = 64

SSD_HEADS = 8
SSD_HEADDIM = 64
SSD_GROUPS = 2
SSD_STATE = 128
SSD_CHUNK = 128

S5_WIDTH = D_MODEL
S5_GROUP = 16
S5_GROUPS = S5_WIDTH // S5_GROUP
S5_STATE = 64

N_EXPERTS = 32
TOP_K = 4
D_EXPERT = D_MODEL
SWIGLU_LIMIT = 7.0
SWIGLU_ALPHA = 1.702

GDN_V_WIDTH = GDN_HEADS * GDN_DV
SSD_INNER = SSD_HEADS * SSD_HEADDIM
GDN_CONV_CH = 2 * GDN_HEADS * GDN_DK + GDN_V_WIDTH
SSD_CONV_CH = SSD_INNER + 2 * SSD_GROUPS * SSD_STATE
IN0_SIZES = (GDN_CONV_CH, SSD_CONV_CH, GDN_V_WIDTH, SSD_INNER, 2 * GDN_HEADS, 2 * GDN_HEADS, 2 * SSD_HEADS)
IN0_WIDTH = sum(IN0_SIZES)
MIX0_WIDTH = GDN_V_WIDTH + SSD_INNER

kernel_name = 'hybrid_gdn_ssd_s5_moe_diffusion'


def rmsnorm(t, w):
    tf = t.astype(jnp.float32)
    tf = tf * lax.rsqrt(jnp.mean(tf * tf, axis=-1, keepdims=True) + RMS_EPS)
    return (tf * w.astype(jnp.float32)).astype(t.dtype)


def l2norm(t):
    tf = t.astype(jnp.float32)
    return tf * lax.rsqrt(jnp.sum(tf * tf, axis=-1, keepdims=True) + RMS_EPS)


def modulate(t, shift, scale):
    return (t * (1.0 + scale) + shift).astype(t.dtype)


def grid_dwconv(t, w, n_rows, n_cols):
    b, l, ch = t.shape
    img = t.reshape(b, n_rows, n_cols, ch)
    out = lax.conv_general_dilated(img, w.astype(t.dtype)[:, :, None, :], window_strides=(1, 1), padding='SAME',
                                   dimension_numbers=('NHWC', 'HWIO', 'NHWC'), feature_group_count=ch)
    return out.reshape(b, l, ch)


def gated_delta_chunked(q, k, v, g, beta, s0):
    f32 = jnp.float32
    b, l, h, dk = q.shape
    c = GDN_CHUNK
    n = l // c

    def chunks(t):
        return t.astype(f32).reshape(b, n, c, h, -1).transpose(1, 0, 3, 2, 4)

    qc = chunks(q) * (dk ** -0.5)
    kc, vc = chunks(k), chunks(v)
    gc = jnp.cumsum(g.astype(f32).reshape(b, n, c, h).transpose(1, 0, 3, 2), axis=-1)
    bc = beta.astype(f32).reshape(b, n, c, h).transpose(1, 0, 3, 2)
    causal = jnp.tril(jnp.ones((c, c), bool))
    strict = jnp.tril(jnp.ones((c, c), bool), -1)
    seg = gc[..., :, None] - gc[..., None, :]
    decay = jnp.where(causal, jnp.exp(jnp.where(causal, seg, 0.0)), 0.0)
    kb = kc * bc[..., None]
    m = jnp.where(strict, jnp.einsum('nbhid,nbhjd->nbhij', kb, kc) * decay, 0.0)
    tri = jnp.eye(c, dtype=f32) + m
    u = lax.linalg.triangular_solve(tri, vc * bc[..., None], left_side=True, lower=True, unit_diagonal=True)
    w = lax.linalg.triangular_solve(tri, kb * jnp.exp(gc)[..., None], left_side=True, lower=True, unit_diagonal=True)
    qk = jnp.einsum('nbhid,nbhjd->nbhij', qc, kc) * decay
    q_dec = qc * jnp.exp(gc)[..., None]
    k_dec = kc * jnp.exp(gc[..., -1:] - gc)[..., None]
    g_last = jnp.exp(gc[..., -1])

    def step(s, inp):
        u_i, w_i, qk_i, qd_i, kd_i, gl_i = inp
        v_new = u_i - w_i @ s
        o_i = qd_i @ s + qk_i @ v_new
        s = s * gl_i[..., None, None] + jnp.swapaxes(kd_i, -1, -2) @ v_new
        return s, o_i

    s_fin, o = lax.scan(step, s0.astype(f32), (u, w, qk, q_dec, k_dec, g_last))
    o = o.transpose(1, 0, 3, 2, 4).reshape(b, l, h, -1)
    return o, s_fin


def ssd_chunked(x, dt, da, bm, cm, s0):
    f32 = jnp.float32
    b, l, h, p = x.shape
    g, nst = bm.shape[2], bm.shape[3]
    r = h // g
    c = SSD_CHUNK
    n = l // c
    xc = (x.astype(f32) * dt.astype(f32)[..., None]).reshape(b, n, c, g, r, p)
    acs = jnp.cumsum(da.astype(f32).reshape(b, n, c, g, r), axis=2)
    bc = bm.astype(f32).reshape(b, n, c, g, nst)
    cc = cm.astype(f32).reshape(b, n, c, g, nst)
    causal = jnp.tril(jnp.ones((c, c), bool))[:, :, None, None]
    seg = acs[:, :, :, None] - acs[:, :, None, :]
    lmat = jnp.where(causal, jnp.exp(jnp.where(causal, seg, 0.0)), 0.0)
    scores = jnp.einsum('bnlgd,bnsgd->bnlsg', cc, bc)[..., None] * lmat
    y_diag = jnp.einsum('bnlsgr,bnsgrp->bnlgrp', scores, xc)
    decay_out = jnp.exp(acs[:, :, -1:] - acs)
    chunk_states = jnp.einsum('bnsgd,bnsgrp->bngrpd', bc, xc * decay_out[..., None])
    states = jnp.concatenate([s0.astype(f32).reshape(b, 1, g, r, p, nst), chunk_states], axis=1)
    cum = jnp.cumsum(jnp.pad(acs[:, :, -1], ((0, 0), (1, 0), (0, 0), (0, 0))), axis=1)
    tri = jnp.tril(jnp.ones((n + 1, n + 1), bool))[:, :, None, None]
    segc = cum[:, :, None] - cum[:, None, :]
    dec = jnp.where(tri, jnp.exp(jnp.where(tri, segc, 0.0)), 0.0)
    new = jnp.einsum('bzcgr,bcgrpd->bzgrpd', dec, states)
    y_off = jnp.einsum('bnlgd,bngrpd->bnlgrp', cc, new[:, :-1]) * jnp.exp(acs)[..., None]
    y = (y_diag + y_off).reshape(b, l, h, p)
    return y, new[:, -1].reshape(b, h, p, nst)


def bidirectional(scan_fn, ctx_dirs, lat_dirs, s0):
    flip = lambda t: jnp.flip(t, axis=1)
    y_ctx, y_lat = 0.0, 0.0
    for d in range(2):
        rev = d == 1
        cargs = tuple(flip(t) for t in ctx_dirs[d]) if rev else ctx_dirs[d]
        largs = tuple(flip(t) for t in lat_dirs[d]) if rev else lat_dirs[d]
        o_c, s_c = scan_fn(*cargs, s0)
        o_l, _ = scan_fn(*largs, s_c)
        y_ctx = y_ctx + (flip(o_c) if rev else o_c)
        y_lat = y_lat + (flip(o_l) if rev else o_l)
    return y_ctx, y_lat


def hybrid_mixer(h_ctx, h_lat, rows, w_in, gdn_conv_w, gdn_a_log, gdn_dt_bias, gdn_norm_w,
                 ssd_conv_w, ssd_conv_b, ssd_a_log, ssd_dt_bias, ssd_d, ssd_norm_w, w_out, need_ctx):
    f32 = jnp.float32
    splits = np.cumsum(IN0_SIZES)[:-1].tolist()

    def features(h, n_rows, n_cols):
        b, l, _ = h.shape
        conv_a, conv_b, z_a, z_b, a_raw, b_raw, dt_raw = jnp.split(h @ w_in, splits, axis=-1)
        conv_a = jax.nn.silu(grid_dwconv(conv_a, gdn_conv_w, n_rows, n_cols))
        conv_b = jax.nn.silu(grid_dwconv(conv_b, ssd_conv_w, n_rows, n_cols) + ssd_conv_b)
        q, k, v = jnp.split(conv_a, [GDN_HEADS * GDN_DK, 2 * GDN_HEADS * GDN_DK], axis=-1)
        q = l2norm(q.reshape(b, l, GDN_HEADS, GDN_DK))
        k = l2norm(k.reshape(b, l, GDN_HEADS, GDN_DK))
        v = v.reshape(b, l, GDN_HEADS, GDN_DV)
        g = -jnp.exp(gdn_a_log.astype(f32)) * jax.nn.softplus(a_raw.astype(f32).reshape(b, l, 2, GDN_HEADS) + gdn_dt_bias)
        beta = jax.nn.sigmoid(b_raw.astype(f32).reshape(b, l, 2, GDN_HEADS))
        xs, bm, cm = jnp.split(conv_b, [SSD_INNER, SSD_INNER + SSD_GROUPS * SSD_STATE], axis=-1)
        xs = xs.reshape(b, l, SSD_HEADS, SSD_HEADDIM)
        bm = bm.reshape(b, l, SSD_GROUPS, SSD_STATE)
        cm = cm.reshape(b, l, SSD_GROUPS, SSD_STATE)
        dt = jax.nn.softplus(dt_raw.astype(f32).reshape(b, l, 2, SSD_HEADS) + ssd_dt_bias)
        da = -jnp.exp(ssd_a_log.astype(f32)) * dt
        gdn_dirs = tuple((q, k, v, g[:, :, d], beta[:, :, d]) for d in range(2))
        ssd_dirs = tuple((xs, dt[:, :, d], da[:, :, d], bm, cm) for d in range(2))
        return gdn_dirs, ssd_dirs, xs, z_a, z_b

    b = h_lat.shape[0]
    gdn_c, ssd_c, xs_c, za_c, zb_c = features(h_ctx, 1, h_ctx.shape[1])
    gdn_l, ssd_l, xs_l, za_l, zb_l = features(h_lat, rows, GRID_W)
    o_c, o_l = bidirectional(gated_delta_chunked, gdn_c, gdn_l, jnp.zeros((b, GDN_HEADS, GDN_DK, GDN_DV), f32))
    y_c, y_l = bidirectional(ssd_chunked, ssd_c, ssd_l, jnp.zeros((b, SSD_HEADS, SSD_HEADDIM, SSD_STATE), f32))

    def merge(o, y, xs, z_a, z_b, dtype):
        bb, l = o.shape[0], o.shape[1]
        o = rmsnorm(o, gdn_norm_w) * jax.nn.silu(z_a.astype(f32)).reshape(bb, l, GDN_HEADS, GDN_DV)
        y = (y + ssd_d.astype(f32)[:, None] * xs).reshape(bb, l, SSD_INNER) * jax.nn.silu(z_b.astype(f32))
        y = rmsnorm(y.reshape(bb, l, SSD_GROUPS, -1), ssd_norm_w.reshape(SSD_GROUPS, -1))
        mixed = jnp.concatenate([o.reshape(bb, l, -1), y.reshape(bb, l, -1)], axis=-1)
        return (mixed.astype(dtype) @ w_out).astype(dtype)

    y_lat = merge(o_l, y_l, xs_l, za_l, zb_l, h_lat.dtype)
    y_ctx = merge(o_c, y_c, xs_c, za_c, zb_c, h_ctx.dtype) if need_ctx else None
    return y_ctx, y_lat


def s5_discretize(a_re, a_im, log_step, b_re, b_im):
    f32 = jnp.float32
    a_re, a_im = a_re.astype(f32), a_im.astype(f32)
    step = jnp.exp(log_step.astype(f32))[:, None]
    mag = jnp.exp(a_re * step)
    lam_re, lam_im = mag * jnp.cos(a_im * step), mag * jnp.sin(a_im * step)
    den = a_re * a_re + a_im * a_im
    f_re = ((lam_re - 1.0) * a_re + lam_im * a_im) / den
    f_im = (lam_im * a_re - (lam_re - 1.0) * a_im) / den
    b_re, b_im = b_re.astype(f32), b_im.astype(f32)
    bb_re = f_re[..., None] * b_re - f_im[..., None] * b_im
    bb_im = f_re[..., None] * b_im + f_im[..., None] * b_re
    return lam_re, lam_im, bb_re, bb_im


def s5_scan(lam_re, lam_im, bb_re, bb_im, u, h0_re, h0_im, reverse):
    bu_re = jnp.einsum('gph,blgh->blgp', bb_re, u)
    bu_im = jnp.einsum('gph,blgh->blgp', bb_im, u)
    first = -1 if reverse else 0
    bu_re = bu_re.at[:, first].add(lam_re * h0_re - lam_im * h0_im)
    bu_im = bu_im.at[:, first].add(lam_re * h0_im + lam_im * h0_re)
    l = u.shape[1]
    a_re = jnp.broadcast_to(lam_re, (1, l) + lam_re.shape)
    a_im = jnp.broadcast_to(lam_im, (1, l) + lam_im.shape)

    def combine(e1, e2):
        a1r, a1i, b1r, b1i = e1
        a2r, a2i, b2r, b2i = e2
        return (a2r * a1r - a2i * a1i, a2r * a1i + a2i * a1r,
                a2r * b1r - a2i * b1i + b2r, a2r * b1i + a2i * b1r + b2i)

    _, _, x_re, x_im = lax.associative_scan(combine, (a_re, a_im, bu_re, bu_im), reverse=reverse, axis=1)
    return x_re, x_im


def s5_readout(x_re, x_im, c_re, c_im):
    f32 = jnp.float32
    y = jnp.einsum('ghp,blgp->blgh', c_re.astype(f32), x_re) - jnp.einsum('ghp,blgp->blgh', c_im.astype(f32), x_im)
    return y.reshape(y.shape[0], y.shape[1], -1)


def s5_mixer(h_ctx, h_lat, w_in, a_re, a_im, log_step, b_re, b_im, c_re, c_im, d_skip, w_glu_a, w_glu_b, need_ctx):
    f32 = jnp.float32
    b = h_lat.shape[0]
    u_c = (h_ctx @ w_in).astype(f32)
    u_l = (h_lat @ w_in).astype(f32)
    ug_c = u_c.reshape(b, u_c.shape[1], S5_GROUPS, S5_GROUP)
    ug_l = u_l.reshape(b, u_l.shape[1], S5_GROUPS, S5_GROUP)
    y_l = d_skip.astype(f32) * u_l
    y_c = d_skip.astype(f32) * u_c if need_ctx else None
    zero = jnp.zeros((b, S5_GROUPS, S5_STATE), f32)
    for d in range(2):
        rev = d == 1
        lam_re, lam_im, bb_re, bb_im = s5_discretize(a_re[d], a_im[d], log_step[d], b_re[d], b_im[d])
        xc_re, xc_im = s5_scan(lam_re, lam_im, bb_re, bb_im, ug_c, zero, zero, rev)
        end = 0 if rev else -1
        xl_re, xl_im = s5_scan(lam_re, lam_im, bb_re, bb_im, ug_l, xc_re[:, end], xc_im[:, end], rev)
        y_l = y_l + s5_readout(xl_re, xl_im, c_re[d], c_im[d])
        if need_ctx:
            y_c = y_c + s5_readout(xc_re, xc_im, c_re[d], c_im[d])

    def glu(y, dtype):
        o = jax.nn.gelu(y).astype(dtype)
        return ((o @ w_glu_a) * jax.nn.sigmoid(o @ w_glu_b)).astype(dtype)

    y_lat = glu(y_l, h_lat.dtype)
    y_ctx = glu(y_c, h_ctx.dtype) if need_ctx else None
    return y_ctx, y_lat


def moe_ffn(h, router_w, router_b, w_gate, b_gate, w_up, b_up, w_down, b_down):
    f32 = jnp.float32
    t = h.reshape(-1, h.shape[-1])
    logits = (t @ router_w + router_b).astype(f32)
    top_val, top_idx = lax.top_k(logits, TOP_K)
    probs = jax.nn.softmax(top_val, axis=-1)
    gates = jnp.einsum('tk,tke->te', probs, jax.nn.one_hot(top_idx, N_EXPERTS, dtype=f32))
    out = jnp.zeros(t.shape, f32)
    for e in range(N_EXPERTS):
        gl = jnp.minimum(t @ w_gate[e] + b_gate[e], SWIGLU_LIMIT)
        lin = jnp.clip(t @ w_up[e] + b_up[e], -SWIGLU_LIMIT, SWIGLU_LIMIT)
        act = gl * jax.nn.sigmoid(SWIGLU_ALPHA * gl) * (lin + 1.0)
        out = out + gates[:, e:e + 1] * (act @ w_down[e] + b_down[e]).astype(f32)
    return out.reshape(h.shape).astype(h.dtype)


def setup_inputs(seed: int = 0) -> dict:
    key = jax.random.key(seed)
    keys = iter(jax.random.split(key, 48))
    f32 = jnp.float32

    def normal(shape, scale):
        return jax.random.normal(next(keys), shape, f32) * scale

    def uniform(shape, lo, hi):
        return jax.random.uniform(next(keys), shape, f32, lo, hi)

    def gain(shape):
        return 1.0 + normal(shape, 0.05)

    def dt_bias(shape):
        dt = jnp.exp(uniform(shape, math.log(1e-3), math.log(1e-1)))
        return jnp.log(jnp.expm1(dt))

    ne, no = (DEPTH + 1) // 2, DEPTH // 2
    d, e, f = D_MODEL, N_EXPERTS, D_EXPERT
    return {
        'x': normal((BATCH, SEQ, d), 1.0),
        'c': normal((BATCH, d), 1.0),
        'ctx': normal((BATCH, CTX_LEN, d), 1.0),
        'c_ctx': normal((d,), 1.0),
        'ada_w': normal((DEPTH, d, 6 * d), 0.5 * d ** -0.5),
        'ada_b': normal((DEPTH, 6 * d), 0.02),
        'mix_norm_pre': gain((DEPTH, d)),
        'mix_norm_post': gain((DEPTH, d)),
        'ffn_norm_pre': gain((DEPTH, d)),
        'ffn_norm_post': gain((DEPTH, d)),
        'router_w': normal((DEPTH, d, e), d ** -0.5),
        'router_b': normal((DEPTH, e), 0.01),
        'moe_w_gate': normal((DEPTH, e, d, f), d ** -0.5),
        'moe_b_gate': normal((DEPTH, e, f), 0.02),
        'moe_w_up': normal((DEPTH, e, d, f), d ** -0.5),
        'moe_b_up': normal((DEPTH, e, f), 0.02),
        'moe_w_down': normal((DEPTH, e, f, d), f ** -0.5),
        'moe_b_down': normal((DEPTH, e, d), 0.02),
        'hy_w_in': normal((ne, d, IN0_WIDTH), d ** -0.5),
        'gdn_conv_w': normal((ne, CONV_K, CONV_K, GDN_CONV_CH), 1.0 / CONV_K),
        'gdn_a_log': jnp.log(uniform((ne, 2, GDN_HEADS), 1.0, 16.0)),
        'gdn_dt_bias': dt_bias((ne, 2, GDN_HEADS)),
        'gdn_norm_w': gain((ne, GDN_DV)),
        'ssd_conv_w': normal((ne, CONV_K, CONV_K, SSD_CONV_CH), 1.0 / CONV_K),
        'ssd_conv_b': normal((ne, SSD_CONV_CH), 0.02),
        'ssd_a_log': jnp.log(uniform((ne, 2, SSD_HEADS), 1.0, 16.0)),
        'ssd_dt_bias': dt_bias((ne, 2, SSD_HEADS)),
        'ssd_d': gain((ne, SSD_HEADS)),
        'ssd_norm_w': gain((ne, SSD_INNER)),
        'hy_w_out': normal((ne, MIX0_WIDTH, d), MIX0_WIDTH ** -0.5),
        's5_w_in': normal((no, d, S5_WIDTH), d ** -0.5),
        's5_a_re': -0.5 + normal((no, 2, S5_GROUPS, S5_STATE), 0.01),
        's5_a_im': jnp.pi * jnp.arange(S5_STATE, dtype=f32) + normal((no, 2, S5_GROUPS, S5_STATE), 0.01),
        's5_log_step': uniform((no, 2, S5_GROUPS), math.log(1e-3), math.log(1e-1)),
        's5_b_re': normal((no, 2, S5_GROUPS, S5_STATE, S5_GROUP), (2 * S5_GROUP) ** -0.5),
        's5_b_im': normal((no, 2, S5_GROUPS, S5_STATE, S5_GROUP), (2 * S5_GROUP) ** -0.5),
        's5_c_re': normal((no, 2, S5_GROUPS, S5_GROUP, S5_STATE), (S5_STATE) ** -0.5),
        's5_c_im': normal((no, 2, S5_GROUPS, S5_GROUP, S5_STATE), (S5_STATE) ** -0.5),
        's5_d': normal((no, S5_WIDTH), 1.0),
        's5_w_glu_a': normal((no, S5_WIDTH, d), S5_WIDTH ** -0.5),
        's5_w_glu_b': normal((no, S5_WIDTH, d), S5_WIDTH ** -0.5),
    }


def reference(x, c, ctx, c_ctx, ada_w, ada_b, mix_norm_pre, mix_norm_post, ffn_norm_pre, ffn_norm_post,
              router_w, router_b, moe_w_gate, moe_b_gate, moe_w_up, moe_b_up, moe_w_down, moe_b_down,
              hy_w_in, gdn_conv_w, gdn_a_log, gdn_dt_bias, gdn_norm_w, ssd_conv_w, ssd_conv_b, ssd_a_log,
              ssd_dt_bias, ssd_d, ssd_norm_w, hy_w_out,
              s5_w_in, s5_a_re, s5_a_im, s5_log_step, s5_b_re, s5_b_im, s5_c_re, s5_c_im, s5_d,
              s5_w_glu_a, s5_w_glu_b):
    rows = x.shape[1] // GRID_W
    h_lat, h_ctx = x, ctx
    cond_lat = jax.nn.silu(c)
    cond_ctx = jax.nn.silu(c_ctx)
    for i in range(DEPTH):
        j = i // 2
        need_ctx = i < DEPTH - 1
        m_lat = jnp.split((cond_lat @ ada_w[i] + ada_b[i])[:, None, :], 6, axis=-1)
        m_ctx = jnp.split(cond_ctx @ ada_w[i] + ada_b[i], 6, axis=-1)
        a_lat = modulate(rmsnorm(h_lat, mix_norm_pre[i]), m_lat[0], m_lat[1])
        a_ctx = modulate(rmsnorm(h_ctx, mix_norm_pre[i]), m_ctx[0], m_ctx[1])
        if i % 2 == 0:
            y_ctx, y_lat = hybrid_mixer(a_ctx, a_lat, rows, hy_w_in[j], gdn_conv_w[j], gdn_a_log[j], gdn_dt_bias[j],
                                        gdn_norm_w[j], ssd_conv_w[j], ssd_conv_b[j], ssd_a_log[j], ssd_dt_bias[j],
                                        ssd_d[j], ssd_norm_w[j], hy_w_out[j], need_ctx)
        else:
            y_ctx, y_lat = s5_mixer(a_ctx, a_lat, s5_w_in[j], s5_a_re[j], s5_a_im[j], s5_log_step[j], s5_b_re[j],
                                    s5_b_im[j], s5_c_re[j], s5_c_im[j], s5_d[j], s5_w_glu_a[j], s5_w_glu_b[j], need_ctx)
        ffn = (router_w[i], router_b[i], moe_w_gate[i], moe_b_gate[i], moe_w_up[i], moe_b_up[i],
               moe_w_down[i], moe_b_down[i])
        h_lat = h_lat + m_lat[2] * rmsnorm(y_lat, mix_norm_post[i])
        f_lat = moe_ffn(modulate(rmsnorm(h_lat, ffn_norm_pre[i]), m_lat[3], m_lat[4]), *ffn)
        h_lat = h_lat + m_lat[5] * rmsnorm(f_lat, ffn_norm_post[i])
        if need_ctx:
            h_ctx = h_ctx + m_ctx[2] * rmsnorm(y_ctx, mix_norm_post[i])
            f_ctx = moe_ffn(modulate(rmsnorm(h_ctx, ffn_norm_pre[i]), m_ctx[3], m_ctx[4]), *ffn)
            h_ctx = h_ctx + m_ctx[5] * rmsnorm(f_ctx, ffn_norm_post[i])
    return h_lat
```

```python
import functools
import math

import jax
import jax.numpy as jnp
import numpy as np
from jax import lax
from jax.experimental import pallas as pl
from jax.experimental.pallas import tpu as pltpu

F32 = jnp.float32
BF16 = jnp.bfloat16

D_MODEL = 1024
GRID_W = 64
RMS_EPS = 1e-6

GDN_HEADS = 4
GDN_DK = 128
GDN_DV = 128
GDN_CHUNK = 64
SSD_HEADS = 8
SSD_HEADDIM = 64
SSD_GROUPS = 2
SSD_STATE = 128
SSD_CHUNK = 128
S5_GROUP = 16
S5_GROUPS = D_MODEL // S5_GROUP
S5_STATE = 64
N_EXPERTS = 32
TOP_K = 4
SWIGLU_LIMIT = 7.0
SWIGLU_ALPHA = 1.702

GDN_V_WIDTH = GDN_HEADS * GDN_DV
SSD_INNER = SSD_HEADS * SSD_HEADDIM
GDN_CONV_CH = 2 * GDN_HEADS * GDN_DK + GDN_V_WIDTH
SSD_CONV_CH = SSD_INNER + 2 * SSD_GROUPS * SSD_STATE
IN0_SIZES = (GDN_CONV_CH, SSD_CONV_CH, GDN_V_WIDTH, SSD_INNER, 2 * GDN_HEADS, 2 * GDN_HEADS, 2 * SSD_HEADS)

V7X_VMEM_LIMIT_BYTES = 56 * 1024 * 1024
MOE_TILE_M = 512
MOE_TILE_F = 512


def _moe_ffn_body(tile_e_ref, tile_ok_ref, x_ref, wg_ref, bg_ref, wu_ref, bu_ref, wd_ref, bd_ref, o_ref):
    i = pl.program_id(0)

    @pl.when(tile_ok_ref[i] > 0)
    def _():
        x = x_ref[...]
        acc = jnp.zeros(o_ref.shape, F32)
        n_f = wg_ref.shape[2] // MOE_TILE_F
        for c in range(n_f):
            cs = slice(c * MOE_TILE_F, (c + 1) * MOE_TILE_F)
            gl = jnp.dot(x, wg_ref[0, :, cs], preferred_element_type=F32) + bg_ref[0, :, cs]
            lin = jnp.dot(x, wu_ref[0, :, cs], preferred_element_type=F32) + bu_ref[0, :, cs]
            gl = jnp.minimum(gl, SWIGLU_LIMIT)
            lin = jnp.clip(lin, -SWIGLU_LIMIT, SWIGLU_LIMIT)
            act = gl * jax.nn.sigmoid(SWIGLU_ALPHA * gl) * (lin + 1.0)
            acc = acc + jnp.dot(act.astype(BF16), wd_ref[0, cs, :], preferred_element_type=F32)
        o_ref[...] = (acc + bd_ref[0]).astype(o_ref.dtype)


def _moe_grouped_ffn(xs, tile_e, tile_ok, wg, bg, wu, bu, wd, bd):
    p, d = xs.shape
    e, _, f = wg.shape
    n_tiles = p // MOE_TILE_M
    grid_spec = pltpu.PrefetchScalarGridSpec(
        num_scalar_prefetch=2,
        grid=(n_tiles,),
        in_specs=[
            pl.BlockSpec((MOE_TILE_M, d), lambda i, te, ok: (i, 0)),
            pl.BlockSpec((1, d, f), lambda i, te, ok: (te[i], 0, 0)),
            pl.BlockSpec((1, 1, f), lambda i, te, ok: (te[i], 0, 0)),
            pl.BlockSpec((1, d, f), lambda i, te, ok: (te[i], 0, 0)),
            pl.BlockSpec((1, 1, f), lambda i, te, ok: (te[i], 0, 0)),
            pl.BlockSpec((1, f, d), lambda i, te, ok: (te[i], 0, 0)),
            pl.BlockSpec((1, 1, d), lambda i, te, ok: (te[i], 0, 0)),
        ],
        out_specs=pl.BlockSpec((MOE_TILE_M, d), lambda i, te, ok: (i, 0)),
    )
    return pl.pallas_call(
        _moe_ffn_body,
        grid_spec=grid_spec,
        out_shape=jax.ShapeDtypeStruct((p, d), BF16),
        compiler_params=pltpu.CompilerParams(
            dimension_semantics=("arbitrary",), vmem_limit_bytes=V7X_VMEM_LIMIT_BYTES),
        name="moe_grouped_ffn",
    )(tile_e, tile_ok, xs, wg, bg.reshape(e, 1, f), wu, bu.reshape(e, 1, f), wd, bd.reshape(e, 1, d))


def _moe_combine_body(yg_ref, p_ref, o_ref):
    d = o_ref.shape[1]
    p = p_ref[...]
    acc = jnp.zeros(o_ref.shape, F32)
    for k in range(TOP_K):
        acc = acc + p[:, k:k + 1] * yg_ref[:, k * d:(k + 1) * d].astype(F32)
    o_ref[...] = acc


def _moe_combine(yg, probs):
    t, kd = yg.shape
    d = kd // TOP_K
    tm = 512
    return pl.pallas_call(
        _moe_combine_body,
        grid=(t // tm,),
        in_specs=[pl.BlockSpec((tm, kd), lambda i: (i, 0)), pl.BlockSpec((tm, TOP_K), lambda i: (i, 0))],
        out_specs=pl.BlockSpec((tm, d), lambda i: (i, 0)),
        out_shape=jax.ShapeDtypeStruct((t, d), F32),
        compiler_params=pltpu.CompilerParams(dimension_semantics=("arbitrary",)),
        name="moe_combine",
    )(yg, probs)


def _moe_route(top_idx):
    t = top_idx.shape[0]
    a = t * TOP_K
    tm = MOE_TILE_M
    eid = top_idx.reshape(a)
    order = jnp.argsort(eid, stable=True).astype(jnp.int32)
    inv = jnp.argsort(order).astype(jnp.int32)
    counts = jnp.sum(jax.nn.one_hot(eid, N_EXPERTS, dtype=jnp.int32), axis=0)
    off = jnp.cumsum(counts) - counts
    pcounts = ((counts + tm - 1) // tm) * tm
    pend = jnp.cumsum(pcounts)
    poff = pend - pcounts
    n_tiles = a // tm + N_EXPERTS
    tile_start = jnp.arange(n_tiles, dtype=jnp.int32) * tm
    tile_e = jnp.minimum(jnp.searchsorted(pend, tile_start, side='right'), N_EXPERTS - 1).astype(jnp.int32)
    tile_ok = (tile_start < pend[-1]).astype(jnp.int32)
    ppos = jnp.arange(n_tiles * tm, dtype=jnp.int32)
    pe = jnp.repeat(tile_e, tm)
    r = ppos - poff[pe]
    src_rank = jnp.clip(off[pe] + jnp.minimum(r, counts[pe] - 1), 0, a - 1)
    src_tok = order[src_rank] // TOP_K
    pos = poff[eid] + (inv - off[eid])
    return src_tok, pos, tile_e, tile_ok


def _moe_ffn(t_bf16, top_idx, probs, wg, bg, wu, bu, wd, bd):
    src_tok, pos, tile_e, tile_ok = _moe_route(top_idx)
    xs = jnp.take(t_bf16, src_tok, axis=0)
    ys = _moe_grouped_ffn(xs, tile_e, tile_ok, wg, bg, wu, bu, wd, bd)
    yg = jnp.take(ys, pos, axis=0).reshape(t_bf16.shape[0], TOP_K * t_bf16.shape[1])
    return _moe_combine(yg, probs)


def _rmsnorm(t, w):
    tf = t.astype(F32)
    tf = tf * lax.rsqrt(jnp.mean(tf * tf, axis=-1, keepdims=True) + RMS_EPS)
    return tf * w.astype(F32)


def _l2norm(t):
    tf = t.astype(F32)
    return tf * lax.rsqrt(jnp.sum(tf * tf, axis=-1, keepdims=True) + RMS_EPS)


def _modulate(t, shift, scale):
    return t * (1.0 + scale) + shift


def _grid_dwconv(t, w, n_rows, n_cols):
    b, l, ch = t.shape
    img = t.reshape(b, n_rows, n_cols, ch)
    out = lax.conv_general_dilated(img, w[:, :, None, :], window_strides=(1, 1), padding='SAME',
                                   dimension_numbers=('NHWC', 'HWIO', 'NHWC'), feature_group_count=ch)
    return out.reshape(b, l, ch)


def _gated_delta_chunked(q, k, v, g, beta, s0):
    b, l, h, dk = q.shape
    c = GDN_CHUNK
    n = l // c

    def chunks(t):
        return t.astype(F32).reshape(b, n, c, h, -1).transpose(1, 0, 3, 2, 4)

    qc = chunks(q) * (dk ** -0.5)
    kc, vc = chunks(k), chunks(v)
    gc = jnp.cumsum(g.astype(F32).reshape(b, n, c, h).transpose(1, 0, 3, 2), axis=-1)
    bc = beta.astype(F32).reshape(b, n, c, h).transpose(1, 0, 3, 2)
    causal = jnp.tril(jnp.ones((c, c), bool))
    strict = jnp.tril(jnp.ones((c, c), bool), -1)
    seg = gc[..., :, None] - gc[..., None, :]
    decay = jnp.where(causal, jnp.exp(jnp.where(causal, seg, 0.0)), 0.0)
    kb = kc * bc[..., None]
    m = jnp.where(strict, jnp.einsum('nbhid,nbhjd->nbhij', kb, kc) * decay, 0.0)
    tri = jnp.eye(c, dtype=F32) + m
    u = lax.linalg.triangular_solve(tri, vc * bc[..., None], left_side=True, lower=True, unit_diagonal=True)
    w = lax.linalg.triangular_solve(tri, kb * jnp.exp(gc)[..., None], left_side=True, lower=True, unit_diagonal=True)
    qk = jnp.einsum('nbhid,nbhjd->nbhij', qc, kc) * decay
    q_dec = qc * jnp.exp(gc)[..., None]
    k_dec = kc * jnp.exp(gc[..., -1:] - gc)[..., None]
    g_last = jnp.exp(gc[..., -1])

    def step(s, inp):
        u_i, w_i, qk_i, qd_i, kd_i, gl_i = inp
        v_new = u_i - w_i @ s
        o_i = qd_i @ s + qk_i @ v_new
        s = s * gl_i[..., None, None] + jnp.swapaxes(kd_i, -1, -2) @ v_new
        return s, o_i

    s_fin, o = lax.scan(step, s0.astype(F32), (u, w, qk, q_dec, k_dec, g_last))
    o = o.transpose(1, 0, 3, 2, 4).reshape(b, l, h, -1)
    return o, s_fin


def _ssd_chunked(x, dt, da, bm, cm, s0):
    b, l, h, p = x.shape
    g, nst = bm.shape[2], bm.shape[3]
    r = h // g
    c = SSD_CHUNK
    n = l // c
    xc = (x.astype(F32) * dt.astype(F32)[..., None]).reshape(b, n, c, g, r, p)
    acs = jnp.cumsum(da.astype(F32).reshape(b, n, c, g, r), axis=2)
    bc = bm.astype(F32).reshape(b, n, c, g, nst)
    cc = cm.astype(F32).reshape(b, n, c, g, nst)
    causal = jnp.tril(jnp.ones((c, c), bool))[:, :, None, None]
    seg = acs[:, :, :, None] - acs[:, :, None, :]
    lmat = jnp.where(causal, jnp.exp(jnp.where(causal, seg, 0.0)), 0.0)
    scores = jnp.einsum('bnlgd,bnsgd->bnlsg', cc, bc)[..., None] * lmat
    y_diag = jnp.einsum('bnlsgr,bnsgrp->bnlgrp', scores, xc)
    decay_out = jnp.exp(acs[:, :, -1:] - acs)
    chunk_states = jnp.einsum('bnsgd,bnsgrp->bngrpd', bc, xc * decay_out[..., None])
    states = jnp.concatenate([s0.astype(F32).reshape(b, 1, g, r, p, nst), chunk_states], axis=1)
    cum = jnp.cumsum(jnp.pad(acs[:, :, -1], ((0, 0), (1, 0), (0, 0), (0, 0))), axis=1)
    tri = jnp.tril(jnp.ones((n + 1, n + 1), bool))[:, :, None, None]
    segc = cum[:, :, None] - cum[:, None, :]
    dec = jnp.where(tri, jnp.exp(jnp.where(tri, segc, 0.0)), 0.0)
    new = jnp.einsum('bzcgr,bcgrpd->bzgrpd', dec, states)
    y_off = jnp.einsum('bnlgd,bngrpd->bnlgrp', cc, new[:, :-1]) * jnp.exp(acs)[..., None]
    y = (y_diag + y_off).reshape(b, l, h, p)
    return y, new[:, -1].reshape(b, h, p, nst)


def _bidirectional(scan_fn, ctx_dirs, lat_dirs, s0):
    flip = lambda t: jnp.flip(t, axis=1)
    y_ctx, y_lat = 0.0, 0.0
    for d in range(2):
        rev = d == 1
        cargs = tuple(flip(t) for t in ctx_dirs[d]) if rev else ctx_dirs[d]
        largs = tuple(flip(t) for t in lat_dirs[d]) if rev else lat_dirs[d]
        o_c, s_c = scan_fn(*cargs, s0)
        o_l, _ = scan_fn(*largs, s_c)
        y_ctx = y_ctx + (flip(o_c) if rev else o_c)
        y_lat = y_lat + (flip(o_l) if rev else o_l)
    return y_ctx, y_lat


def _hybrid_mixer(h_ctx, h_lat, rows, w_in, gdn_conv_w, gdn_a_log, gdn_dt_bias, gdn_norm_w,
                  ssd_conv_w, ssd_conv_b, ssd_a_log, ssd_dt_bias, ssd_d, ssd_norm_w, w_out):
    splits = np.cumsum(IN0_SIZES)[:-1].tolist()

    def features(h, n_rows, n_cols):
        b, l, _ = h.shape
        conv_a, conv_b, z_a, z_b, a_raw, b_raw, dt_raw = jnp.split(h @ w_in, splits, axis=-1)
        conv_a = jax.nn.silu(_grid_dwconv(conv_a, gdn_conv_w, n_rows, n_cols))
        conv_b = jax.nn.silu(_grid_dwconv(conv_b, ssd_conv_w, n_rows, n_cols) + ssd_conv_b)
        q, k, v = jnp.split(conv_a, [GDN_HEADS * GDN_DK, 2 * GDN_HEADS * GDN_DK], axis=-1)
        q = _l2norm(q.reshape(b, l, GDN_HEADS, GDN_DK))
        k = _l2norm(k.reshape(b, l, GDN_HEADS, GDN_DK))
        v = v.reshape(b, l, GDN_HEADS, GDN_DV)
        g = -jnp.exp(gdn_a_log) * jax.nn.softplus(a_raw.reshape(b, l, 2, GDN_HEADS) + gdn_dt_bias)
        beta = jax.nn.sigmoid(b_raw.reshape(b, l, 2, GDN_HEADS))
        xs, bm, cm = jnp.split(conv_b, [SSD_INNER, SSD_INNER + SSD_GROUPS * SSD_STATE], axis=-1)
        xs = xs.reshape(b, l, SSD_HEADS, SSD_HEADDIM)
        bm = bm.reshape(b, l, SSD_GROUPS, SSD_STATE)
        cm = cm.reshape(b, l, SSD_GROUPS, SSD_STATE)
        dt = jax.nn.softplus(dt_raw.reshape(b, l, 2, SSD_HEADS) + ssd_dt_bias)
        da = -jnp.exp(ssd_a_log) * dt
        gdn_dirs = tuple((q, k, v, g[:, :, d], beta[:, :, d]) for d in range(2))
        ssd_dirs = tuple((xs, dt[:, :, d], da[:, :, d], bm, cm) for d in range(2))
        return gdn_dirs, ssd_dirs, xs, z_a, z_b

    b = h_lat.shape[0]
    gdn_c, ssd_c, xs_c, za_c, zb_c = features(h_ctx, 1, h_ctx.shape[1])
    gdn_l, ssd_l, xs_l, za_l, zb_l = features(h_lat, rows, GRID_W)
    o_c, o_l = _bidirectional(_gated_delta_chunked, gdn_c, gdn_l, jnp.zeros((b, GDN_HEADS, GDN_DK, GDN_DV), F32))
    y_c, y_l = _bidirectional(_ssd_chunked, ssd_c, ssd_l, jnp.zeros((b, SSD_HEADS, SSD_HEADDIM, SSD_STATE), F32))

    def merge(o, y, xs, z_a, z_b):
        bb, l = o.shape[0], o.shape[1]
        o = _rmsnorm(o, gdn_norm_w) * jax.nn.silu(z_a).reshape(bb, l, GDN_HEADS, GDN_DV)
        y = (y + ssd_d[:, None] * xs).reshape(bb, l, SSD_INNER) * jax.nn.silu(z_b)
        y = _rmsnorm(y.reshape(bb, l, SSD_GROUPS, -1), ssd_norm_w.reshape(SSD_GROUPS, -1))
        mixed = jnp.concatenate([o.reshape(bb, l, -1), y.reshape(bb, l, -1)], axis=-1)
        return mixed @ w_out

    return merge(o_c, y_c, xs_c, za_c, zb_c), merge(o_l, y_l, xs_l, za_l, zb_l)


def _s5_discretize(a_re, a_im, log_step, b_re, b_im):
    step = jnp.exp(log_step)[:, None]
    mag = jnp.exp(a_re * step)
    lam_re, lam_im = mag * jnp.cos(a_im * step), mag * jnp.sin(a_im * step)
    den = a_re * a_re + a_im * a_im
    f_re = ((lam_re - 1.0) * a_re + lam_im * a_im) / den
    f_im = (lam_im * a_re - (lam_re - 1.0) * a_im) / den
    bb_re = f_re[..., None] * b_re - f_im[..., None] * b_im
    bb_im = f_re[..., None] * b_im + f_im[..., None] * b_re
    return lam_re, lam_im, bb_re, bb_im


def _s5_scan(lam_re, lam_im, bb_re, bb_im, u, h0_re, h0_im, reverse):
    bu_re = jnp.einsum('gph,blgh->blgp', bb_re, u)
    bu_im = jnp.einsum('gph,blgh->blgp', bb_im, u)
    first = -1 if reverse else 0
    bu_re = bu_re.at[:, first].add(lam_re * h0_re - lam_im * h0_im)
    bu_im = bu_im.at[:, first].add(lam_re * h0_im + lam_im * h0_re)
    l = u.shape[1]
    a_re = jnp.broadcast_to(lam_re, (1, l) + lam_re.shape)
    a_im = jnp.broadcast_to(lam_im, (1, l) + lam_im.shape)

    def combine(e1, e2):
        a1r, a1i, b1r, b1i = e1
        a2r, a2i, b2r, b2i = e2
        return (a2r * a1r - a2i * a1i, a2r * a1i + a2i * a1r,
                a2r * b1r - a2i * b1i + b2r, a2r * b1i + a2i * b1r + b2i)

    _, _, x_re, x_im = lax.associative_scan(combine, (a_re, a_im, bu_re, bu_im), reverse=reverse, axis=1)
    return x_re, x_im


def _s5_readout(x_re, x_im, c_re, c_im):
    y = jnp.einsum('ghp,blgp->blgh', c_re, x_re) - jnp.einsum('ghp,blgp->blgh', c_im, x_im)
    return y.reshape(y.shape[0], y.shape[1], -1)


def _s5_mixer(h_ctx, h_lat, w_in, a_re, a_im, log_step, b_re, b_im, c_re, c_im, d_skip, w_glu_a, w_glu_b):
    b = h_lat.shape[0]
    u_c = h_ctx @ w_in
    u_l = h_lat @ w_in
    ug_c = u_c.reshape(b, u_c.shape[1], S5_GROUPS, S5_GROUP)
    ug_l = u_l.reshape(b, u_l.shape[1], S5_GROUPS, S5_GROUP)
    y_l = d_skip * u_l
    zero = jnp.zeros((b, S5_GROUPS, S5_STATE), F32)
    for d in range(2):
        rev = d == 1
        lam_re, lam_im, bb_re, bb_im = _s5_discretize(a_re[d], a_im[d], log_step[d], b_re[d], b_im[d])
        xc_re, xc_im = _s5_scan(lam_re, lam_im, bb_re, bb_im, ug_c, zero, zero, rev)
        end = 0 if rev else -1
        xl_re, xl_im = _s5_scan(lam_re, lam_im, bb_re, bb_im, ug_l, xc_re[:, end], xc_im[:, end], rev)
        y_l = y_l + _s5_readout(xl_re, xl_im, c_re[d], c_im[d])
    o = jax.nn.gelu(y_l)
    return (o @ w_glu_a) * jax.nn.sigmoid(o @ w_glu_b)


def _route_and_ffn(t, router_w, router_b, wg, bg, wu, bu, wd, bd):
    shp = t.shape
    t2 = t.reshape(-1, shp[-1])
    logits = t2 @ router_w + router_b
    top_val, top_idx = lax.top_k(logits, TOP_K)
    probs = jax.nn.softmax(top_val, axis=-1)
    f = _moe_ffn(t2.astype(BF16), top_idx.astype(jnp.int32), probs, wg, bg, wu, bu, wd, bd)
    return f.reshape(shp)


def kernel(x, c, ctx, c_ctx, ada_w, ada_b, mix_norm_pre, mix_norm_post, ffn_norm_pre, ffn_norm_post, router_w, router_b, moe_w_gate, moe_b_gate, moe_w_up, moe_b_up, moe_w_down, moe_b_down, hy_w_in, gdn_conv_w, gdn_a_log, gdn_dt_bias, gdn_norm_w, ssd_conv_w, ssd_conv_b, ssd_a_log, ssd_dt_bias, ssd_d, ssd_norm_w, hy_w_out, s5_w_in, s5_a_re, s5_a_im, s5_log_step, s5_b_re, s5_b_im, s5_c_re, s5_c_im, s5_d, s5_w_glu_a, s5_w_glu_b):
    depth = ada_w.shape[0]
    rows = x.shape[1] // GRID_W
    h_lat, h_ctx = x, ctx
    cond_lat = jax.nn.silu(c)
    cond_ctx = jax.nn.silu(c_ctx)
    for i in range(depth):
        j = i // 2
        need_ctx = i < depth - 1
        m_lat = jnp.split((cond_lat @ ada_w[i] + ada_b[i])[:, None, :], 6, axis=-1)
        m_ctx = jnp.split(cond_ctx @ ada_w[i] + ada_b[i], 6, axis=-1)
        a_lat = _modulate(_rmsnorm(h_lat, mix_norm_pre[i]), m_lat[0], m_lat[1])
        a_ctx = _modulate(_rmsnorm(h_ctx, mix_norm_pre[i]), m_ctx[0], m_ctx[1])
        if i % 2 == 0:
            y_ctx, y_lat = _hybrid_mixer(a_ctx, a_lat, rows, hy_w_in[j], gdn_conv_w[j], gdn_a_log[j], gdn_dt_bias[j],
                                         gdn_norm_w[j], ssd_conv_w[j], ssd_conv_b[j], ssd_a_log[j], ssd_dt_bias[j],
                                         ssd_d[j], ssd_norm_w[j], hy_w_out[j])
        else:
            y_ctx = None
            y_lat = _s5_mixer(a_ctx, a_lat, s5_w_in[j], s5_a_re[j], s5_a_im[j], s5_log_step[j], s5_b_re[j],
                              s5_b_im[j], s5_c_re[j], s5_c_im[j], s5_d[j], s5_w_glu_a[j], s5_w_glu_b[j])
        ffn = (router_w[i], router_b[i], moe_w_gate[i].astype(BF16), moe_b_gate[i], moe_w_up[i].astype(BF16),
               moe_b_up[i], moe_w_down[i].astype(BF16), moe_b_down[i])
        h_lat = h_lat + m_lat[2] * _rmsnorm(y_lat, mix_norm_post[i])
        t_lat = _modulate(_rmsnorm(h_lat, ffn_norm_pre[i]), m_lat[3], m_lat[4])
        if need_ctx:
            h_ctx = h_ctx + m_ctx[2] * _rmsnorm(y_ctx, mix_norm_post[i])
            t_ctx = _modulate(_rmsnorm(h_ctx, ffn_norm_pre[i]), m_ctx[3], m_ctx[4])
            nl = t_lat.shape[0] * t_lat.shape[1]
            t_all = jnp.concatenate([t_lat.reshape(nl, -1), t_ctx.reshape(-1, t_ctx.shape[-1])], axis=0)
            f_all = _route_and_ffn(t_all, *ffn)
            f_lat = f_all[:nl].reshape(t_lat.shape)
            f_ctx = f_all[nl:].reshape(t_ctx.shape)
            h_ctx = h_ctx + m_ctx[5] * _rmsnorm(f_ctx, ffn_norm_post[i])
        else:
            f_lat = _route_and_ffn(t_lat, *ffn)
        h_lat = h_lat + m_lat[5] * _rmsnorm(f_lat, ffn_norm_post[i])
    return h_lat
```

```python
import functools
import math

import jax
import jax.numpy as jnp
import numpy as np
from jax import lax
from jax.experimental import pallas as pl
from jax.experimental.pallas import tpu as pltpu

F32 = jnp.float32
BF16 = jnp.bfloat16

D_MODEL = 1024
GRID_W = 64
RMS_EPS = 1e-6

GDN_HEADS = 4
GDN_DK = 128
GDN_DV = 128
GDN_CHUNK = 64
SSD_HEADS = 8
SSD_HEADDIM = 64
SSD_GROUPS = 2
SSD_STATE = 128
SSD_CHUNK = 128
S5_GROUP = 16
S5_GROUPS = D_MODEL // S5_GROUP
S5_STATE = 64
N_EXPERTS = 32
TOP_K = 4
SWIGLU_LIMIT = 7.0
SWIGLU_ALPHA = 1.702

GDN_V_WIDTH = GDN_HEADS * GDN_DV
SSD_INNER = SSD_HEADS * SSD_HEADDIM
GDN_CONV_CH = 2 * GDN_HEADS * GDN_DK + GDN_V_WIDTH
SSD_CONV_CH = SSD_INNER + 2 * SSD_GROUPS * SSD_STATE
IN0_SIZES = (GDN_CONV_CH, SSD_CONV_CH, GDN_V_WIDTH, SSD_INNER, 2 * GDN_HEADS, 2 * GDN_HEADS, 2 * SSD_HEADS)

V7X_VMEM_LIMIT_BYTES = 56 * 1024 * 1024
MOE_TILE_M = 512
MOE_TILE_F = 512


def _moe_ffn_body(tile_e_ref, tile_ok_ref, x_ref, wg_ref, bg_ref, wu_ref, bu_ref, wd_ref, bd_ref, o_ref):
    i = pl.program_id(0)

    @pl.when(tile_ok_ref[i] > 0)
    def _():
        x = x_ref[...]
        acc = jnp.zeros(o_ref.shape, F32)
        n_f = wg_ref.shape[2] // MOE_TILE_F
        for c in range(n_f):
            cs = slice(c * MOE_TILE_F, (c + 1) * MOE_TILE_F)
            gl = jnp.dot(x, wg_ref[0, :, cs], preferred_element_type=F32) + bg_ref[0, :, cs]
            lin = jnp.dot(x, wu_ref[0, :, cs], preferred_element_type=F32) + bu_ref[0, :, cs]
            gl = jnp.minimum(gl, SWIGLU_LIMIT)
            lin = jnp.clip(lin, -SWIGLU_LIMIT, SWIGLU_LIMIT)
            act = gl * jax.nn.sigmoid(SWIGLU_ALPHA * gl) * (lin + 1.0)
            acc = acc + jnp.dot(act.astype(BF16), wd_ref[0, cs, :], preferred_element_type=F32)
        o_ref[...] = (acc + bd_ref[0]).astype(o_ref.dtype)


def _moe_grouped_ffn(xs, tile_e, tile_ok, wg, bg, wu, bu, wd, bd):
    p, d = xs.shape
    e, _, f = wg.shape
    n_tiles = p // MOE_TILE_M
    grid_spec = pltpu.PrefetchScalarGridSpec(
        num_scalar_prefetch=2,
        grid=(n_tiles,),
        in_specs=[
            pl.BlockSpec((MOE_TILE_M, d), lambda i, te, ok: (i, 0)),
            pl.BlockSpec((1, d, f), lambda i, te, ok: (te[i], 0, 0)),
            pl.BlockSpec((1, 1, f), lambda i, te, ok: (te[i], 0, 0)),
            pl.BlockSpec((1, d, f), lambda i, te, ok: (te[i], 0, 0)),
            pl.BlockSpec((1, 1, f), lambda i, te, ok: (te[i], 0, 0)),
            pl.BlockSpec((1, f, d), lambda i, te, ok: (te[i], 0, 0)),
            pl.BlockSpec((1, 1, d), lambda i, te, ok: (te[i], 0, 0)),
        ],
        out_specs=pl.BlockSpec((MOE_TILE_M, d), lambda i, te, ok: (i, 0)),
    )
    return pl.pallas_call(
        _moe_ffn_body,
        grid_spec=grid_spec,
        out_shape=jax.ShapeDtypeStruct((p, d), BF16),
        compiler_params=pltpu.CompilerParams(
            dimension_semantics=("arbitrary",), vmem_limit_bytes=V7X_VMEM_LIMIT_BYTES),
        name="moe_grouped_ffn",
    )(tile_e, tile_ok, xs, wg, bg.reshape(e, 1, f), wu, bu.reshape(e, 1, f), wd, bd.reshape(e, 1, d))


def _moe_combine_body(yg_ref, p_ref, o_ref):
    d = o_ref.shape[1]
    p = p_ref[...]
    acc = jnp.zeros(o_ref.shape, F32)
    for k in range(TOP_K):
        acc = acc + p[:, k:k + 1] * yg_ref[:, k * d:(k + 1) * d].astype(F32)
    o_ref[...] = acc


def _moe_combine(yg, probs):
    t, kd = yg.shape
    d = kd // TOP_K
    tm = 512
    return pl.pallas_call(
        _moe_combine_body,
        grid=(t // tm,),
        in_specs=[pl.BlockSpec((tm, kd), lambda i: (i, 0)), pl.BlockSpec((tm, TOP_K), lambda i: (i, 0))],
        out_specs=pl.BlockSpec((tm, d), lambda i: (i, 0)),
        out_shape=jax.ShapeDtypeStruct((t, d), F32),
        compiler_params=pltpu.CompilerParams(dimension_semantics=("arbitrary",)),
        name="moe_combine",
    )(yg, probs)


def _moe_route(top_idx):
    t = top_idx.shape[0]
    a = t * TOP_K
    tm = MOE_TILE_M
    eid = top_idx.reshape(a)
    order = jnp.argsort(eid, stable=True).astype(jnp.int32)
    inv = jnp.argsort(order).astype(jnp.int32)
    counts = jnp.sum(jax.nn.one_hot(eid, N_EXPERTS, dtype=jnp.int32), axis=0)
    off = jnp.cumsum(counts) - counts
    pcounts = ((counts + tm - 1) // tm) * tm
    pend = jnp.cumsum(pcounts)
    poff = pend - pcounts
    n_tiles = a // tm + N_EXPERTS
    tile_start = jnp.arange(n_tiles, dtype=jnp.int32) * tm
    tile_e = jnp.minimum(jnp.searchsorted(pend, tile_start, side='right'), N_EXPERTS - 1).astype(jnp.int32)
    tile_ok = (tile_start < pend[-1]).astype(jnp.int32)
    ppos = jnp.arange(n_tiles * tm, dtype=jnp.int32)
    pe = jnp.repeat(tile_e, tm)
    r = ppos - poff[pe]
    src_rank = jnp.clip(off[pe] + jnp.minimum(r, counts[pe] - 1), 0, a - 1)
    src_tok = order[src_rank] // TOP_K
    pos = poff[eid] + (inv - off[eid])
    return src_tok, pos, tile_e, tile_ok


def _moe_ffn(t_bf16, top_idx, probs, wg, bg, wu, bu, wd, bd):
    src_tok, pos, tile_e, tile_ok = _moe_route(top_idx)
    xs = jnp.take(t_bf16, src_tok, axis=0)
    ys = _moe_grouped_ffn(xs, tile_e, tile_ok, wg, bg, wu, bu, wd, bd)
    yg = jnp.take(ys, pos, axis=0).reshape(t_bf16.shape[0], TOP_K * t_bf16.shape[1])
    return _moe_combine(yg, probs)


def _k_rms(t, w):
    return t * lax.rsqrt(jnp.mean(t * t, axis=-1, keepdims=True) + RMS_EPS) * w


def _k_post_mix(y, h, gate, post_w, pre_w, shift, scale, rw, rb):
    h1 = h + gate * _k_rms(y, post_w)
    t = _k_rms(h1, pre_w) * (1.0 + scale) + shift
    logits = jnp.dot(t, rw, preferred_element_type=F32) + rb
    lane = lax.broadcasted_iota(jnp.int32, logits.shape, 1)
    idx_out = jnp.zeros(logits.shape, jnp.int32)
    val_out = jnp.zeros(logits.shape, F32)
    work = logits
    m0 = None
    for k in range(TOP_K):
        m = jnp.max(work, axis=-1, keepdims=True)
        sel = jnp.min(jnp.where(work == m, lane, 128), axis=-1, keepdims=True)
        if k == 0:
            m0 = m
        idx_out = jnp.where(lane == k, sel, idx_out)
        val_out = jnp.where(lane == k, jnp.exp(m - m0), val_out)
        work = jnp.where(lane == sel, -jnp.inf, work)
    probs = val_out / jnp.sum(val_out, axis=-1, keepdims=True)
    return h1, t, idx_out, probs


S5_BATCH_SUB = 8
S5_CH_BLOCK = 128
S5_ST_BLOCK = (S5_CH_BLOCK // S5_GROUP) * S5_STATE
S5_TL = 256


def _s5_inproj_body(h_ref, sh_ref, sc_ref, nw_ref, w_ref, u_ref):
    a = _k_rms(h_ref[0], nw_ref[...]) * (1.0 + sc_ref[0]) + sh_ref[0]
    u_ref[0] = jnp.dot(a.astype(BF16), w_ref[...], preferred_element_type=F32)


def _s5_inproj(h, mods, norm_w, w_bf16, tl):
    b, l, d = h.shape
    nb = mods.shape[0]
    bsel = (lambda bi: bi) if nb > 1 else (lambda bi: 0)
    return pl.pallas_call(
        _s5_inproj_body,
        grid=(b, l // tl),
        in_specs=[
            pl.BlockSpec((1, tl, d), lambda bi, ti: (bi, ti, 0)),
            pl.BlockSpec((1, 1, d), lambda bi, ti: (bsel(bi), 0, 0)),
            pl.BlockSpec((1, 1, d), lambda bi, ti: (bsel(bi), 0, 1)),
            pl.BlockSpec((1, d), lambda bi, ti: (0, 0)),
            pl.BlockSpec((d, d), lambda bi, ti: (0, 0)),
        ],
        out_specs=pl.BlockSpec((1, tl, d), lambda bi, ti: (bi // S5_BATCH_SUB, ti, bi % S5_BATCH_SUB)),
        out_shape=jax.ShapeDtypeStruct((b // S5_BATCH_SUB, l, S5_BATCH_SUB * d), F32),
        compiler_params=pltpu.CompilerParams(
            dimension_semantics=("arbitrary", "arbitrary"), vmem_limit_bytes=V7X_VMEM_LIMIT_BYTES),
        name="s5_inproj",
    )(h, mods, mods, norm_w.reshape(1, d), w_bf16)


def _s5_scan_body(uc_ref, ul_ref, b_ref, c_ref, lam_ref, y_ref, buf_ref, st_ref):
    d = pl.program_id(0)
    tc = pl.program_id(3)
    tl = ul_ref.shape[1]
    rows = tl * S5_BATCH_SUB
    ns = S5_ST_BLOCK

    @pl.when(tc == 0)
    def _():
        st_ref[...] = jnp.zeros(st_ref.shape, F32)
        u2 = uc_ref[0].reshape(rows, S5_CH_BLOCK).astype(BF16)
        buf_ref[...] = jnp.dot(u2, b_ref[0, 0], preferred_element_type=F32)

    @pl.when(tc > 0)
    def _():
        u2 = ul_ref[0].reshape(rows, S5_CH_BLOCK).astype(BF16)
        buf_ref[...] = jnp.dot(u2, b_ref[0, 0], preferred_element_type=F32)

    lam = lam_ref[0, 0]
    lr = jnp.broadcast_to(lam[0:1], (S5_BATCH_SUB, ns))
    li = jnp.broadcast_to(lam[1:2], (S5_BATCH_SUB, ns))

    def step(i, carry):
        xr, xi = carry
        t = jnp.where(d == 0, i, tl - 1 - i)
        r0 = pl.multiple_of(t * S5_BATCH_SUB, S5_BATCH_SUB)
        br = buf_ref[pl.ds(r0, S5_BATCH_SUB), 0:ns]
        bi = buf_ref[pl.ds(r0, S5_BATCH_SUB), ns:2 * ns]
        nr = lr * xr - li * xi + br
        ni = lr * xi + li * xr + bi
        buf_ref[pl.ds(r0, S5_BATCH_SUB), 0:ns] = nr
        buf_ref[pl.ds(r0, S5_BATCH_SUB), ns:2 * ns] = ni
        return nr, ni

    xr, xi = lax.fori_loop(0, tl, step, (st_ref[0], st_ref[1]), unroll=8)
    st_ref[0] = xr
    st_ref[1] = xi

    @pl.when(tc > 0)
    def _():
        y = jnp.dot(buf_ref[...].astype(BF16), c_ref[0, 0], preferred_element_type=F32)
        y_ref[0, 0] = y.reshape(tl, S5_BATCH_SUB, S5_CH_BLOCK)


def _s5_scan(u_ctx, u_lat, bblk, cblk, lam):
    nbg, l_lat, _, d = u_lat.shape
    assert u_ctx.shape[1] == S5_TL and l_lat % S5_TL == 0
    n_lc = l_lat // S5_TL
    n_sb = d // S5_CH_BLOCK

    def lat_chunk(di, tc):
        j = jnp.maximum(tc - 1, 0)
        return jnp.where(di == 0, j, n_lc - 1 - j)

    return pl.pallas_call(
        _s5_scan_body,
        grid=(2, nbg, n_sb, n_lc + 1),
        in_specs=[
            pl.BlockSpec((1, S5_TL, S5_BATCH_SUB, S5_CH_BLOCK), lambda di, bg, sb, tc: (bg, 0, 0, sb)),
            pl.BlockSpec((1, S5_TL, S5_BATCH_SUB, S5_CH_BLOCK), lambda di, bg, sb, tc: (bg, lat_chunk(di, tc), 0, sb)),
            pl.BlockSpec((1, 1, S5_CH_BLOCK, 2 * S5_ST_BLOCK), lambda di, bg, sb, tc: (di, sb, 0, 0)),
            pl.BlockSpec((1, 1, 2 * S5_ST_BLOCK, S5_CH_BLOCK), lambda di, bg, sb, tc: (di, sb, 0, 0)),
            pl.BlockSpec((1, 1, 2, S5_ST_BLOCK), lambda di, bg, sb, tc: (di, sb, 0, 0)),
        ],
        out_specs=pl.BlockSpec((1, 1, S5_TL, S5_BATCH_SUB, S5_CH_BLOCK),
                               lambda di, bg, sb, tc: (di, bg, lat_chunk(di, tc), 0, sb)),
        out_shape=jax.ShapeDtypeStruct((2, nbg, l_lat, S5_BATCH_SUB, d), F32),
        scratch_shapes=[
            pltpu.VMEM((S5_TL * S5_BATCH_SUB, 2 * S5_ST_BLOCK), F32),
            pltpu.VMEM((2, S5_BATCH_SUB, S5_ST_BLOCK), F32),
        ],
        compiler_params=pltpu.CompilerParams(
            dimension_semantics=("arbitrary", "arbitrary", "arbitrary", "arbitrary"),
            vmem_limit_bytes=V7X_VMEM_LIMIT_BYTES),
        name="s5_scan",
    )(u_ctx, u_lat, bblk, cblk, lam)


def _s5_glu_body(y0_ref, y1_ref, u_ref, h_ref, gate_ref, sh_ref, sc_ref, dsk_ref, wa_ref, wb_ref,
                 postw_ref, prew_ref, rw_ref, rb_ref, h1_ref, t_ref, idx_ref, p_ref):
    y = dsk_ref[...] * u_ref[0] + y0_ref[0, 0] + y1_ref[0, 0]
    o = (0.5 * y * (1.0 + jnp.tanh(math.sqrt(2.0 / math.pi) * (y + 0.044715 * (y * y * y))))).astype(BF16)
    ga = jnp.dot(o, wa_ref[...], preferred_element_type=F32)
    gb = jnp.dot(o, wb_ref[...], preferred_element_type=F32)
    ym = ga * jax.nn.sigmoid(gb)
    h1, t, idx, probs = _k_post_mix(ym, h_ref[0], gate_ref[0], postw_ref[...], prew_ref[...],
                                    sh_ref[0], sc_ref[0], rw_ref[...], rb_ref[...])
    h1_ref[0] = h1
    t_ref[0] = t.astype(BF16)
    idx_ref[0] = idx
    p_ref[0] = probs


def _s5_glu_postmix(y, u, h, mods, d_skip, wa, wb, post_w, pre_w, rw_pad, rb_pad, tl):
    b, l, d = h.shape
    nbg = b // S5_BATCH_SUB
    y4 = y.reshape(2, nbg, l, S5_BATCH_SUB * d)
    u3 = u.reshape(nbg, l, S5_BATCH_SUB * d)
    row = lambda v: v.reshape(1, -1)
    full = lambda shape: pl.BlockSpec(shape, lambda bi, ti: (0,) * len(shape))
    mod = lambda k: pl.BlockSpec((1, 1, d), lambda bi, ti: (bi, 0, k))
    tok = lambda w: pl.BlockSpec((1, tl, w), lambda bi, ti: (bi, ti, 0))
    return pl.pallas_call(
        _s5_glu_body,
        grid=(b, l // tl),
        in_specs=[
            pl.BlockSpec((1, 1, tl, d), lambda bi, ti: (0, bi // S5_BATCH_SUB, ti, bi % S5_BATCH_SUB)),
            pl.BlockSpec((1, 1, tl, d), lambda bi, ti: (1, bi // S5_BATCH_SUB, ti, bi % S5_BATCH_SUB)),
            pl.BlockSpec((1, tl, d), lambda bi, ti: (bi // S5_BATCH_SUB, ti, bi % S5_BATCH_SUB)),
            tok(d), mod(2), mod(3), mod(4),
            full((1, d)), full((d, d)), full((d, d)), full((1, d)), full((1, d)), full((d, 128)), full((1, 128)),
        ],
        out_specs=[tok(d), tok(d), tok(128), tok(128)],
        out_shape=[
            jax.ShapeDtypeStruct((b, l, d), F32),
            jax.ShapeDtypeStruct((b, l, d), BF16),
            jax.ShapeDtypeStruct((b, l, 128), jnp.int32),
            jax.ShapeDtypeStruct((b, l, 128), F32),
        ],
        compiler_params=pltpu.CompilerParams(
            dimension_semantics=("arbitrary", "arbitrary"), vmem_limit_bytes=V7X_VMEM_LIMIT_BYTES),
        name="s5_glu_postmix",
    )(y4, y4, u3, h, mods, mods, mods, row(d_skip), wa, wb, row(post_w), row(pre_w), rw_pad, rb_pad)


def _s5_block_params(a_re, a_im, log_step, b_re, b_im, c_re, c_im):
    gpb = S5_CH_BLOCK // S5_GROUP
    eye = jnp.eye(gpb, dtype=F32)
    bblks, cblks, lams = [], [], []
    for di in range(2):
        lam_re, lam_im, bb_re, bb_im = _s5_discretize(a_re[di], a_im[di], log_step[di], b_re[di], b_im[di])
        n_sb = lam_re.shape[0] // gpb

        def bdiag_in(bb):
            t = bb.reshape(n_sb, gpb, S5_STATE, S5_GROUP)
            return jnp.einsum('sgph,gk->sghkp', t, eye).reshape(n_sb, S5_CH_BLOCK, S5_ST_BLOCK)

        def bdiag_out(cc):
            t = cc.reshape(n_sb, gpb, S5_GROUP, S5_STATE)
            return jnp.einsum('sghp,gk->sgpkh', t, eye).reshape(n_sb, S5_ST_BLOCK, S5_CH_BLOCK)

        bblks.append(jnp.concatenate([bdiag_in(bb_re), bdiag_in(bb_im)], axis=-1))
        cblks.append(jnp.concatenate([bdiag_out(c_re[di]), -bdiag_out(c_im[di])], axis=-2))
        lams.append(jnp.stack([lam_re.reshape(n_sb, S5_ST_BLOCK), lam_im.reshape(n_sb, S5_ST_BLOCK)], axis=1))
    return jnp.stack(bblks).astype(BF16), jnp.stack(cblks).astype(BF16), jnp.stack(lams)


def _pad_router(router_w, router_b):
    e = router_w.shape[1]
    rw = jnp.pad(router_w, ((0, 0), (0, 128 - e)))
    rb = jnp.pad(router_b, (0, 128 - e), constant_values=-1e30).reshape(1, 128)
    return rw, rb


def _rmsnorm(t, w):
    tf = t.astype(F32)
    tf = tf * lax.rsqrt(jnp.mean(tf * tf, axis=-1, keepdims=True) + RMS_EPS)
    return tf * w.astype(F32)


def _l2norm(t):
    tf = t.astype(F32)
    return tf * lax.rsqrt(jnp.sum(tf * tf, axis=-1, keepdims=True) + RMS_EPS)


def _modulate(t, shift, scale):
    return t * (1.0 + scale) + shift


def _grid_dwconv(t, w, n_rows, n_cols):
    b, l, ch = t.shape
    img = t.reshape(b, n_rows, n_cols, ch)
    out = lax.conv_general_dilated(img, w[:, :, None, :], window_strides=(1, 1), padding='SAME',
                                   dimension_numbers=('NHWC', 'HWIO', 'NHWC'), feature_group_count=ch)
    return out.reshape(b, l, ch)


def _gated_delta_chunked(q, k, v, g, beta, s0):
    b, l, h, dk = q.shape
    c = GDN_CHUNK
    n = l // c

    def chunks(t):
        return t.astype(F32).reshape(b, n, c, h, -1).transpose(1, 0, 3, 2, 4)

    qc = chunks(q) * (dk ** -0.5)
    kc, vc = chunks(k), chunks(v)
    gc = jnp.cumsum(g.astype(F32).reshape(b, n, c, h).transpose(1, 0, 3, 2), axis=-1)
    bc = beta.astype(F32).reshape(b, n, c, h).transpose(1, 0, 3, 2)
    causal = jnp.tril(jnp.ones((c, c), bool))
    strict = jnp.tril(jnp.ones((c, c), bool), -1)
    seg = gc[..., :, None] - gc[..., None, :]
    decay = jnp.where(causal, jnp.exp(jnp.where(causal, seg, 0.0)), 0.0)
    kb = kc * bc[..., None]
    m = jnp.where(strict, jnp.einsum('nbhid,nbhjd->nbhij', kb, kc) * decay, 0.0)
    tri = jnp.eye(c, dtype=F32) + m
    u = lax.linalg.triangular_solve(tri, vc * bc[..., None], left_side=True, lower=True, unit_diagonal=True)
    w = lax.linalg.triangular_solve(tri, kb * jnp.exp(gc)[..., None], left_side=True, lower=True, unit_diagonal=True)
    qk = jnp.einsum('nbhid,nbhjd->nbhij', qc, kc) * decay
    q_dec = qc * jnp.exp(gc)[..., None]
    k_dec = kc * jnp.exp(gc[..., -1:] - gc)[..., None]
    g_last = jnp.exp(gc[..., -1])

    def step(s, inp):
        u_i, w_i, qk_i, qd_i, kd_i, gl_i = inp
        v_new = u_i - w_i @ s
        o_i = qd_i @ s + qk_i @ v_new
        s = s * gl_i[..., None, None] + jnp.swapaxes(kd_i, -1, -2) @ v_new
        return s, o_i

    s_fin, o = lax.scan(step, s0.astype(F32), (u, w, qk, q_dec, k_dec, g_last))
    o = o.transpose(1, 0, 3, 2, 4).reshape(b, l, h, -1)
    return o, s_fin


def _ssd_chunked(x, dt, da, bm, cm, s0):
    b, l, h, p = x.shape
    g, nst = bm.shape[2], bm.shape[3]
    r = h // g
    c = SSD_CHUNK
    n = l // c
    xc = (x.astype(F32) * dt.astype(F32)[..., None]).reshape(b, n, c, g, r, p)
    acs = jnp.cumsum(da.astype(F32).reshape(b, n, c, g, r), axis=2)
    bc = bm.astype(F32).reshape(b, n, c, g, nst)
    cc = cm.astype(F32).reshape(b, n, c, g, nst)
    causal = jnp.tril(jnp.ones((c, c), bool))[:, :, None, None]
    seg = acs[:, :, :, None] - acs[:, :, None, :]
    lmat = jnp.where(causal, jnp.exp(jnp.where(causal, seg, 0.0)), 0.0)
    scores = jnp.einsum('bnlgd,bnsgd->bnlsg', cc, bc)[..., None] * lmat
    y_diag = jnp.einsum('bnlsgr,bnsgrp->bnlgrp', scores, xc)
    decay_out = jnp.exp(acs[:, :, -1:] - acs)
    chunk_states = jnp.einsum('bnsgd,bnsgrp->bngrpd', bc, xc * decay_out[..., None])
    states = jnp.concatenate([s0.astype(F32).reshape(b, 1, g, r, p, nst), chunk_states], axis=1)
    cum = jnp.cumsum(jnp.pad(acs[:, :, -1], ((0, 0), (1, 0), (0, 0), (0, 0))), axis=1)
    tri = jnp.tril(jnp.ones((n + 1, n + 1), bool))[:, :, None, None]
    segc = cum[:, :, None] - cum[:, None, :]
    dec = jnp.where(tri, jnp.exp(jnp.where(tri, segc, 0.0)), 0.0)
    new = jnp.einsum('bzcgr,bcgrpd->bzgrpd', dec, states)
    y_off = jnp.einsum('bnlgd,bngrpd->bnlgrp', cc, new[:, :-1]) * jnp.exp(acs)[..., None]
    y = (y_diag + y_off).reshape(b, l, h, p)
    return y, new[:, -1].reshape(b, h, p, nst)


def _bidirectional(scan_fn, ctx_dirs, lat_dirs, s0):
    flip = lambda t: jnp.flip(t, axis=1)
    y_ctx, y_lat = 0.0, 0.0
    for d in range(2):
        rev = d == 1
        cargs = tuple(flip(t) for t in ctx_dirs[d]) if rev else ctx_dirs[d]
        largs = tuple(flip(t) for t in lat_dirs[d]) if rev else lat_dirs[d]
        o_c, s_c = scan_fn(*cargs, s0)
        o_l, _ = scan_fn(*largs, s_c)
        y_ctx = y_ctx + (flip(o_c) if rev else o_c)
        y_lat = y_lat + (flip(o_l) if rev else o_l)
    return y_ctx, y_lat


def _hybrid_mixer(h_ctx, h_lat, rows, w_in, gdn_conv_w, gdn_a_log, gdn_dt_bias, gdn_norm_w,
                  ssd_conv_w, ssd_conv_b, ssd_a_log, ssd_dt_bias, ssd_d, ssd_norm_w, w_out):
    splits = np.cumsum(IN0_SIZES)[:-1].tolist()

    def features(h, n_rows, n_cols):
        b, l, _ = h.shape
        conv_a, conv_b, z_a, z_b, a_raw, b_raw, dt_raw = jnp.split(h @ w_in, splits, axis=-1)
        conv_a = jax.nn.silu(_grid_dwconv(conv_a, gdn_conv_w, n_rows, n_cols))
        conv_b = jax.nn.silu(_grid_dwconv(conv_b, ssd_conv_w, n_rows, n_cols) + ssd_conv_b)
        q, k, v = jnp.split(conv_a, [GDN_HEADS * GDN_DK, 2 * GDN_HEADS * GDN_DK], axis=-1)
        q = _l2norm(q.reshape(b, l, GDN_HEADS, GDN_DK))
        k = _l2norm(k.reshape(b, l, GDN_HEADS, GDN_DK))
        v = v.reshape(b, l, GDN_HEADS, GDN_DV)
        g = -jnp.exp(gdn_a_log) * jax.nn.softplus(a_raw.reshape(b, l, 2, GDN_HEADS) + gdn_dt_bias)
        beta = jax.nn.sigmoid(b_raw.reshape(b, l, 2, GDN_HEADS))
        xs, bm, cm = jnp.split(conv_b, [SSD_INNER, SSD_INNER + SSD_GROUPS * SSD_STATE], axis=-1)
        xs = xs.reshape(b, l, SSD_HEADS, SSD_HEADDIM)
        bm = bm.reshape(b, l, SSD_GROUPS, SSD_STATE)
        cm = cm.reshape(b, l, SSD_GROUPS, SSD_STATE)
        dt = jax.nn.softplus(dt_raw.reshape(b, l, 2, SSD_HEADS) + ssd_dt_bias)
        da = -jnp.exp(ssd_a_log) * dt
        gdn_dirs = tuple((q, k, v, g[:, :, d], beta[:, :, d]) for d in range(2))
        ssd_dirs = tuple((xs, dt[:, :, d], da[:, :, d], bm, cm) for d in range(2))
        return gdn_dirs, ssd_dirs, xs, z_a, z_b

    b = h_lat.shape[0]
    gdn_c, ssd_c, xs_c, za_c, zb_c = features(h_ctx, 1, h_ctx.shape[1])
    gdn_l, ssd_l, xs_l, za_l, zb_l = features(h_lat, rows, GRID_W)
    o_c, o_l = _bidirectional(_gated_delta_chunked, gdn_c, gdn_l, jnp.zeros((b, GDN_HEADS, GDN_DK, GDN_DV), F32))
    y_c, y_l = _bidirectional(_ssd_chunked, ssd_c, ssd_l, jnp.zeros((b, SSD_HEADS, SSD_HEADDIM, SSD_STATE), F32))

    def merge(o, y, xs, z_a, z_b):
        bb, l = o.shape[0], o.shape[1]
        o = _rmsnorm(o, gdn_norm_w) * jax.nn.silu(z_a).reshape(bb, l, GDN_HEADS, GDN_DV)
        y = (y + ssd_d[:, None] * xs).reshape(bb, l, SSD_INNER) * jax.nn.silu(z_b)
        y = _rmsnorm(y.reshape(bb, l, SSD_GROUPS, -1), ssd_norm_w.reshape(SSD_GROUPS, -1))
        mixed = jnp.concatenate([o.reshape(bb, l, -1), y.reshape(bb, l, -1)], axis=-1)
        return mixed @ w_out

    return merge(o_c, y_c, xs_c, za_c, zb_c), merge(o_l, y_l, xs_l, za_l, zb_l)


def _s5_discretize(a_re, a_im, log_step, b_re, b_im):
    step = jnp.exp(log_step)[:, None]
    mag = jnp.exp(a_re * step)
    lam_re, lam_im = mag * jnp.cos(a_im * step), mag * jnp.sin(a_im * step)
    den = a_re * a_re + a_im * a_im
    f_re = ((lam_re - 1.0) * a_re + lam_im * a_im) / den
    f_im = (lam_im * a_re - (lam_re - 1.0) * a_im) / den
    bb_re = f_re[..., None] * b_re - f_im[..., None] * b_im
    bb_im = f_re[..., None] * b_im + f_im[..., None] * b_re
    return lam_re, lam_im, bb_re, bb_im


def _route_and_ffn(t, router_w, router_b, wg, bg, wu, bu, wd, bd):
    shp = t.shape
    t2 = t.reshape(-1, shp[-1])
    logits = t2 @ router_w + router_b
    top_val, top_idx = lax.top_k(logits, TOP_K)
    probs = jax.nn.softmax(top_val, axis=-1)
    f = _moe_ffn(t2.astype(BF16), top_idx.astype(jnp.int32), probs, wg, bg, wu, bu, wd, bd)
    return f.reshape(shp)


def kernel(x, c, ctx, c_ctx, ada_w, ada_b, mix_norm_pre, mix_norm_post, ffn_norm_pre, ffn_norm_post, router_w, router_b, moe_w_gate, moe_b_gate, moe_w_up, moe_b_up, moe_w_down, moe_b_down, hy_w_in, gdn_conv_w, gdn_a_log, gdn_dt_bias, gdn_norm_w, ssd_conv_w, ssd_conv_b, ssd_a_log, ssd_dt_bias, ssd_d, ssd_norm_w, hy_w_out, s5_w_in, s5_a_re, s5_a_im, s5_log_step, s5_b_re, s5_b_im, s5_c_re, s5_c_im, s5_d, s5_w_glu_a, s5_w_glu_b):
    depth = ada_w.shape[0]
    rows = x.shape[1] // GRID_W
    h_lat, h_ctx = x, ctx
    cond_lat = jax.nn.silu(c)
    cond_ctx = jax.nn.silu(c_ctx)
    for i in range(depth):
        j = i // 2
        need_ctx = i < depth - 1
        mods_lat = (cond_lat @ ada_w[i] + ada_b[i])[:, None, :]
        mods_ctx = (cond_ctx @ ada_w[i] + ada_b[i])[None, None, :]
        m_lat = jnp.split(mods_lat, 6, axis=-1)
        m_ctx = jnp.split(mods_ctx[0, 0], 6, axis=-1)
        ffn = (router_w[i], router_b[i], moe_w_gate[i].astype(BF16), moe_b_gate[i], moe_w_up[i].astype(BF16),
               moe_b_up[i], moe_w_down[i].astype(BF16), moe_b_down[i])
        if i % 2 == 1:
            assert not need_ctx
            bsz, l, d = h_lat.shape
            nbg = bsz // S5_BATCH_SUB
            w_in = s5_w_in[j].astype(BF16)
            u_lat = _s5_inproj(h_lat, mods_lat, mix_norm_pre[i], w_in, 512)
            u_ctx = _s5_inproj(h_ctx, mods_ctx, mix_norm_pre[i], w_in, h_ctx.shape[1])
            bblk, cblk, lam = _s5_block_params(s5_a_re[j], s5_a_im[j], s5_log_step[j], s5_b_re[j], s5_b_im[j],
                                               s5_c_re[j], s5_c_im[j])
            y = _s5_scan(u_ctx.reshape(nbg, h_ctx.shape[1], S5_BATCH_SUB, d),
                         u_lat.reshape(nbg, l, S5_BATCH_SUB, d), bblk, cblk, lam)
            rw_pad, rb_pad = _pad_router(router_w[i], router_b[i])
            h_lat, t_lat, idx, probs = _s5_glu_postmix(
                y, u_lat, h_lat, mods_lat, s5_d[j], s5_w_glu_a[j].astype(BF16), s5_w_glu_b[j].astype(BF16),
                mix_norm_post[i], ffn_norm_pre[i], rw_pad, rb_pad, 512)
            f_lat = _moe_ffn(t_lat.reshape(bsz * l, d), idx.reshape(bsz * l, 128)[:, :TOP_K],
                             probs.reshape(bsz * l, 128)[:, :TOP_K], *ffn[2:]).reshape(bsz, l, d)
            h_lat = h_lat + m_lat[5] * _rmsnorm(f_lat, ffn_norm_post[i])
            continue
        a_lat = _modulate(_rmsnorm(h_lat, mix_norm_pre[i]), m_lat[0], m_lat[1])
        a_ctx = _modulate(_rmsnorm(h_ctx, mix_norm_pre[i]), m_ctx[0], m_ctx[1])
        y_ctx, y_lat = _hybrid_mixer(a_ctx, a_lat, rows, hy_w_in[j], gdn_conv_w[j], gdn_a_log[j], gdn_dt_bias[j],
                                     gdn_norm_w[j], ssd_conv_w[j], ssd_conv_b[j], ssd_a_log[j], ssd_dt_bias[j],
                                     ssd_d[j], ssd_norm_w[j], hy_w_out[j])
        h_lat = h_lat + m_lat[2] * _rmsnorm(y_lat, mix_norm_post[i])
        t_lat = _modulate(_rmsnorm(h_lat, ffn_norm_pre[i]), m_lat[3], m_lat[4])
        if need_ctx:
            h_ctx = h_ctx + m_ctx[2] * _rmsnorm(y_ctx, mix_norm_post[i])
            t_ctx = _modulate(_rmsnorm(h_ctx, ffn_norm_pre[i]), m_ctx[3], m_ctx[4])
            nl = t_lat.shape[0] * t_lat.shape[1]
            t_all = jnp.concatenate([t_lat.reshape(nl, -1), t_ctx.reshape(-1, t_ctx.shape[-1])], axis=0)
            f_all = _route_and_ffn(t_all, *ffn)
            f_lat = f_all[:nl].reshape(t_lat.shape)
            f_ctx = f_all[nl:].reshape(t_ctx.shape)
            h_ctx = h_ctx + m_ctx[5] * _rmsnorm(f_ctx, ffn_norm_post[i])
        else:
            f_lat = _route_and_ffn(t_lat, *ffn)
        h_lat = h_lat + m_lat[5] * _rmsnorm(f_lat, ffn_norm_post[i])
    return h_lat
```

```python
import functools
import math

import jax
import jax.numpy as jnp
import numpy as np
from jax import lax
from jax.experimental import pallas as pl
from jax.experimental.pallas import tpu as pltpu

F32 = jnp.float32
BF16 = jnp.bfloat16

D_MODEL = 1024
GRID_W = 64
RMS_EPS = 1e-6

GDN_HEADS = 4
GDN_DK = 128
GDN_DV = 128
GDN_CHUNK = 64
SSD_HEADS = 8
SSD_HEADDIM = 64
SSD_GROUPS = 2
SSD_STATE = 128
SSD_CHUNK = 128
S5_GROUP = 16
S5_GROUPS = D_MODEL // S5_GROUP
S5_STATE = 64
N_EXPERTS = 32
TOP_K = 4
SWIGLU_LIMIT = 7.0
SWIGLU_ALPHA = 1.702

GDN_V_WIDTH = GDN_HEADS * GDN_DV
SSD_INNER = SSD_HEADS * SSD_HEADDIM
GDN_CONV_CH = 2 * GDN_HEADS * GDN_DK + GDN_V_WIDTH
SSD_CONV_CH = SSD_INNER + 2 * SSD_GROUPS * SSD_STATE
IN0_SIZES = (GDN_CONV_CH, SSD_CONV_CH, GDN_V_WIDTH, SSD_INNER, 2 * GDN_HEADS, 2 * GDN_HEADS, 2 * SSD_HEADS)

V7X_VMEM_LIMIT_BYTES = 56 * 1024 * 1024
MOE_TILE_M = 512
MOE_TILE_F = 512


def _moe_ffn_body(tile_e_ref, tile_ok_ref, x_ref, wg_ref, bg_ref, wu_ref, bu_ref, wd_ref, bd_ref, o_ref):
    i = pl.program_id(0)

    @pl.when(tile_ok_ref[i] > 0)
    def _():
        x = x_ref[...]
        acc = jnp.zeros(o_ref.shape, F32)
        n_f = wg_ref.shape[2] // MOE_TILE_F
        for c in range(n_f):
            cs = slice(c * MOE_TILE_F, (c + 1) * MOE_TILE_F)
            gl = jnp.dot(x, wg_ref[0, :, cs], preferred_element_type=F32) + bg_ref[0, :, cs]
            lin = jnp.dot(x, wu_ref[0, :, cs], preferred_element_type=F32) + bu_ref[0, :, cs]
            gl = jnp.minimum(gl, SWIGLU_LIMIT)
            lin = jnp.clip(lin, -SWIGLU_LIMIT, SWIGLU_LIMIT)
            act = gl * jax.nn.sigmoid(SWIGLU_ALPHA * gl) * (lin + 1.0)
            acc = acc + jnp.dot(act.astype(BF16), wd_ref[0, cs, :], preferred_element_type=F32)
        o_ref[...] = (acc + bd_ref[0]).astype(o_ref.dtype)

    @pl.when(tile_ok_ref[i] == 0)
    def _():
        o_ref[...] = jnp.zeros(o_ref.shape, o_ref.dtype)


def _moe_grouped_ffn(xs, tile_e, tile_ok, wg, bg, wu, bu, wd, bd):
    p, d = xs.shape
    e, _, f = wg.shape
    n_tiles = p // MOE_TILE_M
    grid_spec = pltpu.PrefetchScalarGridSpec(
        num_scalar_prefetch=2,
        grid=(n_tiles,),
        in_specs=[
            pl.BlockSpec((MOE_TILE_M, d), lambda i, te, ok: (i, 0)),
            pl.BlockSpec((1, d, f), lambda i, te, ok: (te[i], 0, 0)),
            pl.BlockSpec((1, 1, f), lambda i, te, ok: (te[i], 0, 0)),
            pl.BlockSpec((1, d, f), lambda i, te, ok: (te[i], 0, 0)),
            pl.BlockSpec((1, 1, f), lambda i, te, ok: (te[i], 0, 0)),
            pl.BlockSpec((1, f, d), lambda i, te, ok: (te[i], 0, 0)),
            pl.BlockSpec((1, 1, d), lambda i, te, ok: (te[i], 0, 0)),
        ],
        out_specs=pl.BlockSpec((MOE_TILE_M, d), lambda i, te, ok: (i, 0)),
    )
    return pl.pallas_call(
        _moe_ffn_body,
        grid_spec=grid_spec,
        out_shape=jax.ShapeDtypeStruct((p, d), BF16),
        compiler_params=pltpu.CompilerParams(
            dimension_semantics=("arbitrary",), vmem_limit_bytes=V7X_VMEM_LIMIT_BYTES),
        name="moe_grouped_ffn",
    )(tile_e, tile_ok, xs, wg, bg.reshape(e, 1, f), wu, bu.reshape(e, 1, f), wd, bd.reshape(e, 1, d))


def _moe_combine_body(yg_ref, p_ref, h_ref, gate_ref, nw_ref, o_ref):
    d = o_ref.shape[2]
    p = p_ref[0]
    acc = jnp.zeros(o_ref.shape[1:], F32)
    for k in range(TOP_K):
        acc = acc + p[:, k:k + 1] * yg_ref[0, :, k * d:(k + 1) * d].astype(F32)
    o_ref[0] = h_ref[0] + gate_ref[0] * _k_rms(acc, nw_ref[...])


def _moe_combine(yg, probs, h, mods, norm_w, tl):
    b, l, d = h.shape
    nb = mods.shape[0]
    bsel = (lambda bi: bi) if nb > 1 else (lambda bi: 0)
    return pl.pallas_call(
        _moe_combine_body,
        grid=(b, l // tl),
        in_specs=[pl.BlockSpec((1, tl, TOP_K * d), lambda bi, ti: (bi, ti, 0)),
                  pl.BlockSpec((1, tl, 128), lambda bi, ti: (bi, ti, 0)),
                  pl.BlockSpec((1, tl, d), lambda bi, ti: (bi, ti, 0)),
                  pl.BlockSpec((1, 1, d), lambda bi, ti: (bsel(bi), 0, 5)),
                  pl.BlockSpec((1, d), lambda bi, ti: (0, 0))],
        out_specs=pl.BlockSpec((1, tl, d), lambda bi, ti: (bi, ti, 0)),
        out_shape=jax.ShapeDtypeStruct((b, l, d), F32),
        compiler_params=pltpu.CompilerParams(
            dimension_semantics=("arbitrary", "arbitrary"), vmem_limit_bytes=V7X_VMEM_LIMIT_BYTES),
        name="moe_combine",
    )(yg, probs, h, mods, norm_w.reshape(1, -1))


def _moe_route(top_idx):
    t = top_idx.shape[0]
    a = t * TOP_K
    tm = MOE_TILE_M
    eid = top_idx.reshape(a)
    order = jnp.argsort(eid, stable=True).astype(jnp.int32)
    inv = jnp.argsort(order).astype(jnp.int32)
    counts = jnp.sum(jax.nn.one_hot(eid, N_EXPERTS, dtype=jnp.int32), axis=0)
    off = jnp.cumsum(counts) - counts
    pcounts = ((counts + tm - 1) // tm) * tm
    pend = jnp.cumsum(pcounts)
    poff = pend - pcounts
    n_tiles = a // tm + N_EXPERTS
    tile_start = jnp.arange(n_tiles, dtype=jnp.int32) * tm
    tile_e = jnp.minimum(jnp.searchsorted(pend, tile_start, side='right'), N_EXPERTS - 1).astype(jnp.int32)
    tile_ok = (tile_start < pend[-1]).astype(jnp.int32)
    ppos = jnp.arange(n_tiles * tm, dtype=jnp.int32)
    pe = jnp.repeat(tile_e, tm)
    r = ppos - poff[pe]
    src_rank = jnp.clip(off[pe] + jnp.minimum(r, counts[pe] - 1), 0, a - 1)
    src_tok = order[src_rank] // TOP_K
    pos = poff[eid] + (inv - off[eid])
    return src_tok, pos, tile_e, tile_ok


def _moe_layer(streams, wg, bg, wu, bu, wd, bd, norm_w):
    d = streams[0][0].shape[-1]
    t_all = jnp.concatenate([s[0].reshape(-1, d) for s in streams], axis=0)
    idx_all = jnp.concatenate([s[1].reshape(-1, 128)[:, :TOP_K] for s in streams], axis=0)
    src_tok, pos, tile_e, tile_ok = _moe_route(idx_all)
    xs = jnp.take(t_all, src_tok, axis=0)
    ys = _moe_grouped_ffn(xs, tile_e, tile_ok, wg, bg, wu, bu, wd, bd)
    outs, start = [], 0
    for t, _, probs, h, mods, tl in streams:
        b, l, _ = t.shape
        n = b * l * TOP_K
        yg = jnp.take(ys, pos[start:start + n], axis=0).reshape(b, l, TOP_K * d)
        outs.append(_moe_combine(yg, probs, h, mods, norm_w, tl))
        start += n
    return outs


def _k_rms(t, w):
    return t * lax.rsqrt(jnp.mean(t * t, axis=-1, keepdims=True) + RMS_EPS) * w


def _k_post_mix(y, h, gate, post_w, pre_w, shift, scale, rw, rb):
    h1 = h + gate * _k_rms(y, post_w)
    t = _k_rms(h1, pre_w) * (1.0 + scale) + shift
    logits = jnp.dot(t, rw, preferred_element_type=F32) + rb
    lane = lax.broadcasted_iota(jnp.int32, logits.shape, 1)
    idx_out = jnp.zeros(logits.shape, jnp.int32)
    val_out = jnp.zeros(logits.shape, F32)
    work = logits
    m0 = None
    for k in range(TOP_K):
        m = jnp.max(work, axis=-1, keepdims=True)
        sel = jnp.min(jnp.where(work == m, lane, 128), axis=-1, keepdims=True)
        if k == 0:
            m0 = m
        idx_out = jnp.where(lane == k, sel, idx_out)
        val_out = jnp.where(lane == k, jnp.exp(m - m0), val_out)
        work = jnp.where(lane == sel, -jnp.inf, work)
    probs = val_out / jnp.sum(val_out, axis=-1, keepdims=True)
    return h1, t, idx_out, probs


S5_BATCH_SUB = 8
S5_CH_BLOCK = 128
S5_ST_BLOCK = (S5_CH_BLOCK // S5_GROUP) * S5_STATE
S5_TL = 256


def _s5_inproj_body(h_ref, sh_ref, sc_ref, nw_ref, w_ref, u_ref):
    a = _k_rms(h_ref[0], nw_ref[...]) * (1.0 + sc_ref[0]) + sh_ref[0]
    u_ref[0] = jnp.dot(a.astype(BF16), w_ref[...], preferred_element_type=F32)


def _s5_inproj(h, mods, norm_w, w_bf16, tl):
    b, l, d = h.shape
    nb = mods.shape[0]
    bsel = (lambda bi: bi) if nb > 1 else (lambda bi: 0)
    return pl.pallas_call(
        _s5_inproj_body,
        grid=(b, l // tl),
        in_specs=[
            pl.BlockSpec((1, tl, d), lambda bi, ti: (bi, ti, 0)),
            pl.BlockSpec((1, 1, d), lambda bi, ti: (bsel(bi), 0, 0)),
            pl.BlockSpec((1, 1, d), lambda bi, ti: (bsel(bi), 0, 1)),
            pl.BlockSpec((1, d), lambda bi, ti: (0, 0)),
            pl.BlockSpec((d, d), lambda bi, ti: (0, 0)),
        ],
        out_specs=pl.BlockSpec((1, tl, d), lambda bi, ti: (bi // S5_BATCH_SUB, ti, bi % S5_BATCH_SUB)),
        out_shape=jax.ShapeDtypeStruct((b // S5_BATCH_SUB, l, S5_BATCH_SUB * d), F32),
        compiler_params=pltpu.CompilerParams(
            dimension_semantics=("arbitrary", "arbitrary"), vmem_limit_bytes=V7X_VMEM_LIMIT_BYTES),
        name="s5_inproj",
    )(h, mods, mods, norm_w.reshape(1, d), w_bf16)


def _s5_scan_body(uc_ref, ul_ref, b_ref, c_ref, lam_ref, y_ref, buf_ref, st_ref):
    d = pl.program_id(0)
    tc = pl.program_id(3)
    tl = ul_ref.shape[1]
    rows = tl * S5_BATCH_SUB
    ns = S5_ST_BLOCK

    @pl.when(tc == 0)
    def _():
        st_ref[...] = jnp.zeros(st_ref.shape, F32)
        u2 = uc_ref[0].reshape(rows, S5_CH_BLOCK).astype(BF16)
        buf_ref[...] = jnp.dot(u2, b_ref[0, 0], preferred_element_type=F32)

    @pl.when(tc > 0)
    def _():
        u2 = ul_ref[0].reshape(rows, S5_CH_BLOCK).astype(BF16)
        buf_ref[...] = jnp.dot(u2, b_ref[0, 0], preferred_element_type=F32)

    lam = lam_ref[0, 0]
    lr = jnp.broadcast_to(lam[0:1], (S5_BATCH_SUB, ns))
    li = jnp.broadcast_to(lam[1:2], (S5_BATCH_SUB, ns))

    def step(i, carry):
        xr, xi = carry
        t = jnp.where(d == 0, i, tl - 1 - i)
        r0 = pl.multiple_of(t * S5_BATCH_SUB, S5_BATCH_SUB)
        br = buf_ref[pl.ds(r0, S5_BATCH_SUB), 0:ns]
        bi = buf_ref[pl.ds(r0, S5_BATCH_SUB), ns:2 * ns]
        nr = lr * xr - li * xi + br
        ni = lr * xi + li * xr + bi
        buf_ref[pl.ds(r0, S5_BATCH_SUB), 0:ns] = nr
        buf_ref[pl.ds(r0, S5_BATCH_SUB), ns:2 * ns] = ni
        return nr, ni

    xr, xi = lax.fori_loop(0, tl, step, (st_ref[0], st_ref[1]), unroll=8)
    st_ref[0] = xr
    st_ref[1] = xi

    @pl.when(tc > 0)
    def _():
        y = jnp.dot(buf_ref[...].astype(BF16), c_ref[0, 0], preferred_element_type=F32)
        y_ref[0, 0] = y.reshape(tl, S5_BATCH_SUB, S5_CH_BLOCK)


def _s5_scan(u_ctx, u_lat, bblk, cblk, lam):
    nbg, l_lat, _, d = u_lat.shape
    assert u_ctx.shape[1] == S5_TL and l_lat % S5_TL == 0
    n_lc = l_lat // S5_TL
    n_sb = d // S5_CH_BLOCK

    def lat_chunk(di, tc):
        j = jnp.maximum(tc - 1, 0)
        return jnp.where(di == 0, j, n_lc - 1 - j)

    return pl.pallas_call(
        _s5_scan_body,
        grid=(2, nbg, n_sb, n_lc + 1),
        in_specs=[
            pl.BlockSpec((1, S5_TL, S5_BATCH_SUB, S5_CH_BLOCK), lambda di, bg, sb, tc: (bg, 0, 0, sb)),
            pl.BlockSpec((1, S5_TL, S5_BATCH_SUB, S5_CH_BLOCK), lambda di, bg, sb, tc: (bg, lat_chunk(di, tc), 0, sb)),
            pl.BlockSpec((1, 1, S5_CH_BLOCK, 2 * S5_ST_BLOCK), lambda di, bg, sb, tc: (di, sb, 0, 0)),
            pl.BlockSpec((1, 1, 2 * S5_ST_BLOCK, S5_CH_BLOCK), lambda di, bg, sb, tc: (di, sb, 0, 0)),
            pl.BlockSpec((1, 1, 2, S5_ST_BLOCK), lambda di, bg, sb, tc: (di, sb, 0, 0)),
        ],
        out_specs=pl.BlockSpec((1, 1, S5_TL, S5_BATCH_SUB, S5_CH_BLOCK),
                               lambda di, bg, sb, tc: (di, bg, lat_chunk(di, tc), 0, sb)),
        out_shape=jax.ShapeDtypeStruct((2, nbg, l_lat, S5_BATCH_SUB, d), F32),
        scratch_shapes=[
            pltpu.VMEM((S5_TL * S5_BATCH_SUB, 2 * S5_ST_BLOCK), F32),
            pltpu.VMEM((2, S5_BATCH_SUB, S5_ST_BLOCK), F32),
        ],
        compiler_params=pltpu.CompilerParams(
            dimension_semantics=("arbitrary", "arbitrary", "arbitrary", "arbitrary"),
            vmem_limit_bytes=V7X_VMEM_LIMIT_BYTES),
        name="s5_scan",
    )(u_ctx, u_lat, bblk, cblk, lam)


def _s5_glu_body(y0_ref, y1_ref, u_ref, h_ref, gate_ref, sh_ref, sc_ref, dsk_ref, wa_ref, wb_ref,
                 postw_ref, prew_ref, rw_ref, rb_ref, h1_ref, t_ref, idx_ref, p_ref):
    y = dsk_ref[...] * u_ref[0] + y0_ref[0, 0] + y1_ref[0, 0]
    o = (0.5 * y * (1.0 + jnp.tanh(math.sqrt(2.0 / math.pi) * (y + 0.044715 * (y * y * y))))).astype(BF16)
    ga = jnp.dot(o, wa_ref[...], preferred_element_type=F32)
    gb = jnp.dot(o, wb_ref[...], preferred_element_type=F32)
    ym = ga * jax.nn.sigmoid(gb)
    h1, t, idx, probs = _k_post_mix(ym, h_ref[0], gate_ref[0], postw_ref[...], prew_ref[...],
                                    sh_ref[0], sc_ref[0], rw_ref[...], rb_ref[...])
    h1_ref[0] = h1
    t_ref[0] = t.astype(BF16)
    idx_ref[0] = idx
    p_ref[0] = probs


def _s5_glu_postmix(y, u, h, mods, d_skip, wa, wb, post_w, pre_w, rw_pad, rb_pad, tl):
    b, l, d = h.shape
    nbg = b // S5_BATCH_SUB
    y4 = y.reshape(2, nbg, l, S5_BATCH_SUB * d)
    u3 = u.reshape(nbg, l, S5_BATCH_SUB * d)
    row = lambda v: v.reshape(1, -1)
    full = lambda shape: pl.BlockSpec(shape, lambda bi, ti: (0,) * len(shape))
    mod = lambda k: pl.BlockSpec((1, 1, d), lambda bi, ti: (bi, 0, k))
    tok = lambda w: pl.BlockSpec((1, tl, w), lambda bi, ti: (bi, ti, 0))
    return pl.pallas_call(
        _s5_glu_body,
        grid=(b, l // tl),
        in_specs=[
            pl.BlockSpec((1, 1, tl, d), lambda bi, ti: (0, bi // S5_BATCH_SUB, ti, bi % S5_BATCH_SUB)),
            pl.BlockSpec((1, 1, tl, d), lambda bi, ti: (1, bi // S5_BATCH_SUB, ti, bi % S5_BATCH_SUB)),
            pl.BlockSpec((1, tl, d), lambda bi, ti: (bi // S5_BATCH_SUB, ti, bi % S5_BATCH_SUB)),
            tok(d), mod(2), mod(3), mod(4),
            full((1, d)), full((d, d)), full((d, d)), full((1, d)), full((1, d)), full((d, 128)), full((1, 128)),
        ],
        out_specs=[tok(d), tok(d), tok(128), tok(128)],
        out_shape=[
            jax.ShapeDtypeStruct((b, l, d), F32),
            jax.ShapeDtypeStruct((b, l, d), BF16),
            jax.ShapeDtypeStruct((b, l, 128), jnp.int32),
            jax.ShapeDtypeStruct((b, l, 128), F32),
        ],
        compiler_params=pltpu.CompilerParams(
            dimension_semantics=("arbitrary", "arbitrary"), vmem_limit_bytes=V7X_VMEM_LIMIT_BYTES),
        name="s5_glu_postmix",
    )(y4, y4, u3, h, mods, mods, mods, row(d_skip), wa, wb, row(post_w), row(pre_w), rw_pad, rb_pad)


def _s5_block_params(a_re, a_im, log_step, b_re, b_im, c_re, c_im):
    gpb = S5_CH_BLOCK // S5_GROUP
    eye = jnp.eye(gpb, dtype=F32)
    bblks, cblks, lams = [], [], []
    for di in range(2):
        lam_re, lam_im, bb_re, bb_im = _s5_discretize(a_re[di], a_im[di], log_step[di], b_re[di], b_im[di])
        n_sb = lam_re.shape[0] // gpb

        def bdiag_in(bb):
            t = bb.reshape(n_sb, gpb, S5_STATE, S5_GROUP)
            return jnp.einsum('sgph,gk->sghkp', t, eye).reshape(n_sb, S5_CH_BLOCK, S5_ST_BLOCK)

        def bdiag_out(cc):
            t = cc.reshape(n_sb, gpb, S5_GROUP, S5_STATE)
            return jnp.einsum('sghp,gk->sgpkh', t, eye).reshape(n_sb, S5_ST_BLOCK, S5_CH_BLOCK)

        bblks.append(jnp.concatenate([bdiag_in(bb_re), bdiag_in(bb_im)], axis=-1))
        cblks.append(jnp.concatenate([bdiag_out(c_re[di]), -bdiag_out(c_im[di])], axis=-2))
        lams.append(jnp.stack([lam_re.reshape(n_sb, S5_ST_BLOCK), lam_im.reshape(n_sb, S5_ST_BLOCK)], axis=1))
    return jnp.stack(bblks).astype(BF16), jnp.stack(cblks).astype(BF16), jnp.stack(lams)


def _pad_router(router_w, router_b):
    e = router_w.shape[1]
    rw = jnp.pad(router_w, ((0, 0), (0, 128 - e)))
    rb = jnp.pad(router_b, (0, 128 - e), constant_values=-1e30).reshape(1, 128)
    return rw, rb


CONV_CH = GDN_CONV_CH + SSD_CONV_CH
Z_CH = GDN_V_WIDTH + SSD_INNER
IN0_TILE = 256
IN0_CONV_TILES = CONV_CH // IN0_TILE
IN0_Z_TILES = Z_CH // IN0_TILE
IN0_TILES = IN0_CONV_TILES + IN0_Z_TILES + 1
CONV_ROW_CHUNK = 256


def _conv_halo(cols):
    return ((cols + 1 + 7) // 8) * 8


def _in0_body(h_ref, sh_ref, sc_ref, nw_ref, w_ref, cw_ref, cb_ref, gp_ref,
              conv_ref, z_ref, g_ref, gt_ref, a_s, p0_s, pm_s, pp_s, *, n_rows, n_cols):
    n = pl.program_id(1)
    l = h_ref.shape[1]
    halo = _conv_halo(n_cols)
    rc = min(CONV_ROW_CHUNK, l)

    @pl.when(n == 0)
    def _():
        def norm_rows(i, carry):
            r0 = pl.multiple_of(i * rc, rc)
            a = _k_rms(h_ref[0, pl.ds(r0, rc), :], nw_ref[...]) * (1.0 + sc_ref[0]) + sh_ref[0]
            a_s[pl.ds(r0, rc), :] = a.astype(BF16)
            return carry
        lax.fori_loop(0, l // rc, norm_rows, 0)

    p = jnp.dot(a_s[...], w_ref[...], preferred_element_type=F32)

    @pl.when(n < IN0_CONV_TILES)
    def _():
        zero_halo = jnp.zeros((halo, IN0_TILE), F32)
        for s in (p0_s, pm_s, pp_s):
            s[pl.ds(0, halo), :] = zero_halo
            s[pl.ds(halo + l, halo), :] = zero_halo
        p0_s[pl.ds(halo, l), :] = p
        tcol = lax.broadcasted_iota(jnp.int32, (l, IN0_TILE), 0) % n_cols
        pm_s[pl.ds(halo, l), :] = jnp.where(tcol != 0, p0_s[pl.ds(halo - 1, l), :], 0.0)
        pp_s[pl.ds(halo, l), :] = jnp.where(tcol != n_cols - 1, p0_s[pl.ds(halo + 1, l), :], 0.0)
        dys = (0,) if n_rows == 1 else (-1, 0, 1)

        def conv_rows(i, carry):
            r0 = pl.multiple_of(i * rc, rc)
            acc = jnp.zeros((rc, IN0_TILE), F32) + cb_ref[...]
            for dy in dys:
                base = halo + dy * n_cols
                for dx, src in ((0, pm_s), (1, p0_s), (2, pp_s)):
                    tap = (dy + 1) * 3 + dx
                    acc = acc + cw_ref[tap:tap + 1, :] * src[pl.ds(r0 + base, rc), :]
            conv_ref[0, pl.ds(r0, rc), :] = (acc * jax.nn.sigmoid(acc)).astype(conv_ref.dtype)
            return carry
        lax.fori_loop(0, l // rc, conv_rows, 0)

    @pl.when((n >= IN0_CONV_TILES) & (n < IN0_CONV_TILES + IN0_Z_TILES))
    def _():
        z_ref[0] = (p * jax.nn.sigmoid(p)).astype(z_ref.dtype)

    @pl.when(n == IN0_TILES - 1)
    def _():
        pg = p[:, :128]
        lane = lax.broadcasted_iota(jnp.int32, pg.shape, 1)
        xb = pg + gp_ref[0:1, :]
        sp = jnp.maximum(xb, 0.0) + jnp.log(1.0 + jnp.exp(-jnp.abs(xb)))
        neg_a_sp = -jnp.exp(gp_ref[1:2, :]) * sp
        gates = jnp.where(lane < 2 * GDN_HEADS, neg_a_sp,
                          jnp.where(lane < GATE_COL_DT, jax.nn.sigmoid(pg),
                                    jnp.where(lane < GATE_COL_DA, sp,
                                              jnp.where(lane < GATE_COL_DA + 2 * SSD_HEADS, neg_a_sp, 0.0))))
        g_ref[0] = gates
        for ci in range(l // SCAN_C):
            gt_ref[0, ci] = gates[ci * SCAN_C:(ci + 1) * SCAN_C, :].T


def _in0_features(h, mods, norm_w, w_all, conv_w, conv_b, gate_params, n_rows, n_cols):
    b, l, d = h.shape
    nb = mods.shape[0]
    bsel = (lambda bi: bi) if nb > 1 else (lambda bi: 0)
    halo = _conv_halo(n_cols)
    nct, nzt = IN0_CONV_TILES, IN0_Z_TILES
    pad_rows = l + 2 * halo
    return pl.pallas_call(
        functools.partial(_in0_body, n_rows=n_rows, n_cols=n_cols),
        grid=(b, IN0_TILES),
        in_specs=[
            pl.BlockSpec((1, l, d), lambda bi, ni: (bi, 0, 0)),
            pl.BlockSpec((1, 1, d), lambda bi, ni: (bsel(bi), 0, 0)),
            pl.BlockSpec((1, 1, d), lambda bi, ni: (bsel(bi), 0, 1)),
            pl.BlockSpec((1, d), lambda bi, ni: (0, 0)),
            pl.BlockSpec((d, IN0_TILE), lambda bi, ni: (0, ni)),
            pl.BlockSpec((9, IN0_TILE), lambda bi, ni: (0, jnp.minimum(ni, nct - 1))),
            pl.BlockSpec((1, IN0_TILE), lambda bi, ni: (0, jnp.minimum(ni, nct - 1))),
            pl.BlockSpec((2, 128), lambda bi, ni: (0, 0)),
        ],
        out_specs=[
            pl.BlockSpec((1, l, IN0_TILE), lambda bi, ni: (bi, 0, jnp.minimum(ni, nct - 1))),
            pl.BlockSpec((1, l, IN0_TILE), lambda bi, ni: (bi, 0, jnp.clip(ni - nct, 0, nzt - 1))),
            pl.BlockSpec((1, l, 128), lambda bi, ni: (bi, 0, 0)),
            pl.BlockSpec((1, l // SCAN_C, 128, SCAN_C), lambda bi, ni: (bi, 0, 0, 0)),
        ],
        out_shape=[
            jax.ShapeDtypeStruct((b, l, CONV_CH), BF16),
            jax.ShapeDtypeStruct((b, l, Z_CH), BF16),
            jax.ShapeDtypeStruct((b, l, 128), F32),
            jax.ShapeDtypeStruct((b, l // SCAN_C, 128, SCAN_C), F32),
        ],
        scratch_shapes=[
            pltpu.VMEM((l, d), BF16),
            pltpu.VMEM((pad_rows, IN0_TILE), F32),
            pltpu.VMEM((pad_rows, IN0_TILE), F32),
            pltpu.VMEM((pad_rows, IN0_TILE), F32),
        ],
        compiler_params=pltpu.CompilerParams(
            dimension_semantics=("arbitrary", "arbitrary"), vmem_limit_bytes=V7X_VMEM_LIMIT_BYTES),
        name="in0_features",
    )(h, mods, mods, norm_w.reshape(1, d), w_all, conv_w, conv_b, gate_params)


def _in0_params(w_in, gdn_conv_w, ssd_conv_w, ssd_conv_b, gdn_a_log, gdn_dt_bias, ssd_a_log, ssd_dt_bias):
    splits = np.cumsum(IN0_SIZES)[:-1].tolist()
    w_ca, w_cb, w_za, w_zb, w_a, w_b, w_dt = jnp.split(w_in, splits, axis=1)
    d = w_in.shape[0]
    w_gate = jnp.concatenate([w_a, w_b, w_dt, w_dt], axis=1)
    w_gate = jnp.pad(w_gate, ((0, 0), (0, IN0_TILE - w_gate.shape[1])))
    w_all = jnp.concatenate([w_ca, w_cb, w_za, w_zb, w_gate], axis=1).astype(BF16)
    conv_w = jnp.concatenate([gdn_conv_w, ssd_conv_w], axis=-1).reshape(9, CONV_CH)
    conv_b = jnp.concatenate([jnp.zeros((GDN_CONV_CH,), F32), ssd_conv_b]).reshape(1, CONV_CH)
    z8 = jnp.zeros((2 * GDN_HEADS,), F32)
    z16 = jnp.zeros((2 * SSD_HEADS,), F32)
    tail = jnp.zeros((128 - GATE_COL_DA - 2 * SSD_HEADS,), F32)
    bias = jnp.concatenate([gdn_dt_bias.reshape(-1), z8, ssd_dt_bias.reshape(-1), ssd_dt_bias.reshape(-1), tail])
    alog = jnp.concatenate([gdn_a_log.reshape(-1), z8, z16, ssd_a_log.reshape(-1), tail])
    return w_all, conv_w, conv_b, jnp.stack([bias, alog])


def _merge0_body(o_ref, y_ref, xs_ref, z_ref, h_ref, gate_ref, sh_ref, sc_ref, gnw_ref, dsk_ref, snw_ref, wo_ref,
                 postw_ref, prew_ref, rw_ref, rb_ref, h1_ref, t_ref, idx_ref, p_ref):
    z = z_ref[0].astype(F32)
    parts = []
    for hd in range(GDN_HEADS):
        cs = slice(hd * GDN_DV, (hd + 1) * GDN_DV)
        parts.append(_k_rms(o_ref[0, :, cs], gnw_ref[...]) * z[:, cs])
    gw = SSD_INNER // SSD_GROUPS
    for g in range(SSD_GROUPS):
        cs = slice(g * gw, (g + 1) * gw)
        y2 = (y_ref[0, :, cs] + dsk_ref[:, cs] * xs_ref[0, :, cs].astype(F32)) * z[:, GDN_V_WIDTH + g * gw:GDN_V_WIDTH + (g + 1) * gw]
        parts.append(_k_rms(y2, snw_ref[:, cs]))
    mixed = jnp.concatenate(parts, axis=-1).astype(BF16)
    ym = jnp.dot(mixed, wo_ref[...], preferred_element_type=F32)
    h1, t, idx, probs = _k_post_mix(ym, h_ref[0], gate_ref[0], postw_ref[...], prew_ref[...],
                                    sh_ref[0], sc_ref[0], rw_ref[...], rb_ref[...])
    h1_ref[0] = h1
    t_ref[0] = t.astype(BF16)
    idx_ref[0] = idx
    p_ref[0] = probs


def _merge0_postmix(o, y, conv, z, h, mods, gdn_norm_w, ssd_d, ssd_norm_w, w_out, post_w, pre_w, rw_pad, rb_pad, tl):
    b, l, d = h.shape
    nb = mods.shape[0]
    bsel = (lambda bi: bi) if nb > 1 else (lambda bi: 0)
    row = lambda v: v.reshape(1, -1)
    full = lambda shape: pl.BlockSpec(shape, lambda bi, ti: (0,) * len(shape))
    mod = lambda k: pl.BlockSpec((1, 1, d), lambda bi, ti: (bsel(bi), 0, k))
    tok = lambda w: pl.BlockSpec((1, tl, w), lambda bi, ti: (bi, ti, 0))
    xs_block = GDN_CONV_CH // SSD_INNER
    dsk = jnp.repeat(ssd_d, SSD_HEADDIM)
    return pl.pallas_call(
        _merge0_body,
        grid=(b, l // tl),
        in_specs=[
            tok(GDN_V_WIDTH), tok(SSD_INNER),
            pl.BlockSpec((1, tl, SSD_INNER), lambda bi, ti: (bi, ti, xs_block)),
            tok(Z_CH), tok(d), mod(2), mod(3), mod(4),
            full((1, GDN_DV)), full((1, SSD_INNER)), full((1, SSD_INNER)), full((Z_CH, d)),
            full((1, d)), full((1, d)), full((d, 128)), full((1, 128)),
        ],
        out_specs=[tok(d), tok(d), tok(128), tok(128)],
        out_shape=[
            jax.ShapeDtypeStruct((b, l, d), F32),
            jax.ShapeDtypeStruct((b, l, d), BF16),
            jax.ShapeDtypeStruct((b, l, 128), jnp.int32),
            jax.ShapeDtypeStruct((b, l, 128), F32),
        ],
        compiler_params=pltpu.CompilerParams(
            dimension_semantics=("arbitrary", "arbitrary"), vmem_limit_bytes=V7X_VMEM_LIMIT_BYTES),
        name="merge0_postmix",
    )(o, y, conv, z, h, mods, mods, mods, row(gdn_norm_w), row(dsk), row(ssd_norm_w), w_out,
      row(post_w), row(pre_w), rw_pad, rb_pad)


SCAN_C = 128
GDN_INV_BLOCK = 16


def _dot(a, b):
    return jnp.dot(a, b, preferred_element_type=F32)


def _dot_nt(a, b):
    return lax.dot_general(a, b, (((1,), (1,)), ((), ())), preferred_element_type=F32)


def _dot_tn(a, b):
    return lax.dot_general(a, b, (((0,), (0,)), ((), ())), preferred_element_type=F32)


def _bdot(a, b):
    return _dot(a.astype(BF16), b.astype(BF16))


def _bdot_nt(a, b):
    return _dot_nt(a.astype(BF16), b.astype(BF16))


def _bdot_tn(a, b):
    return _dot_tn(a.astype(BF16), b.astype(BF16))


def _scan_masks(fwd):
    row = lax.broadcasted_iota(jnp.int32, (SCAN_C, SCAN_C), 0)
    col = lax.broadcasted_iota(jnp.int32, (SCAN_C, SCAN_C), 1)
    lead = (row - col) * jnp.where(fwd, 1, -1)
    return row, col, lead >= 0, lead <= 0, lead > 0


def _cumsum_col_row(g_col, g_row, incl, incl_t):
    gc_col = jnp.sum(jnp.where(incl, g_row, 0.0), axis=1, keepdims=True)
    gc_row = jnp.sum(jnp.where(incl_t, g_col, 0.0), axis=0, keepdims=True)
    return gc_col, gc_row


def _unit_tri_inverse(m, row, col):
    eye = (row == col).astype(F32)
    same = (row // GDN_INV_BLOCK) == (col // GDN_INV_BLOCK)
    md = jnp.where(same, m, 0.0)
    mo = m - md
    p = md
    td = eye - md
    for _ in range(int(math.log2(GDN_INV_BLOCK)) - 1):
        p = _dot(p, p)
        td = td + _dot(td, p)
    n = _dot(td, mo)
    q = eye - n
    p = n
    for _ in range(int(math.log2(SCAN_C // GDN_INV_BLOCK)) - 1):
        p = _dot(p, p)
        q = q + _dot(q, p)
    return _dot(q, td)


def _gdn_body(ql_ref, kl_ref, vl_ref, gl_ref, gtl_ref, qc_ref, kc_ref, vc_ref, gc_ref, gtc_ref,
              ol_ref, oc_ref, u_s, w_s, qk_s, qd_s, kd_s, dec_s):
    h = pl.program_id(1)
    d = pl.program_id(2)
    fwd = d == 0
    col_g = d * GDN_HEADS + h
    col_b = 2 * GDN_HEADS + col_g
    c = SCAN_C
    ncc = qc_ref.shape[1] // c
    ncl = ql_ref.shape[1] // c
    row, col, incl, incl_t, strict = _scan_masks(fwd)
    lane = lax.broadcasted_iota(jnp.int32, (c, 128), 1)

    @pl.when(fwd)
    def _():
        ol_ref[...] = jnp.zeros(ol_ref.shape, F32)
        oc_ref[...] = jnp.zeros(oc_ref.shape, F32)

    def prep(q_ref, k_ref, v_ref, g_ref, gt_ref, ci, p):
        t0 = pl.multiple_of(ci * c, c)
        q = q_ref[0, pl.ds(t0, c), :].astype(F32)
        k = k_ref[0, pl.ds(t0, c), :].astype(F32)
        v = v_ref[0, pl.ds(t0, c), :].astype(F32)
        q = q * lax.rsqrt(jnp.sum(q * q, axis=-1, keepdims=True) + RMS_EPS) * (GDN_DK ** -0.5)
        k = k * lax.rsqrt(jnp.sum(k * k, axis=-1, keepdims=True) + RMS_EPS)
        gch = g_ref[0, pl.ds(t0, c), :]
        g_col = jnp.sum(jnp.where(lane == col_g, gch, 0.0), axis=1, keepdims=True)
        b_col = jnp.sum(jnp.where(lane == col_b, gch, 0.0), axis=1, keepdims=True)
        g_row = gt_ref[0, ci, pl.ds(col_g, 1), :]
        gc_col, gc_row = _cumsum_col_row(g_col, g_row, incl, incl_t)
        g_tot = jnp.sum(g_col, axis=0, keepdims=True)
        decay = jnp.where(incl, jnp.exp(jnp.where(incl, gc_col - gc_row, 0.0)), 0.0)
        kb = k * b_col
        m = jnp.where(strict, _bdot_nt(kb, k) * decay, 0.0)
        t_inv = _unit_tri_inverse(m, row, col)
        egc = jnp.exp(gc_col)
        p0 = pl.multiple_of(p * c, c)
        u_s[pl.ds(p0, c), :] = _bdot(t_inv, v * b_col)
        w_s[pl.ds(p0, c), :] = _bdot(t_inv, kb * egc)
        qk_s[pl.ds(p0, c), :] = _bdot_nt(q, k) * decay
        qd_s[pl.ds(p0, c), :] = q * egc
        kd_s[pl.ds(p0, c), :] = k * jnp.exp(g_tot - gc_col)
        dec_s[pl.ds(p, 1), :] = jnp.broadcast_to(jnp.exp(g_tot), (1, 128))

    def advance(o_ref, ci, p, s):
        p0 = pl.multiple_of(p * c, c)
        t0 = pl.multiple_of(ci * c, c)
        v_new = u_s[pl.ds(p0, c), :] - _bdot(w_s[pl.ds(p0, c), :], s)
        o = _bdot(qd_s[pl.ds(p0, c), :], s) + _bdot(qk_s[pl.ds(p0, c), :], v_new)
        o_ref[0, pl.ds(t0, c), :] += o
        return s * dec_s[pl.ds(p, 1), :] + _bdot_tn(kd_s[pl.ds(p0, c), :], v_new)

    def order(i, n):
        return jnp.where(fwd, i, n - 1 - i)

    for i in range(ncc):
        prep(qc_ref, kc_ref, vc_ref, gc_ref, gtc_ref, order(i, ncc), i)

    def prep_lat(i, carry):
        prep(ql_ref, kl_ref, vl_ref, gl_ref, gtl_ref, order(i, ncl), ncc + i)
        return carry

    lax.fori_loop(0, ncl, prep_lat, 0, unroll=2)

    s = jnp.zeros((GDN_DK, GDN_DV), F32)
    for i in range(ncc):
        s = advance(oc_ref, order(i, ncc), i, s)
    lax.fori_loop(0, ncl, lambda i, st: advance(ol_ref, order(i, ncl), ncc + i, st), s)


SSD_R = SSD_HEADS // SSD_GROUPS
SSD_PAIRS = SSD_R * SSD_HEADDIM // 128
GATE_COL_DT = 4 * GDN_HEADS
GATE_COL_DA = GATE_COL_DT + 2 * SSD_HEADS


def _ssd_body(xl_ref, bl_ref, cl_ref, gl_ref, gtl_ref, xc_ref, bc_ref, cc_ref, gc_ref, gtc_ref,
              yl_ref, yc_ref, st_ref):
    g = pl.program_id(1)
    d = pl.program_id(2)
    fwd = d == 0
    c = SCAN_C
    ncc = xc_ref.shape[1] // c
    ncl = xl_ref.shape[1] // c
    _, _, incl, incl_t, _ = _scan_masks(fwd)
    lane = lax.broadcasted_iota(jnp.int32, (c, 128), 1)
    low = lane < SSD_HEADDIM

    @pl.when(fwd)
    def _():
        yl_ref[...] = jnp.zeros(yl_ref.shape, F32)
        yc_ref[...] = jnp.zeros(yc_ref.shape, F32)

    st_ref[...] = jnp.zeros(st_ref.shape, F32)

    def chunk(x_ref, b_ref, c_ref, g_ref, gt_ref, y_ref, ci):
        t0 = pl.multiple_of(ci * c, c)
        bm = b_ref[0, pl.ds(t0, c), :].astype(BF16)
        cm = c_ref[0, pl.ds(t0, c), :].astype(BF16)
        gch = g_ref[0, pl.ds(t0, c), :]
        cb = _dot_nt(cm, bm)
        for pr in range(SSD_PAIRS):
            per_head = []
            for s in range(2):
                hh = g * SSD_R + 2 * pr + s
                col_dt = GATE_COL_DT + d * SSD_HEADS + hh
                col_da = GATE_COL_DA + d * SSD_HEADS + hh
                dt_col = jnp.sum(jnp.where(lane == col_dt, gch, 0.0), axis=1, keepdims=True)
                da_col = jnp.sum(jnp.where(lane == col_da, gch, 0.0), axis=1, keepdims=True)
                da_row = gt_ref[0, ci, pl.ds(col_da, 1), :]
                acs_col, acs_row = _cumsum_col_row(da_col, da_row, incl, incl_t)
                a_tot = jnp.sum(da_col, axis=0, keepdims=True)
                lmat = jnp.where(incl, jnp.exp(jnp.where(incl, acs_col - acs_row, 0.0)), 0.0)
                per_head.append((dt_col, acs_col, a_tot, (cb * lmat).astype(BF16)))
            pick = lambda k: jnp.where(low, per_head[0][k], per_head[1][k])
            x = x_ref[0, pl.ds(t0, c), pr * 128:(pr + 1) * 128].astype(F32)
            xdt = x * pick(0)
            acs = pick(1)
            a_tot = pick(2)
            xdt_b = xdt.astype(BF16)
            y_diag = jnp.where(low, _dot(per_head[0][3], xdt_b), _dot(per_head[1][3], xdt_b))
            st = st_ref[pr]
            y_off = _dot(cm, st.astype(BF16)) * jnp.exp(acs)
            y_ref[0, pl.ds(t0, c), pr * 128:(pr + 1) * 128] += y_diag + y_off
            st_ref[pr] = st * jnp.exp(a_tot) + _dot_tn(bm, (xdt * jnp.exp(a_tot - acs)).astype(BF16))

    def order(i, n):
        return jnp.where(fwd, i, n - 1 - i)

    for i in range(ncc):
        chunk(xc_ref, bc_ref, cc_ref, gc_ref, gtc_ref, yc_ref, order(i, ncc))

    def lat(i, carry):
        chunk(xl_ref, bl_ref, cl_ref, gl_ref, gtl_ref, yl_ref, order(i, ncl))
        return carry

    lax.fori_loop(0, ncl, lat, 0)


def _ssd_scan(xbc_l, g_l, gt_l, xbc_c, g_c, gt_c, col0=0):
    b, l, _ = xbc_l.shape
    lc = xbc_c.shape[1]
    gw = SSD_R * SSD_HEADDIM
    x0 = col0 // gw
    nxb = (col0 + SSD_INNER) // 128

    def stream(n):
        return [pl.BlockSpec((1, n, gw), lambda bi, gi, di: (bi, 0, x0 + gi)),
                pl.BlockSpec((1, n, 128), lambda bi, gi, di: (bi, 0, nxb + gi)),
                pl.BlockSpec((1, n, 128), lambda bi, gi, di: (bi, 0, nxb + SSD_GROUPS + gi)),
                pl.BlockSpec((1, n, 128), lambda bi, gi, di: (bi, 0, 0)),
                pl.BlockSpec((1, n // SCAN_C, 128, SCAN_C), lambda bi, gi, di: (bi, 0, 0, 0))]

    return pl.pallas_call(
        _ssd_body,
        grid=(b, SSD_GROUPS, 2),
        in_specs=stream(l) + stream(lc),
        out_specs=[pl.BlockSpec((1, l, gw), lambda bi, gi, di: (bi, 0, gi)),
                   pl.BlockSpec((1, lc, gw), lambda bi, gi, di: (bi, 0, gi))],
        out_shape=[jax.ShapeDtypeStruct((b, l, SSD_INNER), F32),
                   jax.ShapeDtypeStruct((b, lc, SSD_INNER), F32)],
        scratch_shapes=[pltpu.VMEM((SSD_PAIRS, SSD_STATE, 128), F32)],
        compiler_params=pltpu.CompilerParams(
            dimension_semantics=("arbitrary", "arbitrary", "arbitrary"), vmem_limit_bytes=V7X_VMEM_LIMIT_BYTES),
        name="ssd_scan",
    )(xbc_l, xbc_l, xbc_l, g_l, gt_l, xbc_c, xbc_c, xbc_c, g_c, gt_c)


def _chunk_rows(gt):
    b, w, l = gt.shape
    return gt.reshape(b, w, l // SCAN_C, SCAN_C).transpose(0, 2, 1, 3)


def _gdn_scan(qkv_l, g_l, gt_l, qkv_c, g_c, gt_c):
    b, l, _ = qkv_l.shape
    lc = qkv_c.shape[1]
    hd = GDN_HEADS

    def stream(n):
        tok = lambda off: pl.BlockSpec((1, n, 128), lambda bi, hi, di: (bi, 0, off + hi))
        return [tok(0), tok(hd), tok(2 * hd),
                pl.BlockSpec((1, n, 128), lambda bi, hi, di: (bi, 0, 0)),
                pl.BlockSpec((1, n // SCAN_C, 128, SCAN_C), lambda bi, hi, di: (bi, 0, 0, 0))]

    nt = l + lc
    return pl.pallas_call(
        _gdn_body,
        grid=(b, hd, 2),
        in_specs=stream(l) + stream(lc),
        out_specs=[pl.BlockSpec((1, l, 128), lambda bi, hi, di: (bi, 0, hi)),
                   pl.BlockSpec((1, lc, 128), lambda bi, hi, di: (bi, 0, hi))],
        out_shape=[jax.ShapeDtypeStruct((b, l, hd * GDN_DV), F32),
                   jax.ShapeDtypeStruct((b, lc, hd * GDN_DV), F32)],
        scratch_shapes=[pltpu.VMEM((nt, 128), F32) for _ in range(5)]
        + [pltpu.VMEM((nt // SCAN_C, 128), F32)],
        compiler_params=pltpu.CompilerParams(
            dimension_semantics=("arbitrary", "arbitrary", "arbitrary"), vmem_limit_bytes=V7X_VMEM_LIMIT_BYTES),
        name="gdn_scan",
    )(qkv_l, qkv_l, qkv_l, g_l, gt_l, qkv_c, qkv_c, qkv_c, g_c, gt_c)


def _rmsnorm(t, w):
    tf = t.astype(F32)
    tf = tf * lax.rsqrt(jnp.mean(tf * tf, axis=-1, keepdims=True) + RMS_EPS)
    return tf * w.astype(F32)


def _l2norm(t):
    tf = t.astype(F32)
    return tf * lax.rsqrt(jnp.sum(tf * tf, axis=-1, keepdims=True) + RMS_EPS)


def _modulate(t, shift, scale):
    return t * (1.0 + scale) + shift


def _grid_dwconv(t, w, n_rows, n_cols):
    b, l, ch = t.shape
    img = t.reshape(b, n_rows, n_cols, ch)
    out = lax.conv_general_dilated(img, w[:, :, None, :], window_strides=(1, 1), padding='SAME',
                                   dimension_numbers=('NHWC', 'HWIO', 'NHWC'), feature_group_count=ch)
    return out.reshape(b, l, ch)


def _gated_delta_chunked(q, k, v, g, beta, s0):
    b, l, h, dk = q.shape
    c = GDN_CHUNK
    n = l // c

    def chunks(t):
        return t.astype(F32).reshape(b, n, c, h, -1).transpose(1, 0, 3, 2, 4)

    qc = chunks(q) * (dk ** -0.5)
    kc, vc = chunks(k), chunks(v)
    gc = jnp.cumsum(g.astype(F32).reshape(b, n, c, h).transpose(1, 0, 3, 2), axis=-1)
    bc = beta.astype(F32).reshape(b, n, c, h).transpose(1, 0, 3, 2)
    causal = jnp.tril(jnp.ones((c, c), bool))
    strict = jnp.tril(jnp.ones((c, c), bool), -1)
    seg = gc[..., :, None] - gc[..., None, :]
    decay = jnp.where(causal, jnp.exp(jnp.where(causal, seg, 0.0)), 0.0)
    kb = kc * bc[..., None]
    m = jnp.where(strict, jnp.einsum('nbhid,nbhjd->nbhij', kb, kc) * decay, 0.0)
    tri = jnp.eye(c, dtype=F32) + m
    u = lax.linalg.triangular_solve(tri, vc * bc[..., None], left_side=True, lower=True, unit_diagonal=True)
    w = lax.linalg.triangular_solve(tri, kb * jnp.exp(gc)[..., None], left_side=True, lower=True, unit_diagonal=True)
    qk = jnp.einsum('nbhid,nbhjd->nbhij', qc, kc) * decay
    q_dec = qc * jnp.exp(gc)[..., None]
    k_dec = kc * jnp.exp(gc[..., -1:] - gc)[..., None]
    g_last = jnp.exp(gc[..., -1])

    def step(s, inp):
        u_i, w_i, qk_i, qd_i, kd_i, gl_i = inp
        v_new = u_i - w_i @ s
        o_i = qd_i @ s + qk_i @ v_new
        s = s * gl_i[..., None, None] + jnp.swapaxes(kd_i, -1, -2) @ v_new
        return s, o_i

    s_fin, o = lax.scan(step, s0.astype(F32), (u, w, qk, q_dec, k_dec, g_last))
    o = o.transpose(1, 0, 3, 2, 4).reshape(b, l, h, -1)
    return o, s_fin


def _ssd_chunked(x, dt, da, bm, cm, s0):
    b, l, h, p = x.shape
    g, nst = bm.shape[2], bm.shape[3]
    r = h // g
    c = SSD_CHUNK
    n = l // c
    xc = (x.astype(F32) * dt.astype(F32)[..., None]).reshape(b, n, c, g, r, p)
    acs = jnp.cumsum(da.astype(F32).reshape(b, n, c, g, r), axis=2)
    bc = bm.astype(F32).reshape(b, n, c, g, nst)
    cc = cm.astype(F32).reshape(b, n, c, g, nst)
    causal = jnp.tril(jnp.ones((c, c), bool))[:, :, None, None]
    seg = acs[:, :, :, None] - acs[:, :, None, :]
    lmat = jnp.where(causal, jnp.exp(jnp.where(causal, seg, 0.0)), 0.0)
    scores = jnp.einsum('bnlgd,bnsgd->bnlsg', cc, bc)[..., None] * lmat
    y_diag = jnp.einsum('bnlsgr,bnsgrp->bnlgrp', scores, xc)
    decay_out = jnp.exp(acs[:, :, -1:] - acs)
    chunk_states = jnp.einsum('bnsgd,bnsgrp->bngrpd', bc, xc * decay_out[..., None])
    states = jnp.concatenate([s0.astype(F32).reshape(b, 1, g, r, p, nst), chunk_states], axis=1)
    cum = jnp.cumsum(jnp.pad(acs[:, :, -1], ((0, 0), (1, 0), (0, 0), (0, 0))), axis=1)
    tri = jnp.tril(jnp.ones((n + 1, n + 1), bool))[:, :, None, None]
    segc = cum[:, :, None] - cum[:, None, :]
    dec = jnp.where(tri, jnp.exp(jnp.where(tri, segc, 0.0)), 0.0)
    new = jnp.einsum('bzcgr,bcgrpd->bzgrpd', dec, states)
    y_off = jnp.einsum('bnlgd,bngrpd->bnlgrp', cc, new[:, :-1]) * jnp.exp(acs)[..., None]
    y = (y_diag + y_off).reshape(b, l, h, p)
    return y, new[:, -1].reshape(b, h, p, nst)


def _bidirectional(scan_fn, ctx_dirs, lat_dirs, s0):
    flip = lambda t: jnp.flip(t, axis=1)
    y_ctx, y_lat = 0.0, 0.0
    for d in range(2):
        rev = d == 1
        cargs = tuple(flip(t) for t in ctx_dirs[d]) if rev else ctx_dirs[d]
        largs = tuple(flip(t) for t in lat_dirs[d]) if rev else lat_dirs[d]
        o_c, s_c = scan_fn(*cargs, s0)
        o_l, _ = scan_fn(*largs, s_c)
        y_ctx = y_ctx + (flip(o_c) if rev else o_c)
        y_lat = y_lat + (flip(o_l) if rev else o_l)
    return y_ctx, y_lat


def _hybrid_mixer(h_ctx, h_lat, rows, w_in, gdn_conv_w, gdn_a_log, gdn_dt_bias, gdn_norm_w,
                  ssd_conv_w, ssd_conv_b, ssd_a_log, ssd_dt_bias, ssd_d, ssd_norm_w, w_out):
    splits = np.cumsum(IN0_SIZES)[:-1].tolist()

    def features(h, n_rows, n_cols):
        b, l, _ = h.shape
        conv_a, conv_b, z_a, z_b, a_raw, b_raw, dt_raw = jnp.split(h @ w_in, splits, axis=-1)
        conv_a = jax.nn.silu(_grid_dwconv(conv_a, gdn_conv_w, n_rows, n_cols))
        conv_b = jax.nn.silu(_grid_dwconv(conv_b, ssd_conv_w, n_rows, n_cols) + ssd_conv_b)
        q, k, v = jnp.split(conv_a, [GDN_HEADS * GDN_DK, 2 * GDN_HEADS * GDN_DK], axis=-1)
        q = _l2norm(q.reshape(b, l, GDN_HEADS, GDN_DK))
        k = _l2norm(k.reshape(b, l, GDN_HEADS, GDN_DK))
        v = v.reshape(b, l, GDN_HEADS, GDN_DV)
        g = -jnp.exp(gdn_a_log) * jax.nn.softplus(a_raw.reshape(b, l, 2, GDN_HEADS) + gdn_dt_bias)
        beta = jax.nn.sigmoid(b_raw.reshape(b, l, 2, GDN_HEADS))
        xs, bm, cm = jnp.split(conv_b, [SSD_INNER, SSD_INNER + SSD_GROUPS * SSD_STATE], axis=-1)
        xs = xs.reshape(b, l, SSD_HEADS, SSD_HEADDIM)
        bm = bm.reshape(b, l, SSD_GROUPS, SSD_STATE)
        cm = cm.reshape(b, l, SSD_GROUPS, SSD_STATE)
        dt = jax.nn.softplus(dt_raw.reshape(b, l, 2, SSD_HEADS) + ssd_dt_bias)
        da = -jnp.exp(ssd_a_log) * dt
        gdn_dirs = tuple((q, k, v, g[:, :, d], beta[:, :, d]) for d in range(2))
        ssd_dirs = tuple((xs, dt[:, :, d], da[:, :, d], bm, cm) for d in range(2))
        return gdn_dirs, ssd_dirs, xs, z_a, z_b

    b = h_lat.shape[0]
    gdn_c, ssd_c, xs_c, za_c, zb_c = features(h_ctx, 1, h_ctx.shape[1])
    gdn_l, ssd_l, xs_l, za_l, zb_l = features(h_lat, rows, GRID_W)
    o_c, o_l = _bidirectional(_gated_delta_chunked, gdn_c, gdn_l, jnp.zeros((b, GDN_HEADS, GDN_DK, GDN_DV), F32))
    y_c, y_l = _bidirectional(_ssd_chunked, ssd_c, ssd_l, jnp.zeros((b, SSD_HEADS, SSD_HEADDIM, SSD_STATE), F32))

    def merge(o, y, xs, z_a, z_b):
        bb, l = o.shape[0], o.shape[1]
        o = _rmsnorm(o, gdn_norm_w) * jax.nn.silu(z_a).reshape(bb, l, GDN_HEADS, GDN_DV)
        y = (y + ssd_d[:, None] * xs).reshape(bb, l, SSD_INNER) * jax.nn.silu(z_b)
        y = _rmsnorm(y.reshape(bb, l, SSD_GROUPS, -1), ssd_norm_w.reshape(SSD_GROUPS, -1))
        mixed = jnp.concatenate([o.reshape(bb, l, -1), y.reshape(bb, l, -1)], axis=-1)
        return mixed @ w_out

    return merge(o_c, y_c, xs_c, za_c, zb_c), merge(o_l, y_l, xs_l, za_l, zb_l)


def _s5_discretize(a_re, a_im, log_step, b_re, b_im):
    step = jnp.exp(log_step)[:, None]
    mag = jnp.exp(a_re * step)
    lam_re, lam_im = mag * jnp.cos(a_im * step), mag * jnp.sin(a_im * step)
    den = a_re * a_re + a_im * a_im
    f_re = ((lam_re - 1.0) * a_re + lam_im * a_im) / den
    f_im = (lam_im * a_re - (lam_re - 1.0) * a_im) / den
    bb_re = f_re[..., None] * b_re - f_im[..., None] * b_im
    bb_im = f_re[..., None] * b_im + f_im[..., None] * b_re
    return lam_re, lam_im, bb_re, bb_im


def _route_and_ffn(t, router_w, router_b, wg, bg, wu, bu, wd, bd):
    shp = t.shape
    t2 = t.reshape(-1, shp[-1])
    logits = t2 @ router_w + router_b
    top_val, top_idx = lax.top_k(logits, TOP_K)
    probs = jax.nn.softmax(top_val, axis=-1)
    f = _moe_ffn(t2.astype(BF16), top_idx.astype(jnp.int32), probs, wg, bg, wu, bu, wd, bd)
    return f.reshape(shp)


def kernel(x, c, ctx, c_ctx, ada_w, ada_b, mix_norm_pre, mix_norm_post, ffn_norm_pre, ffn_norm_post, router_w, router_b, moe_w_gate, moe_b_gate, moe_w_up, moe_b_up, moe_w_down, moe_b_down, hy_w_in, gdn_conv_w, gdn_a_log, gdn_dt_bias, gdn_norm_w, ssd_conv_w, ssd_conv_b, ssd_a_log, ssd_dt_bias, ssd_d, ssd_norm_w, hy_w_out, s5_w_in, s5_a_re, s5_a_im, s5_log_step, s5_b_re, s5_b_im, s5_c_re, s5_c_im, s5_d, s5_w_glu_a, s5_w_glu_b):
    depth = ada_w.shape[0]
    rows = x.shape[1] // GRID_W
    h_lat, h_ctx = x, ctx
    cond_lat = jax.nn.silu(c)
    cond_ctx = jax.nn.silu(c_ctx)
    for i in range(depth):
        j = i // 2
        need_ctx = i < depth - 1
        mods_lat = (cond_lat @ ada_w[i] + ada_b[i])[:, None, :]
        mods_ctx = (cond_ctx @ ada_w[i] + ada_b[i])[None, None, :]
        m_lat = jnp.split(mods_lat, 6, axis=-1)
        m_ctx = jnp.split(mods_ctx[0, 0], 6, axis=-1)
        ffn = (moe_w_gate[i].astype(BF16), moe_b_gate[i], moe_w_up[i].astype(BF16), moe_b_up[i],
               moe_w_down[i].astype(BF16), moe_b_down[i], ffn_norm_post[i])
        rw_pad, rb_pad = _pad_router(router_w[i], router_b[i])
        bsz, l, d = h_lat.shape
        lc = h_ctx.shape[1]
        tl_lat, tl_ctx = 512, lc
        if i % 2 == 1:
            assert not need_ctx
            nbg = bsz // S5_BATCH_SUB
            w_in = s5_w_in[j].astype(BF16)
            u_lat = _s5_inproj(h_lat, mods_lat, mix_norm_pre[i], w_in, tl_lat)
            u_ctx = _s5_inproj(h_ctx, mods_ctx, mix_norm_pre[i], w_in, tl_ctx)
            bblk, cblk, lam = _s5_block_params(s5_a_re[j], s5_a_im[j], s5_log_step[j], s5_b_re[j], s5_b_im[j],
                                               s5_c_re[j], s5_c_im[j])
            y = _s5_scan(u_ctx.reshape(nbg, lc, S5_BATCH_SUB, d), u_lat.reshape(nbg, l, S5_BATCH_SUB, d),
                         bblk, cblk, lam)
            h1, t_lat, idx, probs = _s5_glu_postmix(
                y, u_lat, h_lat, mods_lat, s5_d[j], s5_w_glu_a[j].astype(BF16), s5_w_glu_b[j].astype(BF16),
                mix_norm_post[i], ffn_norm_pre[i], rw_pad, rb_pad, tl_lat)
            (h_lat,) = _moe_layer([(t_lat, idx, probs, h1, mods_lat, tl_lat)], *ffn)
            continue
        w_all, conv_w, conv_b, gate_params = _in0_params(hy_w_in[j], gdn_conv_w[j], ssd_conv_w[j], ssd_conv_b[j],
                                                         gdn_a_log[j], gdn_dt_bias[j], ssd_a_log[j], ssd_dt_bias[j])
        conv_l, z_l, g_l, gt_l = _in0_features(h_lat, mods_lat, mix_norm_pre[i], w_all, conv_w, conv_b, gate_params,
                                               rows, GRID_W)
        conv_c, z_c, g_c, gt_c = _in0_features(h_ctx, mods_ctx, mix_norm_pre[i], w_all, conv_w, conv_b, gate_params,
                                               1, lc)
        o_l, o_c = _gdn_scan(conv_l, g_l, gt_l, conv_c, g_c, gt_c)
        y_l, y_c = _ssd_scan(conv_l, g_l, gt_l, conv_c, g_c, gt_c, col0=GDN_CONV_CH)
        w_out = hy_w_out[j].astype(BF16)
        merge = lambda o, y, conv, z, h, mods, tl: _merge0_postmix(
            o, y, conv, z, h, mods, gdn_norm_w[j], ssd_d[j], ssd_norm_w[j], w_out,
            mix_norm_post[i], ffn_norm_pre[i], rw_pad, rb_pad, tl)
        streams = [merge(o_l, y_l, conv_l, z_l, h_lat, mods_lat, tl_lat) + (mods_lat, tl_lat)]
        if need_ctx:
            streams.append(merge(o_c, y_c, conv_c, z_c, h_ctx, mods_ctx, tl_ctx) + (mods_ctx, tl_ctx))
        outs = _moe_layer([(t, idx, probs, h1, mods, tl) for h1, t, idx, probs, mods, tl in streams], *ffn)
        h_lat = outs[0]
        if need_ctx:
            h_ctx = outs[1]
    return h_lat
```

```python
import functools
import math

import jax
import jax.numpy as jnp
import numpy as np
from jax import lax
from jax.experimental import pallas as pl
from jax.experimental.pallas import tpu as pltpu

F32 = jnp.float32
BF16 = jnp.bfloat16

D_MODEL = 1024
GRID_W = 64
RMS_EPS = 1e-6

GDN_HEADS = 4
GDN_DK = 128
GDN_DV = 128
GDN_CHUNK = 64
SSD_HEADS = 8
SSD_HEADDIM = 64
SSD_GROUPS = 2
SSD_STATE = 128
SSD_CHUNK = 128
S5_GROUP = 16
S5_GROUPS = D_MODEL // S5_GROUP
S5_STATE = 64
N_EXPERTS = 32
TOP_K = 4
SWIGLU_LIMIT = 7.0
SWIGLU_ALPHA = 1.702

GDN_V_WIDTH = GDN_HEADS * GDN_DV
SSD_INNER = SSD_HEADS * SSD_HEADDIM
GDN_CONV_CH = 2 * GDN_HEADS * GDN_DK + GDN_V_WIDTH
SSD_CONV_CH = SSD_INNER + 2 * SSD_GROUPS * SSD_STATE
IN0_SIZES = (GDN_CONV_CH, SSD_CONV_CH, GDN_V_WIDTH, SSD_INNER, 2 * GDN_HEADS, 2 * GDN_HEADS, 2 * SSD_HEADS)

V7X_VMEM_LIMIT_BYTES = 56 * 1024 * 1024
MOE_TILE_M = 512
MOE_TILE_F = 512


def _moe_ffn_body(tile_e_ref, tile_ok_ref, x_ref, wg_ref, bg_ref, wu_ref, bu_ref, wd_ref, bd_ref, o_ref,
                  wg_s, wu_s, wd_s):
    i = pl.program_id(0)
    n_f = wg_ref.shape[2] // MOE_TILE_F

    @pl.when((i == 0) | (tile_e_ref[i] != tile_e_ref[jnp.maximum(i - 1, 0)]))
    def _():
        for c in range(n_f):
            cs = slice(c * MOE_TILE_F, (c + 1) * MOE_TILE_F)
            wg_s[:, cs] = wg_ref[0, :, cs].astype(BF16)
            wu_s[:, cs] = wu_ref[0, :, cs].astype(BF16)
            wd_s[cs, :] = wd_ref[0, cs, :].astype(BF16)

    @pl.when(tile_ok_ref[i] > 0)
    def _():
        x = x_ref[...]
        acc = jnp.zeros(o_ref.shape, F32)
        for c in range(n_f):
            cs = slice(c * MOE_TILE_F, (c + 1) * MOE_TILE_F)
            gl = jnp.dot(x, wg_s[:, cs], preferred_element_type=F32) + bg_ref[0, :, cs]
            lin = jnp.dot(x, wu_s[:, cs], preferred_element_type=F32) + bu_ref[0, :, cs]
            gl = jnp.minimum(gl, SWIGLU_LIMIT)
            lin = jnp.clip(lin, -SWIGLU_LIMIT, SWIGLU_LIMIT)
            act = gl * jax.nn.sigmoid(SWIGLU_ALPHA * gl) * (lin + 1.0)
            acc = acc + jnp.dot(act.astype(BF16), wd_s[cs, :], preferred_element_type=F32)
        o_ref[...] = (acc + bd_ref[0]).astype(o_ref.dtype)

    @pl.when(tile_ok_ref[i] == 0)
    def _():
        o_ref[...] = jnp.zeros(o_ref.shape, o_ref.dtype)


def _moe_grouped_ffn(xs, tile_e, tile_ok, wg, bg, wu, bu, wd, bd):
    p, d = xs.shape
    e, _, f = wg.shape
    n_tiles = p // MOE_TILE_M
    grid_spec = pltpu.PrefetchScalarGridSpec(
        num_scalar_prefetch=2,
        grid=(n_tiles,),
        in_specs=[
            pl.BlockSpec((MOE_TILE_M, d), lambda i, te, ok: (i, 0)),
            pl.BlockSpec((1, d, f), lambda i, te, ok: (te[i], 0, 0)),
            pl.BlockSpec((1, 1, f), lambda i, te, ok: (te[i], 0, 0)),
            pl.BlockSpec((1, d, f), lambda i, te, ok: (te[i], 0, 0)),
            pl.BlockSpec((1, 1, f), lambda i, te, ok: (te[i], 0, 0)),
            pl.BlockSpec((1, f, d), lambda i, te, ok: (te[i], 0, 0)),
            pl.BlockSpec((1, 1, d), lambda i, te, ok: (te[i], 0, 0)),
        ],
        out_specs=pl.BlockSpec((MOE_TILE_M, d), lambda i, te, ok: (i, 0)),
        scratch_shapes=[pltpu.VMEM((d, f), BF16), pltpu.VMEM((d, f), BF16), pltpu.VMEM((f, d), BF16)],
    )
    return pl.pallas_call(
        _moe_ffn_body,
        grid_spec=grid_spec,
        out_shape=jax.ShapeDtypeStruct((p, d), BF16),
        compiler_params=pltpu.CompilerParams(
            dimension_semantics=("arbitrary",), vmem_limit_bytes=V7X_VMEM_LIMIT_BYTES),
        name="moe_grouped_ffn",
    )(tile_e, tile_ok, xs, wg, bg.reshape(e, 1, f), wu, bu.reshape(e, 1, f), wd, bd.reshape(e, 1, d))


def _moe_combine_body(y0_ref, y1_ref, y2_ref, y3_ref, p_ref, h_ref, gate_ref, nw_ref, o_ref):
    p = p_ref[0]
    acc = jnp.zeros(o_ref.shape[1:], F32)
    for k, y_ref in enumerate((y0_ref, y1_ref, y2_ref, y3_ref)):
        acc = acc + p[:, k:k + 1] * y_ref[0, 0].astype(F32)
    o_ref[0] = h_ref[0] + gate_ref[0] * _k_rms(acc, nw_ref[...])


def _moe_combine(yg, probs, h, mods, norm_w, tl):
    b, l, d = h.shape
    nb = mods.shape[0]
    bsel = (lambda bi: bi) if nb > 1 else (lambda bi: 0)
    yk = lambda k: pl.BlockSpec((1, 1, tl, d), lambda bi, ti: (k, bi, ti, 0))
    return pl.pallas_call(
        _moe_combine_body,
        grid=(b, l // tl),
        in_specs=[yk(0), yk(1), yk(2), yk(3),
                  pl.BlockSpec((1, tl, 128), lambda bi, ti: (bi, ti, 0)),
                  pl.BlockSpec((1, tl, d), lambda bi, ti: (bi, ti, 0)),
                  pl.BlockSpec((1, 1, d), lambda bi, ti: (bsel(bi), 0, 5)),
                  pl.BlockSpec((1, d), lambda bi, ti: (0, 0))],
        out_specs=pl.BlockSpec((1, tl, d), lambda bi, ti: (bi, ti, 0)),
        out_shape=jax.ShapeDtypeStruct((b, l, d), F32),
        compiler_params=pltpu.CompilerParams(
            dimension_semantics=("arbitrary", "arbitrary"), vmem_limit_bytes=V7X_VMEM_LIMIT_BYTES),
        name="moe_combine",
    )(yg, yg, yg, yg, probs, h, mods, norm_w.reshape(1, -1))


def _moe_route(top_idx):
    t = top_idx.shape[0]
    a = t * TOP_K
    tm = MOE_TILE_M
    eid = top_idx.reshape(a)
    order = jnp.argsort(eid, stable=True).astype(jnp.int32)
    inv = jnp.argsort(order).astype(jnp.int32)
    counts = jnp.sum(jax.nn.one_hot(eid, N_EXPERTS, dtype=jnp.int32), axis=0)
    off = jnp.cumsum(counts) - counts
    pcounts = ((counts + tm - 1) // tm) * tm
    pend = jnp.cumsum(pcounts)
    poff = pend - pcounts
    n_tiles = a // tm + N_EXPERTS
    tile_start = jnp.arange(n_tiles, dtype=jnp.int32) * tm
    tile_e = jnp.minimum(jnp.searchsorted(pend, tile_start, side='right'), N_EXPERTS - 1).astype(jnp.int32)
    tile_ok = (tile_start < pend[-1]).astype(jnp.int32)
    ppos = jnp.arange(n_tiles * tm, dtype=jnp.int32)
    pe = jnp.repeat(tile_e, tm)
    r = ppos - poff[pe]
    src_rank = jnp.clip(off[pe] + jnp.minimum(r, counts[pe] - 1), 0, a - 1)
    src_tok = order[src_rank] // TOP_K
    pos = poff[eid] + (inv - off[eid])
    return src_tok, pos, tile_e, tile_ok


def _moe_layer(streams, wg, bg, wu, bu, wd, bd, norm_w):
    d = streams[0][0].shape[-1]
    t_all = jnp.concatenate([s[0].reshape(-1, d) for s in streams], axis=0)
    idx_all = jnp.concatenate([s[1].reshape(-1, 128)[:, :TOP_K] for s in streams], axis=0)
    src_tok, pos, tile_e, tile_ok = _moe_route(idx_all)
    xs = t_all.at[src_tok].get(mode="promise_in_bounds")
    ys = _moe_grouped_ffn(xs, tile_e, tile_ok, wg, bg, wu, bu, wd, bd)
    pos_k = pos.reshape(-1, TOP_K).T
    outs, start = [], 0
    for t, _, probs, h, mods, tl in streams:
        b, l, _ = t.shape
        n = b * l
        yg = ys.at[pos_k[:, start:start + n].reshape(-1)].get(mode="promise_in_bounds")
        outs.append(_moe_combine(yg.reshape(TOP_K, b, l, d), probs, h, mods, norm_w, tl))
        start += n
    return outs


def _k_rms(t, w):
    return t * lax.rsqrt(jnp.mean(t * t, axis=-1, keepdims=True) + RMS_EPS) * w


def _k_post_mix(y, h, gate, post_w, pre_w, shift, scale, rw, rb):
    h1 = h + gate * _k_rms(y, post_w)
    t = _k_rms(h1, pre_w) * (1.0 + scale) + shift
    logits = jnp.dot(t, rw, preferred_element_type=F32) + rb
    lane = lax.broadcasted_iota(jnp.int32, logits.shape, 1)
    idx_out = jnp.zeros(logits.shape, jnp.int32)
    val_out = jnp.zeros(logits.shape, F32)
    work = logits
    m0 = None
    for k in range(TOP_K):
        m = jnp.max(work, axis=-1, keepdims=True)
        sel = jnp.min(jnp.where(work == m, lane, 128), axis=-1, keepdims=True)
        if k == 0:
            m0 = m
        idx_out = jnp.where(lane == k, sel, idx_out)
        val_out = jnp.where(lane == k, jnp.exp(m - m0), val_out)
        work = jnp.where(lane == sel, -jnp.inf, work)
    probs = val_out / jnp.sum(val_out, axis=-1, keepdims=True)
    return h1, t, idx_out, probs


S5_BATCH_SUB = 8
S5_CH_BLOCK = 128
S5_ST_BLOCK = (S5_CH_BLOCK // S5_GROUP) * S5_STATE
S5_TL = 256


def _s5_inproj_body(h_ref, sh_ref, sc_ref, nw_ref, w_ref, u_ref):
    a = _k_rms(h_ref[0], nw_ref[...]) * (1.0 + sc_ref[0]) + sh_ref[0]
    u_ref[0] = jnp.dot(a.astype(BF16), w_ref[...], preferred_element_type=F32)


def _s5_inproj(h, mods, norm_w, w_bf16, tl):
    b, l, d = h.shape
    nb = mods.shape[0]
    bsel = (lambda bi: bi) if nb > 1 else (lambda bi: 0)
    return pl.pallas_call(
        _s5_inproj_body,
        grid=(b, l // tl),
        in_specs=[
            pl.BlockSpec((1, tl, d), lambda bi, ti: (bi, ti, 0)),
            pl.BlockSpec((1, 1, d), lambda bi, ti: (bsel(bi), 0, 0)),
            pl.BlockSpec((1, 1, d), lambda bi, ti: (bsel(bi), 0, 1)),
            pl.BlockSpec((1, d), lambda bi, ti: (0, 0)),
            pl.BlockSpec((d, d), lambda bi, ti: (0, 0)),
        ],
        out_specs=pl.BlockSpec((1, tl, d), lambda bi, ti: (bi // S5_BATCH_SUB, ti, bi % S5_BATCH_SUB)),
        out_shape=jax.ShapeDtypeStruct((b // S5_BATCH_SUB, l, S5_BATCH_SUB * d), F32),
        compiler_params=pltpu.CompilerParams(
            dimension_semantics=("arbitrary", "arbitrary"), vmem_limit_bytes=V7X_VMEM_LIMIT_BYTES),
        name="s5_inproj",
    )(h, mods, mods, norm_w.reshape(1, d), w_bf16)


def _s5_scan_body(uc_ref, ul_ref, b_ref, c_ref, lam_ref, y_ref, buf_ref, st_ref):
    d = pl.program_id(0)
    tc = pl.program_id(3)
    tl = ul_ref.shape[1]
    rows = tl * S5_BATCH_SUB
    ns = S5_ST_BLOCK

    @pl.when(tc == 0)
    def _():
        st_ref[...] = jnp.zeros(st_ref.shape, F32)
        u2 = uc_ref[0].reshape(rows, S5_CH_BLOCK).astype(BF16)
        buf_ref[...] = jnp.dot(u2, b_ref[0, 0], preferred_element_type=F32)

    @pl.when(tc > 0)
    def _():
        u2 = ul_ref[0].reshape(rows, S5_CH_BLOCK).astype(BF16)
        buf_ref[...] = jnp.dot(u2, b_ref[0, 0], preferred_element_type=F32)

    lam = lam_ref[0, 0]
    lr = jnp.broadcast_to(lam[0:1], (S5_BATCH_SUB, ns))
    li = jnp.broadcast_to(lam[1:2], (S5_BATCH_SUB, ns))

    def step(i, carry):
        xr, xi = carry
        t = jnp.where(d == 0, i, tl - 1 - i)
        r0 = pl.multiple_of(t * S5_BATCH_SUB, S5_BATCH_SUB)
        br = buf_ref[pl.ds(r0, S5_BATCH_SUB), 0:ns]
        bi = buf_ref[pl.ds(r0, S5_BATCH_SUB), ns:2 * ns]
        nr = lr * xr - li * xi + br
        ni = lr * xi + li * xr + bi
        buf_ref[pl.ds(r0, S5_BATCH_SUB), 0:ns] = nr
        buf_ref[pl.ds(r0, S5_BATCH_SUB), ns:2 * ns] = ni
        return nr, ni

    xr, xi = lax.fori_loop(0, tl, step, (st_ref[0], st_ref[1]), unroll=8)
    st_ref[0] = xr
    st_ref[1] = xi

    @pl.when(tc > 0)
    def _():
        y = jnp.dot(buf_ref[...].astype(BF16), c_ref[0, 0], preferred_element_type=F32)
        y_ref[0, 0] = y.reshape(tl, S5_BATCH_SUB, S5_CH_BLOCK)


def _s5_scan(u_ctx, u_lat, bblk, cblk, lam):
    nbg, l_lat, _, d = u_lat.shape
    assert u_ctx.shape[1] == S5_TL and l_lat % S5_TL == 0
    n_lc = l_lat // S5_TL
    n_sb = d // S5_CH_BLOCK

    def lat_chunk(di, tc):
        j = jnp.maximum(tc - 1, 0)
        return jnp.where(di == 0, j, n_lc - 1 - j)

    return pl.pallas_call(
        _s5_scan_body,
        grid=(2, nbg, n_sb, n_lc + 1),
        in_specs=[
            pl.BlockSpec((1, S5_TL, S5_BATCH_SUB, S5_CH_BLOCK), lambda di, bg, sb, tc: (bg, 0, 0, sb)),
            pl.BlockSpec((1, S5_TL, S5_BATCH_SUB, S5_CH_BLOCK), lambda di, bg, sb, tc: (bg, lat_chunk(di, tc), 0, sb)),
            pl.BlockSpec((1, 1, S5_CH_BLOCK, 2 * S5_ST_BLOCK), lambda di, bg, sb, tc: (di, sb, 0, 0)),
            pl.BlockSpec((1, 1, 2 * S5_ST_BLOCK, S5_CH_BLOCK), lambda di, bg, sb, tc: (di, sb, 0, 0)),
            pl.BlockSpec((1, 1, 2, S5_ST_BLOCK), lambda di, bg, sb, tc: (di, sb, 0, 0)),
        ],
        out_specs=pl.BlockSpec((1, 1, S5_TL, S5_BATCH_SUB, S5_CH_BLOCK),
                               lambda di, bg, sb, tc: (di, bg, lat_chunk(di, tc), 0, sb)),
        out_shape=jax.ShapeDtypeStruct((2, nbg, l_lat, S5_BATCH_SUB, d), F32),
        scratch_shapes=[
            pltpu.VMEM((S5_TL * S5_BATCH_SUB, 2 * S5_ST_BLOCK), F32),
            pltpu.VMEM((2, S5_BATCH_SUB, S5_ST_BLOCK), F32),
        ],
        compiler_params=pltpu.CompilerParams(
            dimension_semantics=("arbitrary", "arbitrary", "arbitrary", "arbitrary"),
            vmem_limit_bytes=V7X_VMEM_LIMIT_BYTES),
        name="s5_scan",
    )(u_ctx, u_lat, bblk, cblk, lam)


def _s5_glu_body(y0_ref, y1_ref, u_ref, h_ref, gate_ref, sh_ref, sc_ref, dsk_ref, wa_ref, wb_ref,
                 postw_ref, prew_ref, rw_ref, rb_ref, h1_ref, t_ref, idx_ref, p_ref):
    y = dsk_ref[...] * u_ref[0] + y0_ref[0, 0] + y1_ref[0, 0]
    o = (0.5 * y * (1.0 + jnp.tanh(math.sqrt(2.0 / math.pi) * (y + 0.044715 * (y * y * y))))).astype(BF16)
    ga = jnp.dot(o, wa_ref[...], preferred_element_type=F32)
    gb = jnp.dot(o, wb_ref[...], preferred_element_type=F32)
    ym = ga * jax.nn.sigmoid(gb)
    h1, t, idx, probs = _k_post_mix(ym, h_ref[0], gate_ref[0], postw_ref[...], prew_ref[...],
                                    sh_ref[0], sc_ref[0], rw_ref[...], rb_ref[...])
    h1_ref[0] = h1
    t_ref[0] = t.astype(BF16)
    idx_ref[0] = idx
    p_ref[0] = probs


def _s5_glu_postmix(y, u, h, mods, d_skip, wa, wb, post_w, pre_w, rw_pad, rb_pad, tl):
    b, l, d = h.shape
    nbg = b // S5_BATCH_SUB
    y4 = y.reshape(2, nbg, l, S5_BATCH_SUB * d)
    u3 = u.reshape(nbg, l, S5_BATCH_SUB * d)
    row = lambda v: v.reshape(1, -1)
    full = lambda shape: pl.BlockSpec(shape, lambda bi, ti: (0,) * len(shape))
    mod = lambda k: pl.BlockSpec((1, 1, d), lambda bi, ti: (bi, 0, k))
    tok = lambda w: pl.BlockSpec((1, tl, w), lambda bi, ti: (bi, ti, 0))
    return pl.pallas_call(
        _s5_glu_body,
        grid=(b, l // tl),
        in_specs=[
            pl.BlockSpec((1, 1, tl, d), lambda bi, ti: (0, bi // S5_BATCH_SUB, ti, bi % S5_BATCH_SUB)),
            pl.BlockSpec((1, 1, tl, d), lambda bi, ti: (1, bi // S5_BATCH_SUB, ti, bi % S5_BATCH_SUB)),
            pl.BlockSpec((1, tl, d), lambda bi, ti: (bi // S5_BATCH_SUB, ti, bi % S5_BATCH_SUB)),
            tok(d), mod(2), mod(3), mod(4),
            full((1, d)), full((d, d)), full((d, d)), full((1, d)), full((1, d)), full((d, 128)), full((1, 128)),
        ],
        out_specs=[tok(d), tok(d), tok(128), tok(128)],
        out_shape=[
            jax.ShapeDtypeStruct((b, l, d), F32),
            jax.ShapeDtypeStruct((b, l, d), BF16),
            jax.ShapeDtypeStruct((b, l, 128), jnp.int32),
            jax.ShapeDtypeStruct((b, l, 128), F32),
        ],
        compiler_params=pltpu.CompilerParams(
            dimension_semantics=("arbitrary", "arbitrary"), vmem_limit_bytes=V7X_VMEM_LIMIT_BYTES),
        name="s5_glu_postmix",
    )(y4, y4, u3, h, mods, mods, mods, row(d_skip), wa, wb, row(post_w), row(pre_w), rw_pad, rb_pad)


def _s5_block_params(a_re, a_im, log_step, b_re, b_im, c_re, c_im):
    gpb = S5_CH_BLOCK // S5_GROUP
    eye = jnp.eye(gpb, dtype=F32)
    bblks, cblks, lams = [], [], []
    for di in range(2):
        lam_re, lam_im, bb_re, bb_im = _s5_discretize(a_re[di], a_im[di], log_step[di], b_re[di], b_im[di])
        n_sb = lam_re.shape[0] // gpb

        def bdiag_in(bb):
            t = bb.reshape(n_sb, gpb, S5_STATE, S5_GROUP)
            return jnp.einsum('sgph,gk->sghkp', t, eye).reshape(n_sb, S5_CH_BLOCK, S5_ST_BLOCK)

        def bdiag_out(cc):
            t = cc.reshape(n_sb, gpb, S5_GROUP, S5_STATE)
            return jnp.einsum('sghp,gk->sgpkh', t, eye).reshape(n_sb, S5_ST_BLOCK, S5_CH_BLOCK)

        bblks.append(jnp.concatenate([bdiag_in(bb_re), bdiag_in(bb_im)], axis=-1))
        cblks.append(jnp.concatenate([bdiag_out(c_re[di]), -bdiag_out(c_im[di])], axis=-2))
        lams.append(jnp.stack([lam_re.reshape(n_sb, S5_ST_BLOCK), lam_im.reshape(n_sb, S5_ST_BLOCK)], axis=1))
    return jnp.stack(bblks).astype(BF16), jnp.stack(cblks).astype(BF16), jnp.stack(lams)


def _pad_router(router_w, router_b):
    e = router_w.shape[1]
    rw = jnp.pad(router_w, ((0, 0), (0, 128 - e)))
    rb = jnp.pad(router_b, (0, 128 - e), constant_values=-1e30).reshape(1, 128)
    return rw, rb


CONV_CH = GDN_CONV_CH + SSD_CONV_CH
Z_CH = GDN_V_WIDTH + SSD_INNER
IN0_TILE = 256
IN0_CONV_TILES = CONV_CH // IN0_TILE
IN0_Z_TILES = Z_CH // IN0_TILE
IN0_TILES = IN0_CONV_TILES + IN0_Z_TILES + 1
CONV_ROW_CHUNK = 256


def _conv_halo(cols):
    return ((cols + 1 + 7) // 8) * 8


def _in0_body(h_ref, sh_ref, sc_ref, nw_ref, w_ref, cw_ref, cb_ref, gp_ref,
              conv_ref, z_ref, g_ref, gt_ref, a_s, p0_s, pm_s, pp_s, *, n_rows, n_cols):
    n = pl.program_id(1)
    l = h_ref.shape[1]
    halo = _conv_halo(n_cols)
    rc = min(CONV_ROW_CHUNK, l)

    @pl.when(n == 0)
    def _():
        def norm_rows(i, carry):
            r0 = pl.multiple_of(i * rc, rc)
            a = _k_rms(h_ref[0, pl.ds(r0, rc), :], nw_ref[...]) * (1.0 + sc_ref[0]) + sh_ref[0]
            a_s[pl.ds(r0, rc), :] = a.astype(BF16)
            return carry
        lax.fori_loop(0, l // rc, norm_rows, 0)

    p = jnp.dot(a_s[...], w_ref[...], preferred_element_type=F32)

    @pl.when(n < IN0_CONV_TILES)
    def _():
        zero_halo = jnp.zeros((halo, IN0_TILE), F32)
        for s in (p0_s, pm_s, pp_s):
            s[pl.ds(0, halo), :] = zero_halo
            s[pl.ds(halo + l, halo), :] = zero_halo
        p0_s[pl.ds(halo, l), :] = p
        tcol = lax.broadcasted_iota(jnp.int32, (l, IN0_TILE), 0) % n_cols
        pm_s[pl.ds(halo, l), :] = jnp.where(tcol != 0, p0_s[pl.ds(halo - 1, l), :], 0.0)
        pp_s[pl.ds(halo, l), :] = jnp.where(tcol != n_cols - 1, p0_s[pl.ds(halo + 1, l), :], 0.0)
        dys = (0,) if n_rows == 1 else (-1, 0, 1)

        def conv_rows(i, carry):
            r0 = pl.multiple_of(i * rc, rc)
            acc = jnp.zeros((rc, IN0_TILE), F32) + cb_ref[...]
            for dy in dys:
                base = halo + dy * n_cols
                for dx, src in ((0, pm_s), (1, p0_s), (2, pp_s)):
                    tap = (dy + 1) * 3 + dx
                    acc = acc + cw_ref[tap:tap + 1, :] * src[pl.ds(r0 + base, rc), :]
            conv_ref[0, pl.ds(r0, rc), :] = (acc * jax.nn.sigmoid(acc)).astype(conv_ref.dtype)
            return carry
        lax.fori_loop(0, l // rc, conv_rows, 0)

    @pl.when((n >= IN0_CONV_TILES) & (n < IN0_CONV_TILES + IN0_Z_TILES))
    def _():
        z_ref[0] = (p * jax.nn.sigmoid(p)).astype(z_ref.dtype)

    @pl.when(n == IN0_TILES - 1)
    def _():
        pg = p[:, :128]
        lane = lax.broadcasted_iota(jnp.int32, pg.shape, 1)
        xb = pg + gp_ref[0:1, :]
        sp = jnp.maximum(xb, 0.0) + jnp.log(1.0 + jnp.exp(-jnp.abs(xb)))
        neg_a_sp = -jnp.exp(gp_ref[1:2, :]) * sp
        gates = jnp.where(lane < 2 * GDN_HEADS, neg_a_sp,
                          jnp.where(lane < GATE_COL_DT, jax.nn.sigmoid(pg),
                                    jnp.where(lane < GATE_COL_DA, sp,
                                              jnp.where(lane < GATE_COL_DA + 2 * SSD_HEADS, neg_a_sp, 0.0))))
        g_ref[0] = gates
        for ci in range(l // SCAN_C):
            gt_ref[0, ci] = gates[ci * SCAN_C:(ci + 1) * SCAN_C, :].T


def _in0_features(h, mods, norm_w, w_all, conv_w, conv_b, gate_params, n_rows, n_cols):
    b, l, d = h.shape
    nb = mods.shape[0]
    bsel = (lambda bi: bi) if nb > 1 else (lambda bi: 0)
    halo = _conv_halo(n_cols)
    nct, nzt = IN0_CONV_TILES, IN0_Z_TILES
    pad_rows = l + 2 * halo
    return pl.pallas_call(
        functools.partial(_in0_body, n_rows=n_rows, n_cols=n_cols),
        grid=(b, IN0_TILES),
        in_specs=[
            pl.BlockSpec((1, l, d), lambda bi, ni: (bi, 0, 0)),
            pl.BlockSpec((1, 1, d), lambda bi, ni: (bsel(bi), 0, 0)),
            pl.BlockSpec((1, 1, d), lambda bi, ni: (bsel(bi), 0, 1)),
            pl.BlockSpec((1, d), lambda bi, ni: (0, 0)),
            pl.BlockSpec((d, IN0_TILE), lambda bi, ni: (0, ni)),
            pl.BlockSpec((9, IN0_TILE), lambda bi, ni: (0, jnp.minimum(ni, nct - 1))),
            pl.BlockSpec((1, IN0_TILE), lambda bi, ni: (0, jnp.minimum(ni, nct - 1))),
            pl.BlockSpec((2, 128), lambda bi, ni: (0, 0)),
        ],
        out_specs=[
            pl.BlockSpec((1, l, IN0_TILE), lambda bi, ni: (bi, 0, jnp.minimum(ni, nct - 1))),
            pl.BlockSpec((1, l, IN0_TILE), lambda bi, ni: (bi, 0, jnp.clip(ni - nct, 0, nzt - 1))),
            pl.BlockSpec((1, l, 128), lambda bi, ni: (bi, 0, 0)),
            pl.BlockSpec((1, l // SCAN_C, 128, SCAN_C), lambda bi, ni: (bi, 0, 0, 0)),
        ],
        out_shape=[
            jax.ShapeDtypeStruct((b, l, CONV_CH), BF16),
            jax.ShapeDtypeStruct((b, l, Z_CH), BF16),
            jax.ShapeDtypeStruct((b, l, 128), F32),
            jax.ShapeDtypeStruct((b, l // SCAN_C, 128, SCAN_C), F32),
        ],
        scratch_shapes=[
            pltpu.VMEM((l, d), BF16),
            pltpu.VMEM((pad_rows, IN0_TILE), F32),
            pltpu.VMEM((pad_rows, IN0_TILE), F32),
            pltpu.VMEM((pad_rows, IN0_TILE), F32),
        ],
        compiler_params=pltpu.CompilerParams(
            dimension_semantics=("arbitrary", "arbitrary"), vmem_limit_bytes=V7X_VMEM_LIMIT_BYTES),
        name="in0_features",
    )(h, mods, mods, norm_w.reshape(1, d), w_all, conv_w, conv_b, gate_params)


def _in0_params(w_in, gdn_conv_w, ssd_conv_w, ssd_conv_b, gdn_a_log, gdn_dt_bias, ssd_a_log, ssd_dt_bias):
    splits = np.cumsum(IN0_SIZES)[:-1].tolist()
    w_ca, w_cb, w_za, w_zb, w_a, w_b, w_dt = jnp.split(w_in, splits, axis=1)
    d = w_in.shape[0]
    w_gate = jnp.concatenate([w_a, w_b, w_dt, w_dt], axis=1)
    w_gate = jnp.pad(w_gate, ((0, 0), (0, IN0_TILE - w_gate.shape[1])))
    w_all = jnp.concatenate([w_ca, w_cb, w_za, w_zb, w_gate], axis=1).astype(BF16)
    conv_w = jnp.concatenate([gdn_conv_w, ssd_conv_w], axis=-1).reshape(9, CONV_CH)
    conv_b = jnp.concatenate([jnp.zeros((GDN_CONV_CH,), F32), ssd_conv_b]).reshape(1, CONV_CH)
    z8 = jnp.zeros((2 * GDN_HEADS,), F32)
    z16 = jnp.zeros((2 * SSD_HEADS,), F32)
    tail = jnp.zeros((128 - GATE_COL_DA - 2 * SSD_HEADS,), F32)
    bias = jnp.concatenate([gdn_dt_bias.reshape(-1), z8, ssd_dt_bias.reshape(-1), ssd_dt_bias.reshape(-1), tail])
    alog = jnp.concatenate([gdn_a_log.reshape(-1), z8, z16, ssd_a_log.reshape(-1), tail])
    return w_all, conv_w, conv_b, jnp.stack([bias, alog])


def _merge0_body(o_ref, y_ref, xs_ref, z_ref, h_ref, gate_ref, sh_ref, sc_ref, gnw_ref, dsk_ref, snw_ref, wo_ref,
                 postw_ref, prew_ref, rw_ref, rb_ref, h1_ref, t_ref, idx_ref, p_ref):
    z = z_ref[0].astype(F32)
    parts = []
    for hd in range(GDN_HEADS):
        cs = slice(hd * GDN_DV, (hd + 1) * GDN_DV)
        parts.append(_k_rms(o_ref[0, :, cs], gnw_ref[...]) * z[:, cs])
    gw = SSD_INNER // SSD_GROUPS
    for g in range(SSD_GROUPS):
        cs = slice(g * gw, (g + 1) * gw)
        y2 = (y_ref[0, :, cs] + dsk_ref[:, cs] * xs_ref[0, :, cs].astype(F32)) * z[:, GDN_V_WIDTH + g * gw:GDN_V_WIDTH + (g + 1) * gw]
        parts.append(_k_rms(y2, snw_ref[:, cs]))
    mixed = jnp.concatenate(parts, axis=-1).astype(BF16)
    ym = jnp.dot(mixed, wo_ref[...], preferred_element_type=F32)
    h1, t, idx, probs = _k_post_mix(ym, h_ref[0], gate_ref[0], postw_ref[...], prew_ref[...],
                                    sh_ref[0], sc_ref[0], rw_ref[...], rb_ref[...])
    h1_ref[0] = h1
    t_ref[0] = t.astype(BF16)
    idx_ref[0] = idx
    p_ref[0] = probs


def _merge0_postmix(o, y, conv, z, h, mods, gdn_norm_w, ssd_d, ssd_norm_w, w_out, post_w, pre_w, rw_pad, rb_pad, tl):
    b, l, d = h.shape
    nb = mods.shape[0]
    bsel = (lambda bi: bi) if nb > 1 else (lambda bi: 0)
    row = lambda v: v.reshape(1, -1)
    full = lambda shape: pl.BlockSpec(shape, lambda bi, ti: (0,) * len(shape))
    mod = lambda k: pl.BlockSpec((1, 1, d), lambda bi, ti: (bsel(bi), 0, k))
    tok = lambda w: pl.BlockSpec((1, tl, w), lambda bi, ti: (bi, ti, 0))
    xs_block = GDN_CONV_CH // SSD_INNER
    dsk = jnp.repeat(ssd_d, SSD_HEADDIM)
    return pl.pallas_call(
        _merge0_body,
        grid=(b, l // tl),
        in_specs=[
            tok(GDN_V_WIDTH), tok(SSD_INNER),
            pl.BlockSpec((1, tl, SSD_INNER), lambda bi, ti: (bi, ti, xs_block)),
            tok(Z_CH), tok(d), mod(2), mod(3), mod(4),
            full((1, GDN_DV)), full((1, SSD_INNER)), full((1, SSD_INNER)), full((Z_CH, d)),
            full((1, d)), full((1, d)), full((d, 128)), full((1, 128)),
        ],
        out_specs=[tok(d), tok(d), tok(128), tok(128)],
        out_shape=[
            jax.ShapeDtypeStruct((b, l, d), F32),
            jax.ShapeDtypeStruct((b, l, d), BF16),
            jax.ShapeDtypeStruct((b, l, 128), jnp.int32),
            jax.ShapeDtypeStruct((b, l, 128), F32),
        ],
        compiler_params=pltpu.CompilerParams(
            dimension_semantics=("arbitrary", "arbitrary"), vmem_limit_bytes=V7X_VMEM_LIMIT_BYTES),
        name="merge0_postmix",
    )(o, y, conv, z, h, mods, mods, mods, row(gdn_norm_w), row(dsk), row(ssd_norm_w), w_out,
      row(post_w), row(pre_w), rw_pad, rb_pad)


SCAN_C = 128
GDN_INV_BLOCK = 16


def _dot(a, b):
    return jnp.dot(a, b, preferred_element_type=F32)


def _dot_nt(a, b):
    return lax.dot_general(a, b, (((1,), (1,)), ((), ())), preferred_element_type=F32)


def _dot_tn(a, b):
    return lax.dot_general(a, b, (((0,), (0,)), ((), ())), preferred_element_type=F32)


def _bdot(a, b):
    return _dot(a.astype(BF16), b.astype(BF16))


def _bdot_nt(a, b):
    return _dot_nt(a.astype(BF16), b.astype(BF16))


def _bdot_tn(a, b):
    return _dot_tn(a.astype(BF16), b.astype(BF16))


def _scan_masks(fwd):
    row = lax.broadcasted_iota(jnp.int32, (SCAN_C, SCAN_C), 0)
    col = lax.broadcasted_iota(jnp.int32, (SCAN_C, SCAN_C), 1)
    lead = (row - col) * jnp.where(fwd, 1, -1)
    return row, col, lead >= 0, lead <= 0, lead > 0


def _cumsum_col_row(g_col, g_row, incl, incl_t):
    gc_col = jnp.sum(jnp.where(incl, g_row, 0.0), axis=1, keepdims=True)
    gc_row = jnp.sum(jnp.where(incl_t, g_col, 0.0), axis=0, keepdims=True)
    return gc_col, gc_row


def _unit_tri_inverse(ms, row, col):
    eye = (row == col).astype(F32)
    same = (row // GDN_INV_BLOCK) == (col // GDN_INV_BLOCK)
    mds = [jnp.where(same, m, 0.0) for m in ms]
    mos = [m - md for m, md in zip(ms, mds)]
    ps = mds
    tds = [eye - md for md in mds]
    for _ in range(int(math.log2(GDN_INV_BLOCK)) - 1):
        ps = [_dot(p, p) for p in ps]
        tds = [td + _dot(td, p) for td, p in zip(tds, ps)]
    ps = [_dot(td, mo) for td, mo in zip(tds, mos)]
    qs = [eye - n for n in ps]
    for _ in range(int(math.log2(SCAN_C // GDN_INV_BLOCK)) - 1):
        ps = [_dot(p, p) for p in ps]
        qs = [q + _dot(q, p) for q, p in zip(qs, ps)]
    return [_dot(q, td) for q, td in zip(qs, tds)]


GDN_PREP_GROUP = 8


def _gdn_body(ql_ref, kl_ref, vl_ref, gl_ref, gtl_ref, qc_ref, kc_ref, vc_ref, gc_ref, gtc_ref,
              ol_ref, oc_ref, u_s, w_s, qk_s, qd_s, kd_s, dec_s, obl_s, obc_s):
    h = pl.program_id(1)
    c = SCAN_C
    ncc = qc_ref.shape[1] // c
    ncl = ql_ref.shape[1] // c
    lane = lax.broadcasted_iota(jnp.int32, (c, 128), 1)

    def order(d, i, n):
        return i if d == 0 else n - 1 - i

    def prep(d, q_ref, k_ref, v_ref, g_ref, gt_ref, cis, ps):
        row, col, incl, incl_t, strict = _scan_masks(d == 0)
        col_g = d * GDN_HEADS + h
        col_b = 2 * GDN_HEADS + col_g
        loaded = []
        for ci in cis:
            t0 = pl.multiple_of(ci * c, c)
            loaded.append((q_ref[0, pl.ds(t0, c), :].astype(F32), k_ref[0, pl.ds(t0, c), :].astype(F32),
                           v_ref[0, pl.ds(t0, c), :].astype(F32), g_ref[0, pl.ds(t0, c), :],
                           gt_ref[0, ci, pl.ds(col_g, 1), :]))
        parts, ms = [], []
        for q, k, v, gch, g_row in loaded:
            q = q * lax.rsqrt(jnp.sum(q * q, axis=-1, keepdims=True) + RMS_EPS) * (GDN_DK ** -0.5)
            k = k * lax.rsqrt(jnp.sum(k * k, axis=-1, keepdims=True) + RMS_EPS)
            g_col = jnp.sum(jnp.where(lane == col_g, gch, 0.0), axis=1, keepdims=True)
            b_col = jnp.sum(jnp.where(lane == col_b, gch, 0.0), axis=1, keepdims=True)
            gc_col, gc_row = _cumsum_col_row(g_col, g_row, incl, incl_t)
            g_tot = jnp.sum(g_col, axis=0, keepdims=True)
            decay = jnp.where(incl, jnp.exp(jnp.where(incl, gc_col - gc_row, 0.0)), 0.0)
            kb = k * b_col
            ms.append(jnp.where(strict, _bdot_nt(kb, k) * decay, 0.0))
            egc = jnp.exp(gc_col)
            parts.append((v * b_col, kb * egc, _bdot_nt(q, k) * decay, q * egc, k * jnp.exp(g_tot - gc_col),
                          jnp.broadcast_to(jnp.exp(g_tot), (1, 128))))
        t_invs = _unit_tri_inverse(ms, row, col)
        uw = [(_bdot(t_inv, part[0]), _bdot(t_inv, part[1])) for t_inv, part in zip(t_invs, parts)]
        for p, (u, w), (_, _, qk, qd, kd, dec) in zip(ps, uw, parts):
            p0 = pl.multiple_of(p * c, c)
            u_s[d, pl.ds(p0, c), :] = u
            w_s[d, pl.ds(p0, c), :] = w
            qk_s[d, pl.ds(p0, c), :] = qk
            qd_s[d, pl.ds(p0, c), :] = qd
            kd_s[d, pl.ds(p0, c), :] = kd
            dec_s[d, pl.ds(p, 1), :] = dec

    def advance(of_ref, ob_ref, i, n, p, states):
        p0 = pl.multiple_of(p * c, c)
        outs, new_states = [], []
        for d, s in enumerate(states):
            v_new = u_s[d, pl.ds(p0, c), :] - _bdot(w_s[d, pl.ds(p0, c), :], s)
            outs.append(_bdot(qd_s[d, pl.ds(p0, c), :], s) + _bdot(qk_s[d, pl.ds(p0, c), :], v_new))
            new_states.append(s * dec_s[d, pl.ds(p, 1), :] + _bdot_tn(kd_s[d, pl.ds(p0, c), :], v_new))
        of_ref[0, pl.ds(pl.multiple_of(order(0, i, n) * c, c), c), :] = outs[0]
        ob_ref[pl.ds(pl.multiple_of(order(1, i, n) * c, c), c), :] = outs[1]
        return tuple(new_states)

    for d in range(2):
        prep(d, qc_ref, kc_ref, vc_ref, gc_ref, gtc_ref, [order(d, i, ncc) for i in range(ncc)], list(range(ncc)))

        def prep_lat(gi, carry, d=d):
            base = gi * GDN_PREP_GROUP
            prep(d, ql_ref, kl_ref, vl_ref, gl_ref, gtl_ref,
                 [order(d, base + j, ncl) for j in range(GDN_PREP_GROUP)],
                 [ncc + base + j for j in range(GDN_PREP_GROUP)])
            return carry

        lax.fori_loop(0, ncl // GDN_PREP_GROUP, prep_lat, 0)

    states = (jnp.zeros((GDN_DK, GDN_DV), F32), jnp.zeros((GDN_DK, GDN_DV), F32))
    for i in range(ncc):
        states = advance(oc_ref, obc_s, i, ncc, i, states)
    lax.fori_loop(0, ncl, lambda i, st: advance(ol_ref, obl_s, i, ncl, ncc + i, st), states)
    ol_ref[0] = ol_ref[0] + obl_s[...]
    oc_ref[0] = oc_ref[0] + obc_s[...]


SSD_R = SSD_HEADS // SSD_GROUPS
SSD_PAIRS = SSD_R * SSD_HEADDIM // 128
GATE_COL_DT = 4 * GDN_HEADS
GATE_COL_DA = GATE_COL_DT + 2 * SSD_HEADS


def _ssd_body(xl_ref, bl_ref, cl_ref, gl_ref, gtl_ref, xc_ref, bc_ref, cc_ref, gc_ref, gtc_ref,
              yl_ref, yc_ref, st_ref):
    g = pl.program_id(1)
    d = pl.program_id(2)
    fwd = d == 0
    c = SCAN_C
    ncc = xc_ref.shape[1] // c
    ncl = xl_ref.shape[1] // c
    _, _, incl, incl_t, _ = _scan_masks(fwd)
    lane = lax.broadcasted_iota(jnp.int32, (c, 128), 1)
    low = lane < SSD_HEADDIM

    @pl.when(fwd)
    def _():
        yl_ref[...] = jnp.zeros(yl_ref.shape, F32)
        yc_ref[...] = jnp.zeros(yc_ref.shape, F32)

    st_ref[...] = jnp.zeros(st_ref.shape, F32)

    def chunk(x_ref, b_ref, c_ref, g_ref, gt_ref, y_ref, ci):
        t0 = pl.multiple_of(ci * c, c)
        bm = b_ref[0, pl.ds(t0, c), :].astype(BF16)
        cm = c_ref[0, pl.ds(t0, c), :].astype(BF16)
        gch = g_ref[0, pl.ds(t0, c), :]
        cb = _dot_nt(cm, bm)
        for pr in range(SSD_PAIRS):
            per_head = []
            for s in range(2):
                hh = g * SSD_R + 2 * pr + s
                col_dt = GATE_COL_DT + d * SSD_HEADS + hh
                col_da = GATE_COL_DA + d * SSD_HEADS + hh
                dt_col = jnp.sum(jnp.where(lane == col_dt, gch, 0.0), axis=1, keepdims=True)
                da_col = jnp.sum(jnp.where(lane == col_da, gch, 0.0), axis=1, keepdims=True)
                da_row = gt_ref[0, ci, pl.ds(col_da, 1), :]
                acs_col, acs_row = _cumsum_col_row(da_col, da_row, incl, incl_t)
                a_tot = jnp.sum(da_col, axis=0, keepdims=True)
                lmat = jnp.where(incl, jnp.exp(jnp.where(incl, acs_col - acs_row, 0.0)), 0.0)
                per_head.append((dt_col, acs_col, a_tot, (cb * lmat).astype(BF16)))
            pick = lambda k: jnp.where(low, per_head[0][k], per_head[1][k])
            x = x_ref[0, pl.ds(t0, c), pr * 128:(pr + 1) * 128].astype(F32)
            xdt = x * pick(0)
            acs = pick(1)
            a_tot = pick(2)
            xdt_b = xdt.astype(BF16)
            y_diag = jnp.where(low, _dot(per_head[0][3], xdt_b), _dot(per_head[1][3], xdt_b))
            st = st_ref[pr]
            y_off = _dot(cm, st.astype(BF16)) * jnp.exp(acs)
            y_ref[0, pl.ds(t0, c), pr * 128:(pr + 1) * 128] += y_diag + y_off
            st_ref[pr] = st * jnp.exp(a_tot) + _dot_tn(bm, (xdt * jnp.exp(a_tot - acs)).astype(BF16))

    def order(i, n):
        return jnp.where(fwd, i, n - 1 - i)

    for i in range(ncc):
        chunk(xc_ref, bc_ref, cc_ref, gc_ref, gtc_ref, yc_ref, order(i, ncc))

    def lat(i, carry):
        chunk(xl_ref, bl_ref, cl_ref, gl_ref, gtl_ref, yl_ref, order(i, ncl))
        return carry

    lax.fori_loop(0, ncl, lat, 0)


def _ssd_scan(xbc_l, g_l, gt_l, xbc_c, g_c, gt_c, col0=0):
    b, l, _ = xbc_l.shape
    lc = xbc_c.shape[1]
    gw = SSD_R * SSD_HEADDIM
    x0 = col0 // gw
    nxb = (col0 + SSD_INNER) // 128

    def stream(n):
        return [pl.BlockSpec((1, n, gw), lambda bi, gi, di: (bi, 0, x0 + gi)),
                pl.BlockSpec((1, n, 128), lambda bi, gi, di: (bi, 0, nxb + gi)),
                pl.BlockSpec((1, n, 128), lambda bi, gi, di: (bi, 0, nxb + SSD_GROUPS + gi)),
                pl.BlockSpec((1, n, 128), lambda bi, gi, di: (bi, 0, 0)),
                pl.BlockSpec((1, n // SCAN_C, 128, SCAN_C), lambda bi, gi, di: (bi, 0, 0, 0))]

    return pl.pallas_call(
        _ssd_body,
        grid=(b, SSD_GROUPS, 2),
        in_specs=stream(l) + stream(lc),
        out_specs=[pl.BlockSpec((1, l, gw), lambda bi, gi, di: (bi, 0, gi)),
                   pl.BlockSpec((1, lc, gw), lambda bi, gi, di: (bi, 0, gi))],
        out_shape=[jax.ShapeDtypeStruct((b, l, SSD_INNER), F32),
                   jax.ShapeDtypeStruct((b, lc, SSD_INNER), F32)],
        scratch_shapes=[pltpu.VMEM((SSD_PAIRS, SSD_STATE, 128), F32)],
        compiler_params=pltpu.CompilerParams(
            dimension_semantics=("arbitrary", "arbitrary", "arbitrary"), vmem_limit_bytes=V7X_VMEM_LIMIT_BYTES),
        name="ssd_scan",
    )(xbc_l, xbc_l, xbc_l, g_l, gt_l, xbc_c, xbc_c, xbc_c, g_c, gt_c)


def _chunk_rows(gt):
    b, w, l = gt.shape
    return gt.reshape(b, w, l // SCAN_C, SCAN_C).transpose(0, 2, 1, 3)


def _gdn_scan(qkv_l, g_l, gt_l, qkv_c, g_c, gt_c):
    b, l, _ = qkv_l.shape
    lc = qkv_c.shape[1]
    hd = GDN_HEADS

    def stream(n):
        tok = lambda off: pl.BlockSpec((1, n, 128), lambda bi, hi: (bi, 0, off + hi))
        return [tok(0), tok(hd), tok(2 * hd),
                pl.BlockSpec((1, n, 128), lambda bi, hi: (bi, 0, 0)),
                pl.BlockSpec((1, n // SCAN_C, 128, SCAN_C), lambda bi, hi: (bi, 0, 0, 0))]

    nt = l + lc
    return pl.pallas_call(
        _gdn_body,
        grid=(b, hd),
        in_specs=stream(l) + stream(lc),
        out_specs=[pl.BlockSpec((1, l, 128), lambda bi, hi: (bi, 0, hi)),
                   pl.BlockSpec((1, lc, 128), lambda bi, hi: (bi, 0, hi))],
        out_shape=[jax.ShapeDtypeStruct((b, l, hd * GDN_DV), F32),
                   jax.ShapeDtypeStruct((b, lc, hd * GDN_DV), F32)],
        scratch_shapes=[pltpu.VMEM((2, nt, 128), F32) for _ in range(5)]
        + [pltpu.VMEM((2, nt // SCAN_C, 128), F32), pltpu.VMEM((l, 128), F32), pltpu.VMEM((lc, 128), F32)],
        compiler_params=pltpu.CompilerParams(
            dimension_semantics=("arbitrary", "arbitrary"), vmem_limit_bytes=V7X_VMEM_LIMIT_BYTES),
        name="gdn_scan",
    )(qkv_l, qkv_l, qkv_l, g_l, gt_l, qkv_c, qkv_c, qkv_c, g_c, gt_c)


def _rmsnorm(t, w):
    tf = t.astype(F32)
    tf = tf * lax.rsqrt(jnp.mean(tf * tf, axis=-1, keepdims=True) + RMS_EPS)
    return tf * w.astype(F32)


def _l2norm(t):
    tf = t.astype(F32)
    return tf * lax.rsqrt(jnp.sum(tf * tf, axis=-1, keepdims=True) + RMS_EPS)


def _modulate(t, shift, scale):
    return t * (1.0 + scale) + shift


def _grid_dwconv(t, w, n_rows, n_cols):
    b, l, ch = t.shape
    img = t.reshape(b, n_rows, n_cols, ch)
    out = lax.conv_general_dilated(img, w[:, :, None, :], window_strides=(1, 1), padding='SAME',
                                   dimension_numbers=('NHWC', 'HWIO', 'NHWC'), feature_group_count=ch)
    return out.reshape(b, l, ch)


def _gated_delta_chunked(q, k, v, g, beta, s0):
    b, l, h, dk = q.shape
    c = GDN_CHUNK
    n = l // c

    def chunks(t):
        return t.astype(F32).reshape(b, n, c, h, -1).transpose(1, 0, 3, 2, 4)

    qc = chunks(q) * (dk ** -0.5)
    kc, vc = chunks(k), chunks(v)
    gc = jnp.cumsum(g.astype(F32).reshape(b, n, c, h).transpose(1, 0, 3, 2), axis=-1)
    bc = beta.astype(F32).reshape(b, n, c, h).transpose(1, 0, 3, 2)
    causal = jnp.tril(jnp.ones((c, c), bool))
    strict = jnp.tril(jnp.ones((c, c), bool), -1)
    seg = gc[..., :, None] - gc[..., None, :]
    decay = jnp.where(causal, jnp.exp(jnp.where(causal, seg, 0.0)), 0.0)
    kb = kc * bc[..., None]
    m = jnp.where(strict, jnp.einsum('nbhid,nbhjd->nbhij', kb, kc) * decay, 0.0)
    tri = jnp.eye(c, dtype=F32) + m
    u = lax.linalg.triangular_solve(tri, vc * bc[..., None], left_side=True, lower=True, unit_diagonal=True)
    w = lax.linalg.triangular_solve(tri, kb * jnp.exp(gc)[..., None], left_side=True, lower=True, unit_diagonal=True)
    qk = jnp.einsum('nbhid,nbhjd->nbhij', qc, kc) * decay
    q_dec = qc * jnp.exp(gc)[..., None]
    k_dec = kc * jnp.exp(gc[..., -1:] - gc)[..., None]
    g_last = jnp.exp(gc[..., -1])

    def step(s, inp):
        u_i, w_i, qk_i, qd_i, kd_i, gl_i = inp
        v_new = u_i - w_i @ s
        o_i = qd_i @ s + qk_i @ v_new
        s = s * gl_i[..., None, None] + jnp.swapaxes(kd_i, -1, -2) @ v_new
        return s, o_i

    s_fin, o = lax.scan(step, s0.astype(F32), (u, w, qk, q_dec, k_dec, g_last))
    o = o.transpose(1, 0, 3, 2, 4).reshape(b, l, h, -1)
    return o, s_fin


def _ssd_chunked(x, dt, da, bm, cm, s0):
    b, l, h, p = x.shape
    g, nst = bm.shape[2], bm.shape[3]
    r = h // g
    c = SSD_CHUNK
    n = l // c
    xc = (x.astype(F32) * dt.astype(F32)[..., None]).reshape(b, n, c, g, r, p)
    acs = jnp.cumsum(da.astype(F32).reshape(b, n, c, g, r), axis=2)
    bc = bm.astype(F32).reshape(b, n, c, g, nst)
    cc = cm.astype(F32).reshape(b, n, c, g, nst)
    causal = jnp.tril(jnp.ones((c, c), bool))[:, :, None, None]
    seg = acs[:, :, :, None] - acs[:, :, None, :]
    lmat = jnp.where(causal, jnp.exp(jnp.where(causal, seg, 0.0)), 0.0)
    scores = jnp.einsum('bnlgd,bnsgd->bnlsg', cc, bc)[..., None] * lmat
    y_diag = jnp.einsum('bnlsgr,bnsgrp->bnlgrp', scores, xc)
    decay_out = jnp.exp(acs[:, :, -1:] - acs)
    chunk_states = jnp.einsum('bnsgd,bnsgrp->bngrpd', bc, xc * decay_out[..., None])
    states = jnp.concatenate([s0.astype(F32).reshape(b, 1, g, r, p, nst), chunk_states], axis=1)
    cum = jnp.cumsum(jnp.pad(acs[:, :, -1], ((0, 0), (1, 0), (0, 0), (0, 0))), axis=1)
    tri = jnp.tril(jnp.ones((n + 1, n + 1), bool))[:, :, None, None]
    segc = cum[:, :, None] - cum[:, None, :]
    dec = jnp.where(tri, jnp.exp(jnp.where(tri, segc, 0.0)), 0.0)
    new = jnp.einsum('bzcgr,bcgrpd->bzgrpd', dec, states)
    y_off = jnp.einsum('bnlgd,bngrpd->bnlgrp', cc, new[:, :-1]) * jnp.exp(acs)[..., None]
    y = (y_diag + y_off).reshape(b, l, h, p)
    return y, new[:, -1].reshape(b, h, p, nst)


def _bidirectional(scan_fn, ctx_dirs, lat_dirs, s0):
    flip = lambda t: jnp.flip(t, axis=1)
    y_ctx, y_lat = 0.0, 0.0
    for d in range(2):
        rev = d == 1
        cargs = tuple(flip(t) for t in ctx_dirs[d]) if rev else ctx_dirs[d]
        largs = tuple(flip(t) for t in lat_dirs[d]) if rev else lat_dirs[d]
        o_c, s_c = scan_fn(*cargs, s0)
        o_l, _ = scan_fn(*largs, s_c)
        y_ctx = y_ctx + (flip(o_c) if rev else o_c)
        y_lat = y_lat + (flip(o_l) if rev else o_l)
    return y_ctx, y_lat


def _hybrid_mixer(h_ctx, h_lat, rows, w_in, gdn_conv_w, gdn_a_log, gdn_dt_bias, gdn_norm_w,
                  ssd_conv_w, ssd_conv_b, ssd_a_log, ssd_dt_bias, ssd_d, ssd_norm_w, w_out):
    splits = np.cumsum(IN0_SIZES)[:-1].tolist()

    def features(h, n_rows, n_cols):
        b, l, _ = h.shape
        conv_a, conv_b, z_a, z_b, a_raw, b_raw, dt_raw = jnp.split(h @ w_in, splits, axis=-1)
        conv_a = jax.nn.silu(_grid_dwconv(conv_a, gdn_conv_w, n_rows, n_cols))
        conv_b = jax.nn.silu(_grid_dwconv(conv_b, ssd_conv_w, n_rows, n_cols) + ssd_conv_b)
        q, k, v = jnp.split(conv_a, [GDN_HEADS * GDN_DK, 2 * GDN_HEADS * GDN_DK], axis=-1)
        q = _l2norm(q.reshape(b, l, GDN_HEADS, GDN_DK))
        k = _l2norm(k.reshape(b, l, GDN_HEADS, GDN_DK))
        v = v.reshape(b, l, GDN_HEADS, GDN_DV)
        g = -jnp.exp(gdn_a_log) * jax.nn.softplus(a_raw.reshape(b, l, 2, GDN_HEADS) + gdn_dt_bias)
        beta = jax.nn.sigmoid(b_raw.reshape(b, l, 2, GDN_HEADS))
        xs, bm, cm = jnp.split(conv_b, [SSD_INNER, SSD_INNER + SSD_GROUPS * SSD_STATE], axis=-1)
        xs = xs.reshape(b, l, SSD_HEADS, SSD_HEADDIM)
        bm = bm.reshape(b, l, SSD_GROUPS, SSD_STATE)
        cm = cm.reshape(b, l, SSD_GROUPS, SSD_STATE)
        dt = jax.nn.softplus(dt_raw.reshape(b, l, 2, SSD_HEADS) + ssd_dt_bias)
        da = -jnp.exp(ssd_a_log) * dt
        gdn_dirs = tuple((q, k, v, g[:, :, d], beta[:, :, d]) for d in range(2))
        ssd_dirs = tuple((xs, dt[:, :, d], da[:, :, d], bm, cm) for d in range(2))
        return gdn_dirs, ssd_dirs, xs, z_a, z_b

    b = h_lat.shape[0]
    gdn_c, ssd_c, xs_c, za_c, zb_c = features(h_ctx, 1, h_ctx.shape[1])
    gdn_l, ssd_l, xs_l, za_l, zb_l = features(h_lat, rows, GRID_W)
    o_c, o_l = _bidirectional(_gated_delta_chunked, gdn_c, gdn_l, jnp.zeros((b, GDN_HEADS, GDN_DK, GDN_DV), F32))
    y_c, y_l = _bidirectional(_ssd_chunked, ssd_c, ssd_l, jnp.zeros((b, SSD_HEADS, SSD_HEADDIM, SSD_STATE), F32))

    def merge(o, y, xs, z_a, z_b):
        bb, l = o.shape[0], o.shape[1]
        o = _rmsnorm(o, gdn_norm_w) * jax.nn.silu(z_a).reshape(bb, l, GDN_HEADS, GDN_DV)
        y = (y + ssd_d[:, None] * xs).reshape(bb, l, SSD_INNER) * jax.nn.silu(z_b)
        y = _rmsnorm(y.reshape(bb, l, SSD_GROUPS, -1), ssd_norm_w.reshape(SSD_GROUPS, -1))
        mixed = jnp.concatenate([o.reshape(bb, l, -1), y.reshape(bb, l, -1)], axis=-1)
        return mixed @ w_out

    return merge(o_c, y_c, xs_c, za_c, zb_c), merge(o_l, y_l, xs_l, za_l, zb_l)


def _s5_discretize(a_re, a_im, log_step, b_re, b_im):
    step = jnp.exp(log_step)[:, None]
    mag = jnp.exp(a_re * step)
    lam_re, lam_im = mag * jnp.cos(a_im * step), mag * jnp.sin(a_im * step)
    den = a_re * a_re + a_im * a_im
    f_re = ((lam_re - 1.0) * a_re + lam_im * a_im) / den
    f_im = (lam_im * a_re - (lam_re - 1.0) * a_im) / den
    bb_re = f_re[..., None] * b_re - f_im[..., None] * b_im
    bb_im = f_re[..., None] * b_im + f_im[..., None] * b_re
    return lam_re, lam_im, bb_re, bb_im


def _route_and_ffn(t, router_w, router_b, wg, bg, wu, bu, wd, bd):
    shp = t.shape
    t2 = t.reshape(-1, shp[-1])
    logits = t2 @ router_w + router_b
    top_val, top_idx = lax.top_k(logits, TOP_K)
    probs = jax.nn.softmax(top_val, axis=-1)
    f = _moe_ffn(t2.astype(BF16), top_idx.astype(jnp.int32), probs, wg, bg, wu, bu, wd, bd)
    return f.reshape(shp)


def kernel(x, c, ctx, c_ctx, ada_w, ada_b, mix_norm_pre, mix_norm_post, ffn_norm_pre, ffn_norm_post, router_w, router_b, moe_w_gate, moe_b_gate, moe_w_up, moe_b_up, moe_w_down, moe_b_down, hy_w_in, gdn_conv_w, gdn_a_log, gdn_dt_bias, gdn_norm_w, ssd_conv_w, ssd_conv_b, ssd_a_log, ssd_dt_bias, ssd_d, ssd_norm_w, hy_w_out, s5_w_in, s5_a_re, s5_a_im, s5_log_step, s5_b_re, s5_b_im, s5_c_re, s5_c_im, s5_d, s5_w_glu_a, s5_w_glu_b):
    depth = ada_w.shape[0]
    rows = x.shape[1] // GRID_W
    h_lat, h_ctx = x, ctx
    cond_lat = jax.nn.silu(c)
    cond_ctx = jax.nn.silu(c_ctx)
    for i in range(depth):
        j = i // 2
        need_ctx = i < depth - 1
        mods_lat = (cond_lat @ ada_w[i] + ada_b[i])[:, None, :]
        mods_ctx = (cond_ctx @ ada_w[i] + ada_b[i])[None, None, :]
        m_lat = jnp.split(mods_lat, 6, axis=-1)
        m_ctx = jnp.split(mods_ctx[0, 0], 6, axis=-1)
        ffn = (moe_w_gate[i], moe_b_gate[i], moe_w_up[i], moe_b_up[i], moe_w_down[i], moe_b_down[i],
               ffn_norm_post[i])
        rw_pad, rb_pad = _pad_router(router_w[i], router_b[i])
        bsz, l, d = h_lat.shape
        lc = h_ctx.shape[1]
        tl_lat, tl_ctx = 512, lc
        if i % 2 == 1:
            assert not need_ctx
            nbg = bsz // S5_BATCH_SUB
            w_in = s5_w_in[j].astype(BF16)
            u_lat = _s5_inproj(h_lat, mods_lat, mix_norm_pre[i], w_in, tl_lat)
            u_ctx = _s5_inproj(h_ctx, mods_ctx, mix_norm_pre[i], w_in, tl_ctx)
            bblk, cblk, lam = _s5_block_params(s5_a_re[j], s5_a_im[j], s5_log_step[j], s5_b_re[j], s5_b_im[j],
                                               s5_c_re[j], s5_c_im[j])
            y = _s5_scan(u_ctx.reshape(nbg, lc, S5_BATCH_SUB, d), u_lat.reshape(nbg, l, S5_BATCH_SUB, d),
                         bblk, cblk, lam)
            h1, t_lat, idx, probs = _s5_glu_postmix(
                y, u_lat, h_lat, mods_lat, s5_d[j], s5_w_glu_a[j].astype(BF16), s5_w_glu_b[j].astype(BF16),
                mix_norm_post[i], ffn_norm_pre[i], rw_pad, rb_pad, tl_lat)
            (h_lat,) = _moe_layer([(t_lat, idx, probs, h1, mods_lat, tl_lat)], *ffn)
            continue
        w_all, conv_w, conv_b, gate_params = _in0_params(hy_w_in[j], gdn_conv_w[j], ssd_conv_w[j], ssd_conv_b[j],
                                                         gdn_a_log[j], gdn_dt_bias[j], ssd_a_log[j], ssd_dt_bias[j])
        conv_l, z_l, g_l, gt_l = _in0_features(h_lat, mods_lat, mix_norm_pre[i], w_all, conv_w, conv_b, gate_params,
                                               rows, GRID_W)
        conv_c, z_c, g_c, gt_c = _in0_features(h_ctx, mods_ctx, mix_norm_pre[i], w_all, conv_w, conv_b, gate_params,
                                               1, lc)
        o_l, o_c = _gdn_scan(conv_l, g_l, gt_l, conv_c, g_c, gt_c)
        y_l, y_c = _ssd_scan(conv_l, g_l, gt_l, conv_c, g_c, gt_c, col0=GDN_CONV_CH)
        w_out = hy_w_out[j].astype(BF16)
        merge = lambda o, y, conv, z, h, mods, tl: _merge0_postmix(
            o, y, conv, z, h, mods, gdn_norm_w[j], ssd_d[j], ssd_norm_w[j], w_out,
            mix_norm_post[i], ffn_norm_pre[i], rw_pad, rb_pad, tl)
        streams = [merge(o_l, y_l, conv_l, z_l, h_lat, mods_lat, tl_lat) + (mods_lat, tl_lat)]
        if need_ctx:
            streams.append(merge(o_c, y_c, conv_c, z_c, h_ctx, mods_ctx, tl_ctx) + (mods_ctx, tl_ctx))
        outs = _moe_layer([(t, idx, probs, h1, mods, tl) for h1, t, idx, probs, mods, tl in streams], *ffn)
        h_lat = outs[0]
        if need_ctx:
            h_ctx = outs[1]
    return h_lat
```

```python
import functools
import math

import jax
import jax.numpy as jnp
import numpy as np
from jax import lax
from jax.experimental import pallas as pl
from jax.experimental.pallas import tpu as pltpu

F32 = jnp.float32
BF16 = jnp.bfloat16

D_MODEL = 1024
GRID_W = 64
RMS_EPS = 1e-6

GDN_HEADS = 4
GDN_DK = 128
GDN_DV = 128
GDN_CHUNK = 64
SSD_HEADS = 8
SSD_HEADDIM = 64
SSD_GROUPS = 2
SSD_STATE = 128
SSD_CHUNK = 128
S5_GROUP = 16
S5_GROUPS = D_MODEL // S5_GROUP
S5_STATE = 64
N_EXPERTS = 32
TOP_K = 4
SWIGLU_LIMIT = 7.0
SWIGLU_ALPHA = 1.702

GDN_V_WIDTH = GDN_HEADS * GDN_DV
SSD_INNER = SSD_HEADS * SSD_HEADDIM
GDN_CONV_CH = 2 * GDN_HEADS * GDN_DK + GDN_V_WIDTH
SSD_CONV_CH = SSD_INNER + 2 * SSD_GROUPS * SSD_STATE
IN0_SIZES = (GDN_CONV_CH, SSD_CONV_CH, GDN_V_WIDTH, SSD_INNER, 2 * GDN_HEADS, 2 * GDN_HEADS, 2 * SSD_HEADS)

V7X_VMEM_LIMIT_BYTES = 56 * 1024 * 1024
MOE_TILE_M = 512
MOE_TILE_F = 512
MOE_BATCH_SLICES = 2


def _moe_ffn_body(tile_e_ref, tile_ok_ref, x_ref, wg_ref, bg_ref, wu_ref, bu_ref, wd_ref, bd_ref, o_ref,
                  wg_s, wu_s, wd_s):
    i = pl.program_id(0)
    n_f = wg_ref.shape[3] // MOE_TILE_F

    @pl.when((i == 0) | (tile_e_ref[i] != tile_e_ref[jnp.maximum(i - 1, 0)]))
    def _():
        for c in range(n_f):
            cs = slice(c * MOE_TILE_F, (c + 1) * MOE_TILE_F)
            wg_s[:, cs] = wg_ref[0, 0, :, cs].astype(BF16)
            wu_s[:, cs] = wu_ref[0, 0, :, cs].astype(BF16)
            wd_s[cs, :] = wd_ref[0, 0, cs, :].astype(BF16)

    @pl.when(tile_ok_ref[i] > 0)
    def _():
        x = x_ref[...]
        acc = jnp.zeros(o_ref.shape, F32)
        for c in range(n_f):
            cs = slice(c * MOE_TILE_F, (c + 1) * MOE_TILE_F)
            gl = jnp.dot(x, wg_s[:, cs], preferred_element_type=F32) + bg_ref[0, 0, :, cs]
            lin = jnp.dot(x, wu_s[:, cs], preferred_element_type=F32) + bu_ref[0, 0, :, cs]
            gl = jnp.minimum(gl, SWIGLU_LIMIT)
            lin = jnp.clip(lin, -SWIGLU_LIMIT, SWIGLU_LIMIT)
            act = gl * jax.nn.sigmoid(SWIGLU_ALPHA * gl) * (lin + 1.0)
            acc = acc + jnp.dot(act.astype(BF16), wd_s[cs, :], preferred_element_type=F32)
        o_ref[...] = (acc + bd_ref[0, 0]).astype(o_ref.dtype)

    @pl.when(tile_ok_ref[i] == 0)
    def _():
        o_ref[...] = jnp.zeros(o_ref.shape, o_ref.dtype)


def _moe_grouped_ffn(xs, tile_e, tile_ok, layer, wg, bg, wu, bu, wd, bd):
    p, d = xs.shape
    nl, e, _, f = wg.shape
    n_tiles = p // MOE_TILE_M
    grid_spec = pltpu.PrefetchScalarGridSpec(
        num_scalar_prefetch=2,
        grid=(n_tiles,),
        in_specs=[
            pl.BlockSpec((MOE_TILE_M, d), lambda i, te, ok: (i, 0)),
            pl.BlockSpec((1, 1, d, f), lambda i, te, ok: (layer, te[i], 0, 0)),
            pl.BlockSpec((1, 1, 1, f), lambda i, te, ok: (layer, te[i], 0, 0)),
            pl.BlockSpec((1, 1, d, f), lambda i, te, ok: (layer, te[i], 0, 0)),
            pl.BlockSpec((1, 1, 1, f), lambda i, te, ok: (layer, te[i], 0, 0)),
            pl.BlockSpec((1, 1, f, d), lambda i, te, ok: (layer, te[i], 0, 0)),
            pl.BlockSpec((1, 1, 1, d), lambda i, te, ok: (layer, te[i], 0, 0)),
        ],
        out_specs=pl.BlockSpec((MOE_TILE_M, d), lambda i, te, ok: (i, 0)),
        scratch_shapes=[pltpu.VMEM((d, f), BF16), pltpu.VMEM((d, f), BF16), pltpu.VMEM((f, d), BF16)],
    )
    return pl.pallas_call(
        _moe_ffn_body,
        grid_spec=grid_spec,
        out_shape=jax.ShapeDtypeStruct((p, d), BF16),
        compiler_params=pltpu.CompilerParams(
            dimension_semantics=("arbitrary",), vmem_limit_bytes=V7X_VMEM_LIMIT_BYTES),
        name="moe_grouped_ffn",
    )(tile_e, tile_ok, xs, wg, bg.reshape(nl, e, 1, f), wu, bu.reshape(nl, e, 1, f), wd, bd.reshape(nl, e, 1, d))


def _moe_combine_body(y0_ref, y1_ref, y2_ref, y3_ref, p_ref, h_ref, gate_ref, nw_ref, *rest):
    o_ref = rest[-1]
    p = p_ref[0]
    acc = jnp.zeros(o_ref.shape[1:], F32)
    for k, y_ref in enumerate((y0_ref, y1_ref, y2_ref, y3_ref)):
        acc = acc + p[:, k:k + 1] * y_ref[0, 0].astype(F32)
    o_ref[0] = h_ref[0] + gate_ref[0] * _k_rms(acc, nw_ref[...])


def _moe_combine(yg, probs, h, mods, norm_w, tl, b0, prev):
    b, l, d = h.shape
    nb = yg.shape[1]
    bsel = (lambda bi: bi + b0) if mods.shape[0] > 1 else (lambda bi: 0)
    yk = lambda k: pl.BlockSpec((1, 1, tl, d), lambda bi, ti: (k, bi, ti, 0))
    in_specs = [yk(0), yk(1), yk(2), yk(3),
                pl.BlockSpec((1, tl, 128), lambda bi, ti: (bi + b0, ti, 0)),
                pl.BlockSpec((1, tl, d), lambda bi, ti: (bi + b0, ti, 0)),
                pl.BlockSpec((1, 1, d), lambda bi, ti: (bsel(bi), 0, 5)),
                pl.BlockSpec((1, d), lambda bi, ti: (0, 0))]
    args = [yg, yg, yg, yg, probs, h, mods, norm_w.reshape(1, -1)]
    aliases = {}
    if prev is not None:
        in_specs.append(pl.BlockSpec(memory_space=pl.ANY))
        args.append(prev)
        aliases = {len(args) - 1: 0}
    return pl.pallas_call(
        _moe_combine_body,
        grid=(nb, l // tl),
        in_specs=in_specs,
        out_specs=pl.BlockSpec((1, tl, d), lambda bi, ti: (bi + b0, ti, 0)),
        out_shape=jax.ShapeDtypeStruct((b, l, d), F32),
        input_output_aliases=aliases,
        compiler_params=pltpu.CompilerParams(
            dimension_semantics=("arbitrary", "arbitrary"), vmem_limit_bytes=V7X_VMEM_LIMIT_BYTES),
        name="moe_combine",
    )(*args)


def _moe_route(top_idx):
    t = top_idx.shape[0]
    a = t * TOP_K
    tm = MOE_TILE_M
    eid = top_idx.reshape(a)
    order = jnp.argsort(eid, stable=True).astype(jnp.int32)
    inv = jnp.argsort(order).astype(jnp.int32)
    counts = jnp.sum(jax.nn.one_hot(eid, N_EXPERTS, dtype=jnp.int32), axis=0)
    off = jnp.cumsum(counts) - counts
    pcounts = ((counts + tm - 1) // tm) * tm
    pend = jnp.cumsum(pcounts)
    poff = pend - pcounts
    n_tiles = a // tm + N_EXPERTS
    tile_start = jnp.arange(n_tiles, dtype=jnp.int32) * tm
    n_done = jnp.sum((tile_start[:, None] >= pend[None, :]).astype(jnp.int32), axis=1)
    tile_e = jnp.minimum(n_done, N_EXPERTS - 1)
    tile_ok = (tile_start < pend[-1]).astype(jnp.int32)
    ppos = jnp.arange(n_tiles * tm, dtype=jnp.int32)
    pe = jnp.repeat(tile_e, tm)
    r = ppos - poff[pe]
    src_rank = jnp.clip(off[pe] + jnp.minimum(r, counts[pe] - 1), 0, a - 1)
    src_tok = order[src_rank] // TOP_K
    pos = poff[eid] + (inv - off[eid])
    return src_tok, pos, tile_e, tile_ok


def _moe_layer(streams, layer, wg, bg, wu, bu, wd, bd, norm_w):
    d = streams[0][0].shape[-1]
    outs = [None] * len(streams)
    for mb in range(MOE_BATCH_SLICES):
        cuts = [(mb * (s[0].shape[0] // MOE_BATCH_SLICES), s[0].shape[0] // MOE_BATCH_SLICES) for s in streams]
        t_all = jnp.concatenate([s[0][b0:b0 + nb].reshape(-1, d) for s, (b0, nb) in zip(streams, cuts)], axis=0)
        idx_all = jnp.concatenate([s[1][b0:b0 + nb].reshape(-1, 128)[:, :TOP_K]
                                   for s, (b0, nb) in zip(streams, cuts)], axis=0)
        src_tok, pos, tile_e, tile_ok = _moe_route(idx_all)
        xs = t_all.at[src_tok].get(mode="promise_in_bounds")
        ys = _moe_grouped_ffn(xs, tile_e, tile_ok, layer, wg, bg, wu, bu, wd, bd)
        pos_k = pos.reshape(-1, TOP_K).T
        start = 0
        for si, ((t, _, probs, h, mods, tl), (b0, nb)) in enumerate(zip(streams, cuts)):
            l = t.shape[1]
            n = nb * l
            yg = ys.at[pos_k[:, start:start + n].reshape(-1)].get(mode="promise_in_bounds")
            outs[si] = _moe_combine(yg.reshape(TOP_K, nb, l, d), probs, h, mods, norm_w, tl, b0, outs[si])
            start += n
    return outs


def _k_rms(t, w):
    return t * lax.rsqrt(jnp.mean(t * t, axis=-1, keepdims=True) + RMS_EPS) * w


def _k_post_mix(y, h, gate, post_w, pre_w, shift, scale, rw, rb):
    h1 = h + gate * _k_rms(y, post_w)
    t = _k_rms(h1, pre_w) * (1.0 + scale) + shift
    logits = jnp.dot(t, rw, preferred_element_type=F32) + rb
    lane = lax.broadcasted_iota(jnp.int32, logits.shape, 1)
    idx_out = jnp.zeros(logits.shape, jnp.int32)
    val_out = jnp.zeros(logits.shape, F32)
    work = logits
    m0 = None
    for k in range(TOP_K):
        m = jnp.max(work, axis=-1, keepdims=True)
        sel = jnp.min(jnp.where(work == m, lane, 128), axis=-1, keepdims=True)
        if k == 0:
            m0 = m
        idx_out = jnp.where(lane == k, sel, idx_out)
        val_out = jnp.where(lane == k, jnp.exp(m - m0), val_out)
        work = jnp.where(lane == sel, -jnp.inf, work)
    probs = val_out / jnp.sum(val_out, axis=-1, keepdims=True)
    return h1, t, idx_out, probs


S5_BATCH_SUB = 8
S5_CH_BLOCK = 128
S5_ST_BLOCK = (S5_CH_BLOCK // S5_GROUP) * S5_STATE
S5_TL = 256


def _s5_inproj_body(h_ref, sh_ref, sc_ref, nw_ref, w_ref, u_ref):
    a = _k_rms(h_ref[0], nw_ref[...]) * (1.0 + sc_ref[0]) + sh_ref[0]
    u_ref[0] = jnp.dot(a.astype(BF16), w_ref[...], preferred_element_type=F32)


def _s5_inproj(h, mods, norm_w, w_bf16, tl):
    b, l, d = h.shape
    nb = mods.shape[0]
    bsel = (lambda bi: bi) if nb > 1 else (lambda bi: 0)
    return pl.pallas_call(
        _s5_inproj_body,
        grid=(b, l // tl),
        in_specs=[
            pl.BlockSpec((1, tl, d), lambda bi, ti: (bi, ti, 0)),
            pl.BlockSpec((1, 1, d), lambda bi, ti: (bsel(bi), 0, 0)),
            pl.BlockSpec((1, 1, d), lambda bi, ti: (bsel(bi), 0, 1)),
            pl.BlockSpec((1, d), lambda bi, ti: (0, 0)),
            pl.BlockSpec((d, d), lambda bi, ti: (0, 0)),
        ],
        out_specs=pl.BlockSpec((1, tl, d), lambda bi, ti: (bi // S5_BATCH_SUB, ti, bi % S5_BATCH_SUB)),
        out_shape=jax.ShapeDtypeStruct((b // S5_BATCH_SUB, l, S5_BATCH_SUB * d), F32),
        compiler_params=pltpu.CompilerParams(
            dimension_semantics=("arbitrary", "arbitrary"), vmem_limit_bytes=V7X_VMEM_LIMIT_BYTES),
        name="s5_inproj",
    )(h, mods, mods, norm_w.reshape(1, d), w_bf16)


def _s5_scan_body(uc_ref, ul_ref, b_ref, c_ref, lam_ref, y_ref, buf_ref, st_ref):
    d = pl.program_id(0)
    tc = pl.program_id(3)
    tl = ul_ref.shape[1]
    rows = tl * S5_BATCH_SUB
    ns = S5_ST_BLOCK

    @pl.when(tc == 0)
    def _():
        st_ref[...] = jnp.zeros(st_ref.shape, F32)
        u2 = uc_ref[0].reshape(rows, S5_CH_BLOCK).astype(BF16)
        buf_ref[...] = jnp.dot(u2, b_ref[0, 0], preferred_element_type=F32)

    @pl.when(tc > 0)
    def _():
        u2 = ul_ref[0].reshape(rows, S5_CH_BLOCK).astype(BF16)
        buf_ref[...] = jnp.dot(u2, b_ref[0, 0], preferred_element_type=F32)

    lam = lam_ref[0, 0]
    lr = jnp.broadcast_to(lam[0:1], (S5_BATCH_SUB, ns))
    li = jnp.broadcast_to(lam[1:2], (S5_BATCH_SUB, ns))

    def step(i, carry):
        xr, xi = carry
        t = jnp.where(d == 0, i, tl - 1 - i)
        r0 = pl.multiple_of(t * S5_BATCH_SUB, S5_BATCH_SUB)
        br = buf_ref[pl.ds(r0, S5_BATCH_SUB), 0:ns]
        bi = buf_ref[pl.ds(r0, S5_BATCH_SUB), ns:2 * ns]
        nr = lr * xr - li * xi + br
        ni = lr * xi + li * xr + bi
        buf_ref[pl.ds(r0, S5_BATCH_SUB), 0:ns] = nr
        buf_ref[pl.ds(r0, S5_BATCH_SUB), ns:2 * ns] = ni
        return nr, ni

    xr, xi = lax.fori_loop(0, tl, step, (st_ref[0], st_ref[1]), unroll=8)
    st_ref[0] = xr
    st_ref[1] = xi

    @pl.when(tc > 0)
    def _():
        y = jnp.dot(buf_ref[...].astype(BF16), c_ref[0, 0], preferred_element_type=F32)
        y_ref[0, 0] = y.reshape(tl, S5_BATCH_SUB, S5_CH_BLOCK)


def _s5_scan(u_ctx, u_lat, bblk, cblk, lam):
    nbg, l_lat, _, d = u_lat.shape
    assert u_ctx.shape[1] == S5_TL and l_lat % S5_TL == 0
    n_lc = l_lat // S5_TL
    n_sb = d // S5_CH_BLOCK

    def lat_chunk(di, tc):
        j = jnp.maximum(tc - 1, 0)
        return jnp.where(di == 0, j, n_lc - 1 - j)

    return pl.pallas_call(
        _s5_scan_body,
        grid=(2, nbg, n_sb, n_lc + 1),
        in_specs=[
            pl.BlockSpec((1, S5_TL, S5_BATCH_SUB, S5_CH_BLOCK), lambda di, bg, sb, tc: (bg, 0, 0, sb)),
            pl.BlockSpec((1, S5_TL, S5_BATCH_SUB, S5_CH_BLOCK), lambda di, bg, sb, tc: (bg, lat_chunk(di, tc), 0, sb)),
            pl.BlockSpec((1, 1, S5_CH_BLOCK, 2 * S5_ST_BLOCK), lambda di, bg, sb, tc: (di, sb, 0, 0)),
            pl.BlockSpec((1, 1, 2 * S5_ST_BLOCK, S5_CH_BLOCK), lambda di, bg, sb, tc: (di, sb, 0, 0)),
            pl.BlockSpec((1, 1, 2, S5_ST_BLOCK), lambda di, bg, sb, tc: (di, sb, 0, 0)),
        ],
        out_specs=pl.BlockSpec((1, 1, S5_TL, S5_BATCH_SUB, S5_CH_BLOCK),
                               lambda di, bg, sb, tc: (di, bg, lat_chunk(di, tc), 0, sb)),
        out_shape=jax.ShapeDtypeStruct((2, nbg, l_lat, S5_BATCH_SUB, d), F32),
        scratch_shapes=[
            pltpu.VMEM((S5_TL * S5_BATCH_SUB, 2 * S5_ST_BLOCK), F32),
            pltpu.VMEM((2, S5_BATCH_SUB, S5_ST_BLOCK), F32),
        ],
        compiler_params=pltpu.CompilerParams(
            dimension_semantics=("arbitrary", "arbitrary", "arbitrary", "arbitrary"),
            vmem_limit_bytes=V7X_VMEM_LIMIT_BYTES),
        name="s5_scan",
    )(u_ctx, u_lat, bblk, cblk, lam)


def _s5_glu_body(y0_ref, y1_ref, u_ref, h_ref, gate_ref, sh_ref, sc_ref, dsk_ref, wa_ref, wb_ref,
                 postw_ref, prew_ref, rw_ref, rb_ref, h1_ref, t_ref, idx_ref, p_ref):
    y = dsk_ref[...] * u_ref[0] + y0_ref[0, 0] + y1_ref[0, 0]
    o = (0.5 * y * (1.0 + jnp.tanh(math.sqrt(2.0 / math.pi) * (y + 0.044715 * (y * y * y))))).astype(BF16)
    ga = jnp.dot(o, wa_ref[...], preferred_element_type=F32)
    gb = jnp.dot(o, wb_ref[...], preferred_element_type=F32)
    ym = ga * jax.nn.sigmoid(gb)
    h1, t, idx, probs = _k_post_mix(ym, h_ref[0], gate_ref[0], postw_ref[...], prew_ref[...],
                                    sh_ref[0], sc_ref[0], rw_ref[...], rb_ref[...])
    h1_ref[0] = h1
    t_ref[0] = t.astype(BF16)
    idx_ref[0] = idx
    p_ref[0] = probs


def _s5_glu_postmix(y, u, h, mods, d_skip, wa, wb, post_w, pre_w, rw_pad, rb_pad, tl):
    b, l, d = h.shape
    nbg = b // S5_BATCH_SUB
    y4 = y.reshape(2, nbg, l, S5_BATCH_SUB * d)
    u3 = u.reshape(nbg, l, S5_BATCH_SUB * d)
    row = lambda v: v.reshape(1, -1)
    full = lambda shape: pl.BlockSpec(shape, lambda bi, ti: (0,) * len(shape))
    mod = lambda k: pl.BlockSpec((1, 1, d), lambda bi, ti: (bi, 0, k))
    tok = lambda w: pl.BlockSpec((1, tl, w), lambda bi, ti: (bi, ti, 0))
    return pl.pallas_call(
        _s5_glu_body,
        grid=(b, l // tl),
        in_specs=[
            pl.BlockSpec((1, 1, tl, d), lambda bi, ti: (0, bi // S5_BATCH_SUB, ti, bi % S5_BATCH_SUB)),
            pl.BlockSpec((1, 1, tl, d), lambda bi, ti: (1, bi // S5_BATCH_SUB, ti, bi % S5_BATCH_SUB)),
            pl.BlockSpec((1, tl, d), lambda bi, ti: (bi // S5_BATCH_SUB, ti, bi % S5_BATCH_SUB)),
            tok(d), mod(2), mod(3), mod(4),
            full((1, d)), full((d, d)), full((d, d)), full((1, d)), full((1, d)), full((d, 128)), full((1, 128)),
        ],
        out_specs=[tok(d), tok(d), tok(128), tok(128)],
        out_shape=[
            jax.ShapeDtypeStruct((b, l, d), F32),
            jax.ShapeDtypeStruct((b, l, d), BF16),
            jax.ShapeDtypeStruct((b, l, 128), jnp.int32),
            jax.ShapeDtypeStruct((b, l, 128), F32),
        ],
        compiler_params=pltpu.CompilerParams(
            dimension_semantics=("arbitrary", "arbitrary"), vmem_limit_bytes=V7X_VMEM_LIMIT_BYTES),
        name="s5_glu_postmix",
    )(y4, y4, u3, h, mods, mods, mods, row(d_skip), wa, wb, row(post_w), row(pre_w), rw_pad, rb_pad)


def _s5_block_params(a_re, a_im, log_step, b_re, b_im, c_re, c_im):
    gpb = S5_CH_BLOCK // S5_GROUP
    eye = jnp.eye(gpb, dtype=F32)
    bblks, cblks, lams = [], [], []
    for di in range(2):
        lam_re, lam_im, bb_re, bb_im = _s5_discretize(a_re[di], a_im[di], log_step[di], b_re[di], b_im[di])
        n_sb = lam_re.shape[0] // gpb

        def bdiag_in(bb):
            t = bb.reshape(n_sb, gpb, S5_STATE, S5_GROUP)
            return jnp.einsum('sgph,gk->sghkp', t, eye).reshape(n_sb, S5_CH_BLOCK, S5_ST_BLOCK)

        def bdiag_out(cc):
            t = cc.reshape(n_sb, gpb, S5_GROUP, S5_STATE)
            return jnp.einsum('sghp,gk->sgpkh', t, eye).reshape(n_sb, S5_ST_BLOCK, S5_CH_BLOCK)

        bblks.append(jnp.concatenate([bdiag_in(bb_re), bdiag_in(bb_im)], axis=-1))
        cblks.append(jnp.concatenate([bdiag_out(c_re[di]), -bdiag_out(c_im[di])], axis=-2))
        lams.append(jnp.stack([lam_re.reshape(n_sb, S5_ST_BLOCK), lam_im.reshape(n_sb, S5_ST_BLOCK)], axis=1))
    return jnp.stack(bblks).astype(BF16), jnp.stack(cblks).astype(BF16), jnp.stack(lams)


def _pad_router(router_w, router_b):
    e = router_w.shape[1]
    rw = jnp.pad(router_w, ((0, 0), (0, 128 - e)))
    rb = jnp.pad(router_b, (0, 128 - e), constant_values=-1e30).reshape(1, 128)
    return rw, rb


CONV_CH = GDN_CONV_CH + SSD_CONV_CH
Z_CH = GDN_V_WIDTH + SSD_INNER
IN0_TILE = 256
IN0_CONV_TILES = CONV_CH // IN0_TILE
IN0_Z_TILES = Z_CH // IN0_TILE
IN0_TILES = IN0_CONV_TILES + IN0_Z_TILES + 1
CONV_ROW_CHUNK = 256


def _conv_halo(cols):
    return ((cols + 1 + 7) // 8) * 8


def _in0_body(h_ref, sh_ref, sc_ref, nw_ref, w_ref, cw_ref, cb_ref, gp_ref,
              conv_ref, z_ref, g_ref, gt_ref, a_s, p0_s, pm_s, pp_s, *, n_rows, n_cols):
    n = pl.program_id(1)
    l = h_ref.shape[1]
    halo = _conv_halo(n_cols)
    rc = min(CONV_ROW_CHUNK, l)

    @pl.when(n == 0)
    def _():
        def norm_rows(i, carry):
            r0 = pl.multiple_of(i * rc, rc)
            a = _k_rms(h_ref[0, pl.ds(r0, rc), :], nw_ref[...]) * (1.0 + sc_ref[0]) + sh_ref[0]
            a_s[pl.ds(r0, rc), :] = a.astype(BF16)
            return carry
        lax.fori_loop(0, l // rc, norm_rows, 0)

    p = jnp.dot(a_s[...], w_ref[...], preferred_element_type=F32)

    @pl.when(n < IN0_CONV_TILES)
    def _():
        zero_halo = jnp.zeros((halo, IN0_TILE), F32)
        for s in (p0_s, pm_s, pp_s):
            s[pl.ds(0, halo), :] = zero_halo
            s[pl.ds(halo + l, halo), :] = zero_halo
        p0_s[pl.ds(halo, l), :] = p
        tcol = lax.broadcasted_iota(jnp.int32, (l, IN0_TILE), 0) % n_cols
        pm_s[pl.ds(halo, l), :] = jnp.where(tcol != 0, p0_s[pl.ds(halo - 1, l), :], 0.0)
        pp_s[pl.ds(halo, l), :] = jnp.where(tcol != n_cols - 1, p0_s[pl.ds(halo + 1, l), :], 0.0)
        dys = (0,) if n_rows == 1 else (-1, 0, 1)

        def conv_rows(i, carry):
            r0 = pl.multiple_of(i * rc, rc)
            acc = jnp.zeros((rc, IN0_TILE), F32) + cb_ref[...]
            for dy in dys:
                base = halo + dy * n_cols
                for dx, src in ((0, pm_s), (1, p0_s), (2, pp_s)):
                    tap = (dy + 1) * 3 + dx
                    acc = acc + cw_ref[tap:tap + 1, :] * src[pl.ds(r0 + base, rc), :]
            conv_ref[0, pl.ds(r0, rc), :] = (acc * jax.nn.sigmoid(acc)).astype(conv_ref.dtype)
            return carry
        lax.fori_loop(0, l // rc, conv_rows, 0)

    @pl.when((n >= IN0_CONV_TILES) & (n < IN0_CONV_TILES + IN0_Z_TILES))
    def _():
        z_ref[0] = (p * jax.nn.sigmoid(p)).astype(z_ref.dtype)

    @pl.when(n == IN0_TILES - 1)
    def _():
        pg = p[:, :128]
        lane = lax.broadcasted_iota(jnp.int32, pg.shape, 1)
        xb = pg + gp_ref[0:1, :]
        sp = jnp.maximum(xb, 0.0) + jnp.log(1.0 + jnp.exp(-jnp.abs(xb)))
        neg_a_sp = -jnp.exp(gp_ref[1:2, :]) * sp
        gates = jnp.where(lane < 2 * GDN_HEADS, neg_a_sp,
                          jnp.where(lane < GATE_COL_DT, jax.nn.sigmoid(pg),
                                    jnp.where(lane < GATE_COL_DA, sp,
                                              jnp.where(lane < GATE_COL_DA + 2 * SSD_HEADS, neg_a_sp, 0.0))))
        g_ref[0] = gates
        for ci in range(l // SCAN_C):
            gt_ref[0, ci] = gates[ci * SCAN_C:(ci + 1) * SCAN_C, :].T


def _in0_features(h, mods, norm_w, w_all, conv_w, conv_b, gate_params, n_rows, n_cols):
    b, l, d = h.shape
    nb = mods.shape[0]
    bsel = (lambda bi: bi) if nb > 1 else (lambda bi: 0)
    halo = _conv_halo(n_cols)
    nct, nzt = IN0_CONV_TILES, IN0_Z_TILES
    pad_rows = l + 2 * halo
    return pl.pallas_call(
        functools.partial(_in0_body, n_rows=n_rows, n_cols=n_cols),
        grid=(b, IN0_TILES),
        in_specs=[
            pl.BlockSpec((1, l, d), lambda bi, ni: (bi, 0, 0)),
            pl.BlockSpec((1, 1, d), lambda bi, ni: (bsel(bi), 0, 0)),
            pl.BlockSpec((1, 1, d), lambda bi, ni: (bsel(bi), 0, 1)),
            pl.BlockSpec((1, d), lambda bi, ni: (0, 0)),
            pl.BlockSpec((d, IN0_TILE), lambda bi, ni: (0, ni)),
            pl.BlockSpec((9, IN0_TILE), lambda bi, ni: (0, jnp.minimum(ni, nct - 1))),
            pl.BlockSpec((1, IN0_TILE), lambda bi, ni: (0, jnp.minimum(ni, nct - 1))),
            pl.BlockSpec((2, 128), lambda bi, ni: (0, 0)),
        ],
        out_specs=[
            pl.BlockSpec((1, l, IN0_TILE), lambda bi, ni: (bi, 0, jnp.minimum(ni, nct - 1))),
            pl.BlockSpec((1, l, IN0_TILE), lambda bi, ni: (bi, 0, jnp.clip(ni - nct, 0, nzt - 1))),
            pl.BlockSpec((1, l, 128), lambda bi, ni: (bi, 0, 0)),
            pl.BlockSpec((1, l // SCAN_C, 128, SCAN_C), lambda bi, ni: (bi, 0, 0, 0)),
        ],
        out_shape=[
            jax.ShapeDtypeStruct((b, l, CONV_CH), BF16),
            jax.ShapeDtypeStruct((b, l, Z_CH), BF16),
            jax.ShapeDtypeStruct((b, l, 128), F32),
            jax.ShapeDtypeStruct((b, l // SCAN_C, 128, SCAN_C), F32),
        ],
        scratch_shapes=[
            pltpu.VMEM((l, d), BF16),
            pltpu.VMEM((pad_rows, IN0_TILE), F32),
            pltpu.VMEM((pad_rows, IN0_TILE), F32),
            pltpu.VMEM((pad_rows, IN0_TILE), F32),
        ],
        compiler_params=pltpu.CompilerParams(
            dimension_semantics=("arbitrary", "arbitrary"), vmem_limit_bytes=V7X_VMEM_LIMIT_BYTES),
        name="in0_features",
    )(h, mods, mods, norm_w.reshape(1, d), w_all, conv_w, conv_b, gate_params)


def _in0_params(w_in, gdn_conv_w, ssd_conv_w, ssd_conv_b, gdn_a_log, gdn_dt_bias, ssd_a_log, ssd_dt_bias):
    splits = np.cumsum(IN0_SIZES)[:-1].tolist()
    w_ca, w_cb, w_za, w_zb, w_a, w_b, w_dt = jnp.split(w_in, splits, axis=1)
    d = w_in.shape[0]
    w_gate = jnp.concatenate([w_a, w_b, w_dt, w_dt], axis=1)
    w_gate = jnp.pad(w_gate, ((0, 0), (0, IN0_TILE - w_gate.shape[1])))
    w_all = jnp.concatenate([w_ca, w_cb, w_za, w_zb, w_gate], axis=1).astype(BF16)
    conv_w = jnp.concatenate([gdn_conv_w, ssd_conv_w], axis=-1).reshape(9, CONV_CH)
    conv_b = jnp.concatenate([jnp.zeros((GDN_CONV_CH,), F32), ssd_conv_b]).reshape(1, CONV_CH)
    z8 = jnp.zeros((2 * GDN_HEADS,), F32)
    z16 = jnp.zeros((2 * SSD_HEADS,), F32)
    tail = jnp.zeros((128 - GATE_COL_DA - 2 * SSD_HEADS,), F32)
    bias = jnp.concatenate([gdn_dt_bias.reshape(-1), z8, ssd_dt_bias.reshape(-1), ssd_dt_bias.reshape(-1), tail])
    alog = jnp.concatenate([gdn_a_log.reshape(-1), z8, z16, ssd_a_log.reshape(-1), tail])
    return w_all, conv_w, conv_b, jnp.stack([bias, alog])


def _merge0_body(o_ref, y_ref, xs_ref, z_ref, h_ref, gate_ref, sh_ref, sc_ref, gnw_ref, dsk_ref, snw_ref, wo_ref,
                 postw_ref, prew_ref, rw_ref, rb_ref, h1_ref, t_ref, idx_ref, p_ref):
    z = z_ref[0].astype(F32)
    parts = []
    for hd in range(GDN_HEADS):
        cs = slice(hd * GDN_DV, (hd + 1) * GDN_DV)
        parts.append(_k_rms(o_ref[0, :, cs], gnw_ref[...]) * z[:, cs])
    gw = SSD_INNER // SSD_GROUPS
    for g in range(SSD_GROUPS):
        cs = slice(g * gw, (g + 1) * gw)
        y2 = (y_ref[0, :, cs] + dsk_ref[:, cs] * xs_ref[0, :, cs].astype(F32)) * z[:, GDN_V_WIDTH + g * gw:GDN_V_WIDTH + (g + 1) * gw]
        parts.append(_k_rms(y2, snw_ref[:, cs]))
    mixed = jnp.concatenate(parts, axis=-1).astype(BF16)
    ym = jnp.dot(mixed, wo_ref[...], preferred_element_type=F32)
    h1, t, idx, probs = _k_post_mix(ym, h_ref[0], gate_ref[0], postw_ref[...], prew_ref[...],
                                    sh_ref[0], sc_ref[0], rw_ref[...], rb_ref[...])
    h1_ref[0] = h1
    t_ref[0] = t.astype(BF16)
    idx_ref[0] = idx
    p_ref[0] = probs


def _merge0_postmix(o, y, conv, z, h, mods, gdn_norm_w, ssd_d, ssd_norm_w, w_out, post_w, pre_w, rw_pad, rb_pad, tl):
    b, l, d = h.shape
    nb = mods.shape[0]
    bsel = (lambda bi: bi) if nb > 1 else (lambda bi: 0)
    row = lambda v: v.reshape(1, -1)
    full = lambda shape: pl.BlockSpec(shape, lambda bi, ti: (0,) * len(shape))
    mod = lambda k: pl.BlockSpec((1, 1, d), lambda bi, ti: (bsel(bi), 0, k))
    tok = lambda w: pl.BlockSpec((1, tl, w), lambda bi, ti: (bi, ti, 0))
    xs_block = GDN_CONV_CH // SSD_INNER
    dsk = jnp.repeat(ssd_d, SSD_HEADDIM)
    return pl.pallas_call(
        _merge0_body,
        grid=(b, l // tl),
        in_specs=[
            tok(GDN_V_WIDTH), tok(SSD_INNER),
            pl.BlockSpec((1, tl, SSD_INNER), lambda bi, ti: (bi, ti, xs_block)),
            tok(Z_CH), tok(d), mod(2), mod(3), mod(4),
            full((1, GDN_DV)), full((1, SSD_INNER)), full((1, SSD_INNER)), full((Z_CH, d)),
            full((1, d)), full((1, d)), full((d, 128)), full((1, 128)),
        ],
        out_specs=[tok(d), tok(d), tok(128), tok(128)],
        out_shape=[
            jax.ShapeDtypeStruct((b, l, d), F32),
            jax.ShapeDtypeStruct((b, l, d), BF16),
            jax.ShapeDtypeStruct((b, l, 128), jnp.int32),
            jax.ShapeDtypeStruct((b, l, 128), F32),
        ],
        compiler_params=pltpu.CompilerParams(
            dimension_semantics=("arbitrary", "arbitrary"), vmem_limit_bytes=V7X_VMEM_LIMIT_BYTES),
        name="merge0_postmix",
    )(o, y, conv, z, h, mods, mods, mods, row(gdn_norm_w), row(dsk), row(ssd_norm_w), w_out,
      row(post_w), row(pre_w), rw_pad, rb_pad)


SCAN_C = 128
GDN_INV_BLOCK = 16


def _dot(a, b):
    return jnp.dot(a, b, preferred_element_type=F32)


def _dot_nt(a, b):
    return lax.dot_general(a, b, (((1,), (1,)), ((), ())), preferred_element_type=F32)


def _dot_tn(a, b):
    return lax.dot_general(a, b, (((0,), (0,)), ((), ())), preferred_element_type=F32)


def _bdot(a, b):
    return _dot(a.astype(BF16), b.astype(BF16))


def _bdot_nt(a, b):
    return _dot_nt(a.astype(BF16), b.astype(BF16))


def _bdot_tn(a, b):
    return _dot_tn(a.astype(BF16), b.astype(BF16))


def _scan_masks(fwd):
    row = lax.broadcasted_iota(jnp.int32, (SCAN_C, SCAN_C), 0)
    col = lax.broadcasted_iota(jnp.int32, (SCAN_C, SCAN_C), 1)
    lead = (row - col) * jnp.where(fwd, 1, -1)
    return row, col, lead >= 0, lead <= 0, lead > 0


def _cumsum_col_row(g_col, g_row, incl, incl_t):
    gc_col = jnp.sum(jnp.where(incl, g_row, 0.0), axis=1, keepdims=True)
    gc_row = jnp.sum(jnp.where(incl_t, g_col, 0.0), axis=0, keepdims=True)
    return gc_col, gc_row


def _unit_tri_inverse(ms, row, col):
    eye = (row == col).astype(F32)
    same = (row // GDN_INV_BLOCK) == (col // GDN_INV_BLOCK)
    mds = [jnp.where(same, m, 0.0) for m in ms]
    mos = [m - md for m, md in zip(ms, mds)]
    ps = mds
    tds = [eye - md for md in mds]
    for _ in range(int(math.log2(GDN_INV_BLOCK)) - 1):
        ps = [_dot(p, p) for p in ps]
        tds = [td + _dot(td, p) for td, p in zip(tds, ps)]
    ps = [_dot(td, mo) for td, mo in zip(tds, mos)]
    qs = [eye - n for n in ps]
    for _ in range(int(math.log2(SCAN_C // GDN_INV_BLOCK)) - 1):
        ps = [_dot(p, p) for p in ps]
        qs = [q + _dot(q, p) for q, p in zip(qs, ps)]
    return [_dot(q, td) for q, td in zip(qs, tds)]


GDN_PREP_GROUP = 8


def _gdn_body(ql_ref, kl_ref, vl_ref, gl_ref, gtl_ref, qc_ref, kc_ref, vc_ref, gc_ref, gtc_ref,
              ol_ref, oc_ref, u_s, w_s, qk_s, qd_s, kd_s, dec_s, obl_s, obc_s):
    h = pl.program_id(1)
    c = SCAN_C
    ncc = qc_ref.shape[1] // c
    ncl = ql_ref.shape[1] // c
    lane = lax.broadcasted_iota(jnp.int32, (c, 128), 1)

    def order(d, i, n):
        return i if d == 0 else n - 1 - i

    def prep(d, q_ref, k_ref, v_ref, g_ref, gt_ref, cis, ps):
        row, col, incl, incl_t, strict = _scan_masks(d == 0)
        col_g = d * GDN_HEADS + h
        col_b = 2 * GDN_HEADS + col_g
        loaded = []
        for ci in cis:
            t0 = pl.multiple_of(ci * c, c)
            loaded.append((q_ref[0, pl.ds(t0, c), :].astype(F32), k_ref[0, pl.ds(t0, c), :].astype(F32),
                           v_ref[0, pl.ds(t0, c), :].astype(F32), g_ref[0, pl.ds(t0, c), :],
                           gt_ref[0, ci, pl.ds(col_g, 1), :]))
        parts, ms = [], []
        for q, k, v, gch, g_row in loaded:
            q = q * lax.rsqrt(jnp.sum(q * q, axis=-1, keepdims=True) + RMS_EPS) * (GDN_DK ** -0.5)
            k = k * lax.rsqrt(jnp.sum(k * k, axis=-1, keepdims=True) + RMS_EPS)
            g_col = jnp.sum(jnp.where(lane == col_g, gch, 0.0), axis=1, keepdims=True)
            b_col = jnp.sum(jnp.where(lane == col_b, gch, 0.0), axis=1, keepdims=True)
            gc_col, gc_row = _cumsum_col_row(g_col, g_row, incl, incl_t)
            g_tot = jnp.sum(g_col, axis=0, keepdims=True)
            decay = jnp.where(incl, jnp.exp(jnp.where(incl, gc_col - gc_row, 0.0)), 0.0)
            kb = k * b_col
            ms.append(jnp.where(strict, _bdot_nt(kb, k) * decay, 0.0))
            egc = jnp.exp(gc_col)
            parts.append((v * b_col, kb * egc, _bdot_nt(q, k) * decay, q * egc, k * jnp.exp(g_tot - gc_col),
                          jnp.broadcast_to(jnp.exp(g_tot), (1, 128))))
        t_invs = _unit_tri_inverse(ms, row, col)
        uw = [(_bdot(t_inv, part[0]), _bdot(t_inv, part[1])) for t_inv, part in zip(t_invs, parts)]
        for p, (u, w), (_, _, qk, qd, kd, dec) in zip(ps, uw, parts):
            p0 = pl.multiple_of(p * c, c)
            u_s[d, pl.ds(p0, c), :] = u
            w_s[d, pl.ds(p0, c), :] = w
            qk_s[d, pl.ds(p0, c), :] = qk
            qd_s[d, pl.ds(p0, c), :] = qd
            kd_s[d, pl.ds(p0, c), :] = kd
            dec_s[d, pl.ds(p, 1), :] = dec

    def advance(of_ref, ob_ref, i, n, p, states):
        p0 = pl.multiple_of(p * c, c)
        outs, new_states = [], []
        for d, s in enumerate(states):
            v_new = u_s[d, pl.ds(p0, c), :] - _bdot(w_s[d, pl.ds(p0, c), :], s)
            outs.append(_bdot(qd_s[d, pl.ds(p0, c), :], s) + _bdot(qk_s[d, pl.ds(p0, c), :], v_new))
            new_states.append(s * dec_s[d, pl.ds(p, 1), :] + _bdot_tn(kd_s[d, pl.ds(p0, c), :], v_new))
        of_ref[0, pl.ds(pl.multiple_of(order(0, i, n) * c, c), c), :] = outs[0]
        ob_ref[pl.ds(pl.multiple_of(order(1, i, n) * c, c), c), :] = outs[1]
        return tuple(new_states)

    for d in range(2):
        prep(d, qc_ref, kc_ref, vc_ref, gc_ref, gtc_ref, [order(d, i, ncc) for i in range(ncc)], list(range(ncc)))

        def prep_lat(gi, carry, d=d):
            base = gi * GDN_PREP_GROUP
            prep(d, ql_ref, kl_ref, vl_ref, gl_ref, gtl_ref,
                 [order(d, base + j, ncl) for j in range(GDN_PREP_GROUP)],
                 [ncc + base + j for j in range(GDN_PREP_GROUP)])
            return carry

        lax.fori_loop(0, ncl // GDN_PREP_GROUP, prep_lat, 0)

    states = (jnp.zeros((GDN_DK, GDN_DV), F32), jnp.zeros((GDN_DK, GDN_DV), F32))
    for i in range(ncc):
        states = advance(oc_ref, obc_s, i, ncc, i, states)
    lax.fori_loop(0, ncl, lambda i, st: advance(ol_ref, obl_s, i, ncl, ncc + i, st), states)
    ol_ref[0] = ol_ref[0] + obl_s[...]
    oc_ref[0] = oc_ref[0] + obc_s[...]


SSD_R = SSD_HEADS // SSD_GROUPS
SSD_PAIRS = SSD_R * SSD_HEADDIM // 128
GATE_COL_DT = 4 * GDN_HEADS
GATE_COL_DA = GATE_COL_DT + 2 * SSD_HEADS


def _ssd_body(xl_ref, bl_ref, cl_ref, gl_ref, gtl_ref, xc_ref, bc_ref, cc_ref, gc_ref, gtc_ref,
              yl_ref, yc_ref, st_ref):
    g = pl.program_id(1)
    d = pl.program_id(2)
    fwd = d == 0
    c = SCAN_C
    ncc = xc_ref.shape[1] // c
    ncl = xl_ref.shape[1] // c
    _, _, incl, incl_t, _ = _scan_masks(fwd)
    lane = lax.broadcasted_iota(jnp.int32, (c, 128), 1)
    low = lane < SSD_HEADDIM

    @pl.when(fwd)
    def _():
        yl_ref[...] = jnp.zeros(yl_ref.shape, F32)
        yc_ref[...] = jnp.zeros(yc_ref.shape, F32)

    st_ref[...] = jnp.zeros(st_ref.shape, F32)

    def chunk(x_ref, b_ref, c_ref, g_ref, gt_ref, y_ref, ci):
        t0 = pl.multiple_of(ci * c, c)
        bm = b_ref[0, pl.ds(t0, c), :].astype(BF16)
        cm = c_ref[0, pl.ds(t0, c), :].astype(BF16)
        gch = g_ref[0, pl.ds(t0, c), :]
        cb = _dot_nt(cm, bm)
        for pr in range(SSD_PAIRS):
            per_head = []
            for s in range(2):
                hh = g * SSD_R + 2 * pr + s
                col_dt = GATE_COL_DT + d * SSD_HEADS + hh
                col_da = GATE_COL_DA + d * SSD_HEADS + hh
                dt_col = jnp.sum(jnp.where(lane == col_dt, gch, 0.0), axis=1, keepdims=True)
                da_col = jnp.sum(jnp.where(lane == col_da, gch, 0.0), axis=1, keepdims=True)
                da_row = gt_ref[0, ci, pl.ds(col_da, 1), :]
                acs_col, acs_row = _cumsum_col_row(da_col, da_row, incl, incl_t)
                a_tot = jnp.sum(da_col, axis=0, keepdims=True)
                lmat = jnp.where(incl, jnp.exp(jnp.where(incl, acs_col - acs_row, 0.0)), 0.0)
                per_head.append((dt_col, acs_col, a_tot, (cb * lmat).astype(BF16)))
            pick = lambda k: jnp.where(low, per_head[0][k], per_head[1][k])
            x = x_ref[0, pl.ds(t0, c), pr * 128:(pr + 1) * 128].astype(F32)
            xdt = x * pick(0)
            acs = pick(1)
            a_tot = pick(2)
            xdt_b = xdt.astype(BF16)
            y_diag = jnp.where(low, _dot(per_head[0][3], xdt_b), _dot(per_head[1][3], xdt_b))
            st = st_ref[pr]
            y_off = _dot(cm, st.astype(BF16)) * jnp.exp(acs)
            y_ref[0, pl.ds(t0, c), pr * 128:(pr + 1) * 128] += y_diag + y_off
            st_ref[pr] = st * jnp.exp(a_tot) + _dot_tn(bm, (xdt * jnp.exp(a_tot - acs)).astype(BF16))

    def order(i, n):
        return jnp.where(fwd, i, n - 1 - i)

    for i in range(ncc):
        chunk(xc_ref, bc_ref, cc_ref, gc_ref, gtc_ref, yc_ref, order(i, ncc))

    def lat(i, carry):
        chunk(xl_ref, bl_ref, cl_ref, gl_ref, gtl_ref, yl_ref, order(i, ncl))
        return carry

    lax.fori_loop(0, ncl, lat, 0)


def _ssd_scan(xbc_l, g_l, gt_l, xbc_c, g_c, gt_c, col0=0):
    b, l, _ = xbc_l.shape
    lc = xbc_c.shape[1]
    gw = SSD_R * SSD_HEADDIM
    x0 = col0 // gw
    nxb = (col0 + SSD_INNER) // 128

    def stream(n):
        return [pl.BlockSpec((1, n, gw), lambda bi, gi, di: (bi, 0, x0 + gi)),
                pl.BlockSpec((1, n, 128), lambda bi, gi, di: (bi, 0, nxb + gi)),
                pl.BlockSpec((1, n, 128), lambda bi, gi, di: (bi, 0, nxb + SSD_GROUPS + gi)),
                pl.BlockSpec((1, n, 128), lambda bi, gi, di: (bi, 0, 0)),
                pl.BlockSpec((1, n // SCAN_C, 128, SCAN_C), lambda bi, gi, di: (bi, 0, 0, 0))]

    return pl.pallas_call(
        _ssd_body,
        grid=(b, SSD_GROUPS, 2),
        in_specs=stream(l) + stream(lc),
        out_specs=[pl.BlockSpec((1, l, gw), lambda bi, gi, di: (bi, 0, gi)),
                   pl.BlockSpec((1, lc, gw), lambda bi, gi, di: (bi, 0, gi))],
        out_shape=[jax.ShapeDtypeStruct((b, l, SSD_INNER), F32),
                   jax.ShapeDtypeStruct((b, lc, SSD_INNER), F32)],
        scratch_shapes=[pltpu.VMEM((SSD_PAIRS, SSD_STATE, 128), F32)],
        compiler_params=pltpu.CompilerParams(
            dimension_semantics=("arbitrary", "arbitrary", "arbitrary"), vmem_limit_bytes=V7X_VMEM_LIMIT_BYTES),
        name="ssd_scan",
    )(xbc_l, xbc_l, xbc_l, g_l, gt_l, xbc_c, xbc_c, xbc_c, g_c, gt_c)


def _chunk_rows(gt):
    b, w, l = gt.shape
    return gt.reshape(b, w, l // SCAN_C, SCAN_C).transpose(0, 2, 1, 3)


def _gdn_scan(qkv_l, g_l, gt_l, qkv_c, g_c, gt_c):
    b, l, _ = qkv_l.shape
    lc = qkv_c.shape[1]
    hd = GDN_HEADS

    def stream(n):
        tok = lambda off: pl.BlockSpec((1, n, 128), lambda bi, hi: (bi, 0, off + hi))
        return [tok(0), tok(hd), tok(2 * hd),
                pl.BlockSpec((1, n, 128), lambda bi, hi: (bi, 0, 0)),
                pl.BlockSpec((1, n // SCAN_C, 128, SCAN_C), lambda bi, hi: (bi, 0, 0, 0))]

    nt = l + lc
    return pl.pallas_call(
        _gdn_body,
        grid=(b, hd),
        in_specs=stream(l) + stream(lc),
        out_specs=[pl.BlockSpec((1, l, 128), lambda bi, hi: (bi, 0, hi)),
                   pl.BlockSpec((1, lc, 128), lambda bi, hi: (bi, 0, hi))],
        out_shape=[jax.ShapeDtypeStruct((b, l, hd * GDN_DV), F32),
                   jax.ShapeDtypeStruct((b, lc, hd * GDN_DV), F32)],
        scratch_shapes=[pltpu.VMEM((2, nt, 128), F32) for _ in range(5)]
        + [pltpu.VMEM((2, nt // SCAN_C, 128), F32), pltpu.VMEM((l, 128), F32), pltpu.VMEM((lc, 128), F32)],
        compiler_params=pltpu.CompilerParams(
            dimension_semantics=("arbitrary", "arbitrary"), vmem_limit_bytes=V7X_VMEM_LIMIT_BYTES),
        name="gdn_scan",
    )(qkv_l, qkv_l, qkv_l, g_l, gt_l, qkv_c, qkv_c, qkv_c, g_c, gt_c)


def _rmsnorm(t, w):
    tf = t.astype(F32)
    tf = tf * lax.rsqrt(jnp.mean(tf * tf, axis=-1, keepdims=True) + RMS_EPS)
    return tf * w.astype(F32)


def _l2norm(t):
    tf = t.astype(F32)
    return tf * lax.rsqrt(jnp.sum(tf * tf, axis=-1, keepdims=True) + RMS_EPS)


def _modulate(t, shift, scale):
    return t * (1.0 + scale) + shift


def _grid_dwconv(t, w, n_rows, n_cols):
    b, l, ch = t.shape
    img = t.reshape(b, n_rows, n_cols, ch)
    out = lax.conv_general_dilated(img, w[:, :, None, :], window_strides=(1, 1), padding='SAME',
                                   dimension_numbers=('NHWC', 'HWIO', 'NHWC'), feature_group_count=ch)
    return out.reshape(b, l, ch)


def _gated_delta_chunked(q, k, v, g, beta, s0):
    b, l, h, dk = q.shape
    c = GDN_CHUNK
    n = l // c

    def chunks(t):
        return t.astype(F32).reshape(b, n, c, h, -1).transpose(1, 0, 3, 2, 4)

    qc = chunks(q) * (dk ** -0.5)
    kc, vc = chunks(k), chunks(v)
    gc = jnp.cumsum(g.astype(F32).reshape(b, n, c, h).transpose(1, 0, 3, 2), axis=-1)
    bc = beta.astype(F32).reshape(b, n, c, h).transpose(1, 0, 3, 2)
    causal = jnp.tril(jnp.ones((c, c), bool))
    strict = jnp.tril(jnp.ones((c, c), bool), -1)
    seg = gc[..., :, None] - gc[..., None, :]
    decay = jnp.where(causal, jnp.exp(jnp.where(causal, seg, 0.0)), 0.0)
    kb = kc * bc[..., None]
    m = jnp.where(strict, jnp.einsum('nbhid,nbhjd->nbhij', kb, kc) * decay, 0.0)
    tri = jnp.eye(c, dtype=F32) + m
    u = lax.linalg.triangular_solve(tri, vc * bc[..., None], left_side=True, lower=True, unit_diagonal=True)
    w = lax.linalg.triangular_solve(tri, kb * jnp.exp(gc)[..., None], left_side=True, lower=True, unit_diagonal=True)
    qk = jnp.einsum('nbhid,nbhjd->nbhij', qc, kc) * decay
    q_dec = qc * jnp.exp(gc)[..., None]
    k_dec = kc * jnp.exp(gc[..., -1:] - gc)[..., None]
    g_last = jnp.exp(gc[..., -1])

    def step(s, inp):
        u_i, w_i, qk_i, qd_i, kd_i, gl_i = inp
        v_new = u_i - w_i @ s
        o_i = qd_i @ s + qk_i @ v_new
        s = s * gl_i[..., None, None] + jnp.swapaxes(kd_i, -1, -2) @ v_new
        return s, o_i

    s_fin, o = lax.scan(step, s0.astype(F32), (u, w, qk, q_dec, k_dec, g_last))
    o = o.transpose(1, 0, 3, 2, 4).reshape(b, l, h, -1)
    return o, s_fin


def _ssd_chunked(x, dt, da, bm, cm, s0):
    b, l, h, p = x.shape
    g, nst = bm.shape[2], bm.shape[3]
    r = h // g
    c = SSD_CHUNK
    n = l // c
    xc = (x.astype(F32) * dt.astype(F32)[..., None]).reshape(b, n, c, g, r, p)
    acs = jnp.cumsum(da.astype(F32).reshape(b, n, c, g, r), axis=2)
    bc = bm.astype(F32).reshape(b, n, c, g, nst)
    cc = cm.astype(F32).reshape(b, n, c, g, nst)
    causal = jnp.tril(jnp.ones((c, c), bool))[:, :, None, None]
    seg = acs[:, :, :, None] - acs[:, :, None, :]
    lmat = jnp.where(causal, jnp.exp(jnp.where(causal, seg, 0.0)), 0.0)
    scores = jnp.einsum('bnlgd,bnsgd->bnlsg', cc, bc)[..., None] * lmat
    y_diag = jnp.einsum('bnlsgr,bnsgrp->bnlgrp', scores, xc)
    decay_out = jnp.exp(acs[:, :, -1:] - acs)
    chunk_states = jnp.einsum('bnsgd,bnsgrp->bngrpd', bc, xc * decay_out[..., None])
    states = jnp.concatenate([s0.astype(F32).reshape(b, 1, g, r, p, nst), chunk_states], axis=1)
    cum = jnp.cumsum(jnp.pad(acs[:, :, -1], ((0, 0), (1, 0), (0, 0), (0, 0))), axis=1)
    tri = jnp.tril(jnp.ones((n + 1, n + 1), bool))[:, :, None, None]
    segc = cum[:, :, None] - cum[:, None, :]
    dec = jnp.where(tri, jnp.exp(jnp.where(tri, segc, 0.0)), 0.0)
    new = jnp.einsum('bzcgr,bcgrpd->bzgrpd', dec, states)
    y_off = jnp.einsum('bnlgd,bngrpd->bnlgrp', cc, new[:, :-1]) * jnp.exp(acs)[..., None]
    y = (y_diag + y_off).reshape(b, l, h, p)
    return y, new[:, -1].reshape(b, h, p, nst)


def _bidirectional(scan_fn, ctx_dirs, lat_dirs, s0):
    flip = lambda t: jnp.flip(t, axis=1)
    y_ctx, y_lat = 0.0, 0.0
    for d in range(2):
        rev = d == 1
        cargs = tuple(flip(t) for t in ctx_dirs[d]) if rev else ctx_dirs[d]
        largs = tuple(flip(t) for t in lat_dirs[d]) if rev else lat_dirs[d]
        o_c, s_c = scan_fn(*cargs, s0)
        o_l, _ = scan_fn(*largs, s_c)
        y_ctx = y_ctx + (flip(o_c) if rev else o_c)
        y_lat = y_lat + (flip(o_l) if rev else o_l)
    return y_ctx, y_lat


def _hybrid_mixer(h_ctx, h_lat, rows, w_in, gdn_conv_w, gdn_a_log, gdn_dt_bias, gdn_norm_w,
                  ssd_conv_w, ssd_conv_b, ssd_a_log, ssd_dt_bias, ssd_d, ssd_norm_w, w_out):
    splits = np.cumsum(IN0_SIZES)[:-1].tolist()

    def features(h, n_rows, n_cols):
        b, l, _ = h.shape
        conv_a, conv_b, z_a, z_b, a_raw, b_raw, dt_raw = jnp.split(h @ w_in, splits, axis=-1)
        conv_a = jax.nn.silu(_grid_dwconv(conv_a, gdn_conv_w, n_rows, n_cols))
        conv_b = jax.nn.silu(_grid_dwconv(conv_b, ssd_conv_w, n_rows, n_cols) + ssd_conv_b)
        q, k, v = jnp.split(conv_a, [GDN_HEADS * GDN_DK, 2 * GDN_HEADS * GDN_DK], axis=-1)
        q = _l2norm(q.reshape(b, l, GDN_HEADS, GDN_DK))
        k = _l2norm(k.reshape(b, l, GDN_HEADS, GDN_DK))
        v = v.reshape(b, l, GDN_HEADS, GDN_DV)
        g = -jnp.exp(gdn_a_log) * jax.nn.softplus(a_raw.reshape(b, l, 2, GDN_HEADS) + gdn_dt_bias)
        beta = jax.nn.sigmoid(b_raw.reshape(b, l, 2, GDN_HEADS))
        xs, bm, cm = jnp.split(conv_b, [SSD_INNER, SSD_INNER + SSD_GROUPS * SSD_STATE], axis=-1)
        xs = xs.reshape(b, l, SSD_HEADS, SSD_HEADDIM)
        bm = bm.reshape(b, l, SSD_GROUPS, SSD_STATE)
        cm = cm.reshape(b, l, SSD_GROUPS, SSD_STATE)
        dt = jax.nn.softplus(dt_raw.reshape(b, l, 2, SSD_HEADS) + ssd_dt_bias)
        da = -jnp.exp(ssd_a_log) * dt
        gdn_dirs = tuple((q, k, v, g[:, :, d], beta[:, :, d]) for d in range(2))
        ssd_dirs = tuple((xs, dt[:, :, d], da[:, :, d], bm, cm) for d in range(2))
        return gdn_dirs, ssd_dirs, xs, z_a, z_b

    b = h_lat.shape[0]
    gdn_c, ssd_c, xs_c, za_c, zb_c = features(h_ctx, 1, h_ctx.shape[1])
    gdn_l, ssd_l, xs_l, za_l, zb_l = features(h_lat, rows, GRID_W)
    o_c, o_l = _bidirectional(_gated_delta_chunked, gdn_c, gdn_l, jnp.zeros((b, GDN_HEADS, GDN_DK, GDN_DV), F32))
    y_c, y_l = _bidirectional(_ssd_chunked, ssd_c, ssd_l, jnp.zeros((b, SSD_HEADS, SSD_HEADDIM, SSD_STATE), F32))

    def merge(o, y, xs, z_a, z_b):
        bb, l = o.shape[0], o.shape[1]
        o = _rmsnorm(o, gdn_norm_w) * jax.nn.silu(z_a).reshape(bb, l, GDN_HEADS, GDN_DV)
        y = (y + ssd_d[:, None] * xs).reshape(bb, l, SSD_INNER) * jax.nn.silu(z_b)
        y = _rmsnorm(y.reshape(bb, l, SSD_GROUPS, -1), ssd_norm_w.reshape(SSD_GROUPS, -1))
        mixed = jnp.concatenate([o.reshape(bb, l, -1), y.reshape(bb, l, -1)], axis=-1)
        return mixed @ w_out

    return merge(o_c, y_c, xs_c, za_c, zb_c), merge(o_l, y_l, xs_l, za_l, zb_l)


def _s5_discretize(a_re, a_im, log_step, b_re, b_im):
    step = jnp.exp(log_step)[:, None]
    mag = jnp.exp(a_re * step)
    lam_re, lam_im = mag * jnp.cos(a_im * step), mag * jnp.sin(a_im * step)
    den = a_re * a_re + a_im * a_im
    f_re = ((lam_re - 1.0) * a_re + lam_im * a_im) / den
    f_im = (lam_im * a_re - (lam_re - 1.0) * a_im) / den
    bb_re = f_re[..., None] * b_re - f_im[..., None] * b_im
    bb_im = f_re[..., None] * b_im + f_im[..., None] * b_re
    return lam_re, lam_im, bb_re, bb_im


def _route_and_ffn(t, router_w, router_b, wg, bg, wu, bu, wd, bd):
    shp = t.shape
    t2 = t.reshape(-1, shp[-1])
    logits = t2 @ router_w + router_b
    top_val, top_idx = lax.top_k(logits, TOP_K)
    probs = jax.nn.softmax(top_val, axis=-1)
    f = _moe_ffn(t2.astype(BF16), top_idx.astype(jnp.int32), probs, wg, bg, wu, bu, wd, bd)
    return f.reshape(shp)


def kernel(x, c, ctx, c_ctx, ada_w, ada_b, mix_norm_pre, mix_norm_post, ffn_norm_pre, ffn_norm_post, router_w, router_b, moe_w_gate, moe_b_gate, moe_w_up, moe_b_up, moe_w_down, moe_b_down, hy_w_in, gdn_conv_w, gdn_a_log, gdn_dt_bias, gdn_norm_w, ssd_conv_w, ssd_conv_b, ssd_a_log, ssd_dt_bias, ssd_d, ssd_norm_w, hy_w_out, s5_w_in, s5_a_re, s5_a_im, s5_log_step, s5_b_re, s5_b_im, s5_c_re, s5_c_im, s5_d, s5_w_glu_a, s5_w_glu_b):
    depth = ada_w.shape[0]
    rows = x.shape[1] // GRID_W
    h_lat, h_ctx = x, ctx
    cond_lat = jax.nn.silu(c)
    cond_ctx = jax.nn.silu(c_ctx)
    for i in range(depth):
        j = i // 2
        need_ctx = i < depth - 1
        mods_lat = (cond_lat @ ada_w[i] + ada_b[i])[:, None, :]
        mods_ctx = (cond_ctx @ ada_w[i] + ada_b[i])[None, None, :]
        m_lat = jnp.split(mods_lat, 6, axis=-1)
        m_ctx = jnp.split(mods_ctx[0, 0], 6, axis=-1)
        ffn = (i, moe_w_gate, moe_b_gate, moe_w_up, moe_b_up, moe_w_down, moe_b_down, ffn_norm_post[i])
        rw_pad, rb_pad = _pad_router(router_w[i], router_b[i])
        bsz, l, d = h_lat.shape
        lc = h_ctx.shape[1]
        tl_lat, tl_ctx = 512, lc
        if i % 2 == 1:
            assert not need_ctx
            nbg = bsz // S5_BATCH_SUB
            w_in = s5_w_in[j].astype(BF16)
            u_lat = _s5_inproj(h_lat, mods_lat, mix_norm_pre[i], w_in, tl_lat)
            u_ctx = _s5_inproj(h_ctx, mods_ctx, mix_norm_pre[i], w_in, tl_ctx)
            bblk, cblk, lam = _s5_block_params(s5_a_re[j], s5_a_im[j], s5_log_step[j], s5_b_re[j], s5_b_im[j],
                                               s5_c_re[j], s5_c_im[j])
            y = _s5_scan(u_ctx.reshape(nbg, lc, S5_BATCH_SUB, d), u_lat.reshape(nbg, l, S5_BATCH_SUB, d),
                         bblk, cblk, lam)
            h1, t_lat, idx, probs = _s5_glu_postmix(
                y, u_lat, h_lat, mods_lat, s5_d[j], s5_w_glu_a[j].astype(BF16), s5_w_glu_b[j].astype(BF16),
                mix_norm_post[i], ffn_norm_pre[i], rw_pad, rb_pad, tl_lat)
            (h_lat,) = _moe_layer([(t_lat, idx, probs, h1, mods_lat, tl_lat)], *ffn)
            continue
        w_all, conv_w, conv_b, gate_params = _in0_params(hy_w_in[j], gdn_conv_w[j], ssd_conv_w[j], ssd_conv_b[j],
                                                         gdn_a_log[j], gdn_dt_bias[j], ssd_a_log[j], ssd_dt_bias[j])
        conv_l, z_l, g_l, gt_l = _in0_features(h_lat, mods_lat, mix_norm_pre[i], w_all, conv_w, conv_b, gate_params,
                                               rows, GRID_W)
        conv_c, z_c, g_c, gt_c = _in0_features(h_ctx, mods_ctx, mix_norm_pre[i], w_all, conv_w, conv_b, gate_params,
                                               1, lc)
        o_l, o_c = _gdn_scan(conv_l, g_l, gt_l, conv_c, g_c, gt_c)
        y_l, y_c = _ssd_scan(conv_l, g_l, gt_l, conv_c, g_c, gt_c, col0=GDN_CONV_CH)
        w_out = hy_w_out[j].astype(BF16)
        merge = lambda o, y, conv, z, h, mods, tl: _merge0_postmix(
            o, y, conv, z, h, mods, gdn_norm_w[j], ssd_d[j], ssd_norm_w[j], w_out,
            mix_norm_post[i], ffn_norm_pre[i], rw_pad, rb_pad, tl)
        streams = [merge(o_l, y_l, conv_l, z_l, h_lat, mods_lat, tl_lat) + (mods_lat, tl_lat)]
        if need_ctx:
            streams.append(merge(o_c, y_c, conv_c, z_c, h_ctx, mods_ctx, tl_ctx) + (mods_ctx, tl_ctx))
        outs = _moe_layer([(t, idx, probs, h1, mods, tl) for h1, t, idx, probs, mods, tl in streams], *ffn)
        h_lat = outs[0]
        if need_ctx:
            h_ctx = outs[1]
    return h_lat
```

```python
import functools
import math

import jax
import jax.numpy as jnp
import numpy as np
from jax import lax
from jax.experimental import pallas as pl
from jax.experimental.pallas import tpu as pltpu

F32 = jnp.float32
BF16 = jnp.bfloat16

D_MODEL = 1024
GRID_W = 64
RMS_EPS = 1e-6

GDN_HEADS = 4
GDN_DK = 128
GDN_DV = 128
GDN_CHUNK = 64
SSD_HEADS = 8
SSD_HEADDIM = 64
SSD_GROUPS = 2
SSD_STATE = 128
SSD_CHUNK = 128
S5_GROUP = 16
S5_GROUPS = D_MODEL // S5_GROUP
S5_STATE = 64
N_EXPERTS = 32
TOP_K = 4
SWIGLU_LIMIT = 7.0
SWIGLU_ALPHA = 1.702

GDN_V_WIDTH = GDN_HEADS * GDN_DV
SSD_INNER = SSD_HEADS * SSD_HEADDIM
GDN_CONV_CH = 2 * GDN_HEADS * GDN_DK + GDN_V_WIDTH
SSD_CONV_CH = SSD_INNER + 2 * SSD_GROUPS * SSD_STATE
IN0_SIZES = (GDN_CONV_CH, SSD_CONV_CH, GDN_V_WIDTH, SSD_INNER, 2 * GDN_HEADS, 2 * GDN_HEADS, 2 * SSD_HEADS)

V7X_VMEM_LIMIT_BYTES = 56 * 1024 * 1024
MOE_TILE_M = 512
MOE_TILE_F = 512
MOE_BATCH_SLICES = 2


def _moe_ffn_body(tile_e_ref, tile_ok_ref, x_ref, wg_ref, bg_ref, wu_ref, bu_ref, wd_ref, bd_ref, o_ref,
                  wg_s, wu_s, wd_s):
    i = pl.program_id(0)
    n_f = wg_ref.shape[3] // MOE_TILE_F

    @pl.when((i == 0) | (tile_e_ref[i] != tile_e_ref[jnp.maximum(i - 1, 0)]))
    def _():
        for c in range(n_f):
            cs = slice(c * MOE_TILE_F, (c + 1) * MOE_TILE_F)
            wg_s[:, cs] = wg_ref[0, 0, :, cs].astype(BF16)
            wu_s[:, cs] = wu_ref[0, 0, :, cs].astype(BF16)
            wd_s[cs, :] = wd_ref[0, 0, cs, :].astype(BF16)

    @pl.when(tile_ok_ref[i] > 0)
    def _():
        x = x_ref[...]
        acc = jnp.zeros(o_ref.shape, F32)
        for c in range(n_f):
            cs = slice(c * MOE_TILE_F, (c + 1) * MOE_TILE_F)
            gl = jnp.dot(x, wg_s[:, cs], preferred_element_type=F32) + bg_ref[0, 0, :, cs]
            lin = jnp.dot(x, wu_s[:, cs], preferred_element_type=F32) + bu_ref[0, 0, :, cs]
            gl = jnp.minimum(gl, SWIGLU_LIMIT)
            lin = jnp.clip(lin, -SWIGLU_LIMIT, SWIGLU_LIMIT)
            act = gl * jax.nn.sigmoid(SWIGLU_ALPHA * gl) * (lin + 1.0)
            acc = acc + jnp.dot(act.astype(BF16), wd_s[cs, :], preferred_element_type=F32)
        o_ref[...] = (acc + bd_ref[0, 0]).astype(o_ref.dtype)

    @pl.when(tile_ok_ref[i] == 0)
    def _():
        o_ref[...] = jnp.zeros(o_ref.shape, o_ref.dtype)


def _moe_grouped_ffn(xs, tile_e, tile_ok, layer, wg, bg, wu, bu, wd, bd):
    p, d = xs.shape
    nl, e, _, f = wg.shape
    n_tiles = p // MOE_TILE_M
    grid_spec = pltpu.PrefetchScalarGridSpec(
        num_scalar_prefetch=2,
        grid=(n_tiles,),
        in_specs=[
            pl.BlockSpec((MOE_TILE_M, d), lambda i, te, ok: (i, 0)),
            pl.BlockSpec((1, 1, d, f), lambda i, te, ok: (layer, te[i], 0, 0)),
            pl.BlockSpec((1, 1, 1, f), lambda i, te, ok: (layer, te[i], 0, 0)),
            pl.BlockSpec((1, 1, d, f), lambda i, te, ok: (layer, te[i], 0, 0)),
            pl.BlockSpec((1, 1, 1, f), lambda i, te, ok: (layer, te[i], 0, 0)),
            pl.BlockSpec((1, 1, f, d), lambda i, te, ok: (layer, te[i], 0, 0)),
            pl.BlockSpec((1, 1, 1, d), lambda i, te, ok: (layer, te[i], 0, 0)),
        ],
        out_specs=pl.BlockSpec((MOE_TILE_M, d), lambda i, te, ok: (i, 0)),
        scratch_shapes=[pltpu.VMEM((d, f), BF16), pltpu.VMEM((d, f), BF16), pltpu.VMEM((f, d), BF16)],
    )
    return pl.pallas_call(
        _moe_ffn_body,
        grid_spec=grid_spec,
        out_shape=jax.ShapeDtypeStruct((p, d), BF16),
        compiler_params=pltpu.CompilerParams(
            dimension_semantics=("arbitrary",), vmem_limit_bytes=V7X_VMEM_LIMIT_BYTES),
        name="moe_grouped_ffn",
    )(tile_e, tile_ok, xs, wg, bg.reshape(nl, e, 1, f), wu, bu.reshape(nl, e, 1, f), wd, bd.reshape(nl, e, 1, d))


def _moe_combine_body(y0_ref, y1_ref, y2_ref, y3_ref, p_ref, h_ref, gate_ref, nw_ref, *rest):
    o_ref = rest[-1]
    p = p_ref[0]
    acc = jnp.zeros(o_ref.shape[1:], F32)
    for k, y_ref in enumerate((y0_ref, y1_ref, y2_ref, y3_ref)):
        acc = acc + p[:, k:k + 1] * y_ref[0, 0].astype(F32)
    o_ref[0] = h_ref[0] + gate_ref[0] * _k_rms(acc, nw_ref[...])


def _moe_combine(yg, probs, h, mods, norm_w, tl, b0, prev):
    b, l, d = h.shape
    nb = yg.shape[1]
    bsel = (lambda bi: bi + b0) if mods.shape[0] > 1 else (lambda bi: 0)
    yk = lambda k: pl.BlockSpec((1, 1, tl, d), lambda bi, ti: (k, bi, ti, 0))
    in_specs = [yk(0), yk(1), yk(2), yk(3),
                pl.BlockSpec((1, tl, 128), lambda bi, ti: (bi + b0, ti, 0)),
                pl.BlockSpec((1, tl, d), lambda bi, ti: (bi + b0, ti, 0)),
                pl.BlockSpec((1, 1, d), lambda bi, ti: (bsel(bi), 0, 5)),
                pl.BlockSpec((1, d), lambda bi, ti: (0, 0))]
    args = [yg, yg, yg, yg, probs, h, mods, norm_w.reshape(1, -1)]
    aliases = {}
    if prev is not None:
        in_specs.append(pl.BlockSpec(memory_space=pl.ANY))
        args.append(prev)
        aliases = {len(args) - 1: 0}
    return pl.pallas_call(
        _moe_combine_body,
        grid=(nb, l // tl),
        in_specs=in_specs,
        out_specs=pl.BlockSpec((1, tl, d), lambda bi, ti: (bi + b0, ti, 0)),
        out_shape=jax.ShapeDtypeStruct((b, l, d), F32),
        input_output_aliases=aliases,
        compiler_params=pltpu.CompilerParams(
            dimension_semantics=("arbitrary", "arbitrary"), vmem_limit_bytes=V7X_VMEM_LIMIT_BYTES),
        name="moe_combine",
    )(*args)


def _moe_route(top_idx):
    t = top_idx.shape[0]
    a = t * TOP_K
    tm = MOE_TILE_M
    eid = top_idx.reshape(a)
    order = jnp.argsort(eid, stable=True).astype(jnp.int32)
    inv = jnp.argsort(order).astype(jnp.int32)
    counts = jnp.sum(jax.nn.one_hot(eid, N_EXPERTS, dtype=jnp.int32), axis=0)
    off = jnp.cumsum(counts) - counts
    pcounts = ((counts + tm - 1) // tm) * tm
    pend = jnp.cumsum(pcounts)
    poff = pend - pcounts
    n_tiles = a // tm + N_EXPERTS
    tile_start = jnp.arange(n_tiles, dtype=jnp.int32) * tm
    n_done = jnp.sum((tile_start[:, None] >= pend[None, :]).astype(jnp.int32), axis=1)
    tile_e = jnp.minimum(n_done, N_EXPERTS - 1)
    tile_ok = (tile_start < pend[-1]).astype(jnp.int32)
    ppos = jnp.arange(n_tiles * tm, dtype=jnp.int32)
    pe = jnp.repeat(tile_e, tm)
    r = ppos - poff[pe]
    src_rank = jnp.clip(off[pe] + jnp.minimum(r, counts[pe] - 1), 0, a - 1)
    src_tok = order[src_rank] // TOP_K
    pos = poff[eid] + (inv - off[eid])
    return src_tok, pos, tile_e, tile_ok


def _moe_layer(streams, layer, wg, bg, wu, bu, wd, bd, norm_w):
    d = streams[0][0].shape[-1]
    outs = [None] * len(streams)
    for mb in range(MOE_BATCH_SLICES):
        cuts = [(mb * (s[0].shape[0] // MOE_BATCH_SLICES), s[0].shape[0] // MOE_BATCH_SLICES) for s in streams]
        t_all = jnp.concatenate([s[0][b0:b0 + nb].reshape(-1, d) for s, (b0, nb) in zip(streams, cuts)], axis=0)
        idx_all = jnp.concatenate([s[1][b0:b0 + nb].reshape(-1, 128)[:, :TOP_K]
                                   for s, (b0, nb) in zip(streams, cuts)], axis=0)
        src_tok, pos, tile_e, tile_ok = _moe_route(idx_all)
        xs = t_all.at[src_tok].get(mode="promise_in_bounds")
        ys = _moe_grouped_ffn(xs, tile_e, tile_ok, layer, wg, bg, wu, bu, wd, bd)
        pos_k = pos.reshape(-1, TOP_K).T
        start = 0
        for si, ((t, _, probs, h, mods, tl), (b0, nb)) in enumerate(zip(streams, cuts)):
            l = t.shape[1]
            n = nb * l
            yg = ys.at[pos_k[:, start:start + n].reshape(-1)].get(mode="promise_in_bounds")
            outs[si] = _moe_combine(yg.reshape(TOP_K, nb, l, d), probs, h, mods, norm_w, tl, b0, outs[si])
            start += n
    return outs


def _k_rms(t, w):
    return t * lax.rsqrt(jnp.mean(t * t, axis=-1, keepdims=True) + RMS_EPS) * w


def _k_post_mix(y, h, gate, post_w, pre_w, shift, scale, rw, rb):
    h1 = h + gate * _k_rms(y, post_w)
    t = _k_rms(h1, pre_w) * (1.0 + scale) + shift
    logits = jnp.dot(t, rw, preferred_element_type=F32) + rb
    lane = lax.broadcasted_iota(jnp.int32, logits.shape, 1)
    idx_out = jnp.zeros(logits.shape, jnp.int32)
    val_out = jnp.zeros(logits.shape, F32)
    work = logits
    m0 = None
    for k in range(TOP_K):
        m = jnp.max(work, axis=-1, keepdims=True)
        sel = jnp.min(jnp.where(work == m, lane, 128), axis=-1, keepdims=True)
        if k == 0:
            m0 = m
        idx_out = jnp.where(lane == k, sel, idx_out)
        val_out = jnp.where(lane == k, jnp.exp(m - m0), val_out)
        work = jnp.where(lane == sel, -jnp.inf, work)
    probs = val_out / jnp.sum(val_out, axis=-1, keepdims=True)
    return h1, t, idx_out, probs


S5_BATCH_SUB = 8
S5_CH_BLOCK = 128
S5_ST_BLOCK = (S5_CH_BLOCK // S5_GROUP) * S5_STATE
S5_TL = 256
S5_GLU_TL = 64


def _s5_inproj_body(h_ref, sh_ref, sc_ref, nw_ref, w_ref, u_ref, il_s):
    tl = h_ref.shape[1]
    ncb = il_s.shape[0]
    for b in range(S5_BATCH_SUB):
        mb = b if sh_ref.shape[0] > 1 else 0
        a = _k_rms(h_ref[b], nw_ref[...]) * (1.0 + sc_ref[mb]) + sh_ref[mb]
        u = jnp.dot(a.astype(BF16), w_ref[...], preferred_element_type=F32)
        for cb in range(ncb):
            il_s[cb, pl.ds(b, tl, stride=S5_BATCH_SUB), :] = u[:, cb * 128:(cb + 1) * 128]
    for cb in range(ncb):
        u_ref[0, :, cb * 128:(cb + 1) * 128] = il_s[cb]


def _s5_inproj(h, mods, norm_w, w_bf16, tl):
    b, l, d = h.shape
    nbg = b // S5_BATCH_SUB
    nm = S5_BATCH_SUB if mods.shape[0] > 1 else 1
    msel = (lambda gi: gi) if mods.shape[0] > 1 else (lambda gi: 0)
    return pl.pallas_call(
        _s5_inproj_body,
        grid=(nbg, l // tl),
        in_specs=[
            pl.BlockSpec((S5_BATCH_SUB, tl, d), lambda gi, ti: (gi, ti, 0)),
            pl.BlockSpec((nm, 1, d), lambda gi, ti: (msel(gi), 0, 0)),
            pl.BlockSpec((nm, 1, d), lambda gi, ti: (msel(gi), 0, 1)),
            pl.BlockSpec((1, d), lambda gi, ti: (0, 0)),
            pl.BlockSpec((d, d), lambda gi, ti: (0, 0)),
        ],
        out_specs=pl.BlockSpec((1, tl * S5_BATCH_SUB, d), lambda gi, ti: (gi, ti, 0)),
        out_shape=jax.ShapeDtypeStruct((nbg, l * S5_BATCH_SUB, d), F32),
        scratch_shapes=[pltpu.VMEM((d // 128, tl * S5_BATCH_SUB, 128), F32)],
        compiler_params=pltpu.CompilerParams(
            dimension_semantics=("arbitrary", "arbitrary"), vmem_limit_bytes=V7X_VMEM_LIMIT_BYTES),
        name="s5_inproj",
    )(h, mods, mods, norm_w.reshape(1, d), w_bf16)


def _s5_scan_body(uc_ref, ul_ref, b_ref, c_ref, lam_ref, y_ref, buf_ref, st_ref):
    d = pl.program_id(0)
    tc = pl.program_id(3)
    tl = ul_ref.shape[1]
    rows = tl * S5_BATCH_SUB
    ns = S5_ST_BLOCK

    @pl.when(tc == 0)
    def _():
        st_ref[...] = jnp.zeros(st_ref.shape, F32)
        u2 = uc_ref[0].reshape(rows, S5_CH_BLOCK).astype(BF16)
        buf_ref[...] = jnp.dot(u2, b_ref[0, 0], preferred_element_type=F32)

    @pl.when(tc > 0)
    def _():
        u2 = ul_ref[0].reshape(rows, S5_CH_BLOCK).astype(BF16)
        buf_ref[...] = jnp.dot(u2, b_ref[0, 0], preferred_element_type=F32)

    lam = lam_ref[0, 0]
    lr = jnp.broadcast_to(lam[0:1], (S5_BATCH_SUB, ns))
    li = jnp.broadcast_to(lam[1:2], (S5_BATCH_SUB, ns))

    def step(i, carry):
        xr, xi = carry
        t = jnp.where(d == 0, i, tl - 1 - i)
        r0 = pl.multiple_of(t * S5_BATCH_SUB, S5_BATCH_SUB)
        br = buf_ref[pl.ds(r0, S5_BATCH_SUB), 0:ns]
        bi = buf_ref[pl.ds(r0, S5_BATCH_SUB), ns:2 * ns]
        nr = lr * xr - li * xi + br
        ni = lr * xi + li * xr + bi
        buf_ref[pl.ds(r0, S5_BATCH_SUB), 0:ns] = nr
        buf_ref[pl.ds(r0, S5_BATCH_SUB), ns:2 * ns] = ni
        return nr, ni

    xr, xi = lax.fori_loop(0, tl, step, (st_ref[0], st_ref[1]), unroll=8)
    st_ref[0] = xr
    st_ref[1] = xi

    @pl.when(tc > 0)
    def _():
        y = jnp.dot(buf_ref[...].astype(BF16), c_ref[0, 0], preferred_element_type=F32)
        y_ref[0, 0] = y.reshape(tl, S5_BATCH_SUB, S5_CH_BLOCK)


def _s5_scan(u_ctx, u_lat, bblk, cblk, lam):
    nbg, l_lat, _, d = u_lat.shape
    assert u_ctx.shape[1] == S5_TL and l_lat % S5_TL == 0
    n_lc = l_lat // S5_TL
    n_sb = d // S5_CH_BLOCK

    def lat_chunk(di, tc):
        j = jnp.maximum(tc - 1, 0)
        return jnp.where(di == 0, j, n_lc - 1 - j)

    return pl.pallas_call(
        _s5_scan_body,
        grid=(2, nbg, n_sb, n_lc + 1),
        in_specs=[
            pl.BlockSpec((1, S5_TL, S5_BATCH_SUB, S5_CH_BLOCK), lambda di, bg, sb, tc: (bg, 0, 0, sb)),
            pl.BlockSpec((1, S5_TL, S5_BATCH_SUB, S5_CH_BLOCK), lambda di, bg, sb, tc: (bg, lat_chunk(di, tc), 0, sb)),
            pl.BlockSpec((1, 1, S5_CH_BLOCK, 2 * S5_ST_BLOCK), lambda di, bg, sb, tc: (di, sb, 0, 0)),
            pl.BlockSpec((1, 1, 2 * S5_ST_BLOCK, S5_CH_BLOCK), lambda di, bg, sb, tc: (di, sb, 0, 0)),
            pl.BlockSpec((1, 1, 2, S5_ST_BLOCK), lambda di, bg, sb, tc: (di, sb, 0, 0)),
        ],
        out_specs=pl.BlockSpec((1, 1, S5_TL, S5_BATCH_SUB, S5_CH_BLOCK),
                               lambda di, bg, sb, tc: (di, bg, lat_chunk(di, tc), 0, sb)),
        out_shape=jax.ShapeDtypeStruct((2, nbg, l_lat, S5_BATCH_SUB, d), F32),
        scratch_shapes=[
            pltpu.VMEM((S5_TL * S5_BATCH_SUB, 2 * S5_ST_BLOCK), F32),
            pltpu.VMEM((2, S5_BATCH_SUB, S5_ST_BLOCK), F32),
        ],
        compiler_params=pltpu.CompilerParams(
            dimension_semantics=("arbitrary", "arbitrary", "arbitrary", "arbitrary"),
            vmem_limit_bytes=V7X_VMEM_LIMIT_BYTES),
        name="s5_scan",
    )(u_ctx, u_lat, bblk, cblk, lam)


def _s5_glu_body(y0_ref, y1_ref, u_ref, h_ref, gate_ref, sh_ref, sc_ref, dsk_ref, wa_ref, wb_ref,
                 postw_ref, prew_ref, rw_ref, rb_ref, h1_ref, t_ref, idx_ref, p_ref, ym_s):
    tl = h_ref.shape[1]
    y = dsk_ref[...] * u_ref[0] + y0_ref[0, 0] + y1_ref[0, 0]
    o = (0.5 * y * (1.0 + jnp.tanh(math.sqrt(2.0 / math.pi) * (y + 0.044715 * (y * y * y))))).astype(BF16)
    ga = jnp.dot(o, wa_ref[...], preferred_element_type=F32)
    gb = jnp.dot(o, wb_ref[...], preferred_element_type=F32)
    ym_all = ga * jax.nn.sigmoid(gb)
    ncb = ym_s.shape[0]
    for cb in range(ncb):
        ym_s[cb] = ym_all[:, cb * 128:(cb + 1) * 128]
    for b in range(S5_BATCH_SUB):
        ym = jnp.concatenate([ym_s[cb, pl.ds(b, tl, stride=S5_BATCH_SUB), :] for cb in range(ncb)],
                             axis=-1)
        h1, t, idx, probs = _k_post_mix(ym, h_ref[b], gate_ref[b], postw_ref[...], prew_ref[...],
                                        sh_ref[b], sc_ref[b], rw_ref[...], rb_ref[...])
        h1_ref[b] = h1
        t_ref[b] = t.astype(BF16)
        idx_ref[b] = idx
        p_ref[b] = probs


def _s5_glu_postmix(y, u, h, mods, d_skip, wa, wb, post_w, pre_w, rw_pad, rb_pad, tl):
    b, l, d = h.shape
    nbg = b // S5_BATCH_SUB
    nsub = S5_BATCH_SUB
    row = lambda v: v.reshape(1, -1)
    full = lambda shape: pl.BlockSpec(shape, lambda gi, ti: (0,) * len(shape))
    mod = lambda k: pl.BlockSpec((nsub, 1, d), lambda gi, ti: (gi, 0, k))
    tok = lambda w: pl.BlockSpec((nsub, tl, w), lambda gi, ti: (gi, ti, 0))
    return pl.pallas_call(
        _s5_glu_body,
        grid=(nbg, l // tl),
        in_specs=[
            pl.BlockSpec((1, 1, tl * nsub, d), lambda gi, ti: (0, gi, ti, 0)),
            pl.BlockSpec((1, 1, tl * nsub, d), lambda gi, ti: (1, gi, ti, 0)),
            pl.BlockSpec((1, tl * nsub, d), lambda gi, ti: (gi, ti, 0)),
            tok(d), mod(2), mod(3), mod(4),
            full((1, d)), full((d, d)), full((d, d)), full((1, d)), full((1, d)), full((d, 128)), full((1, 128)),
        ],
        out_specs=[tok(d), tok(d), tok(128), tok(128)],
        scratch_shapes=[pltpu.VMEM((d // 128, tl * nsub, 128), F32)],
        out_shape=[
            jax.ShapeDtypeStruct((b, l, d), F32),
            jax.ShapeDtypeStruct((b, l, d), BF16),
            jax.ShapeDtypeStruct((b, l, 128), jnp.int32),
            jax.ShapeDtypeStruct((b, l, 128), F32),
        ],
        compiler_params=pltpu.CompilerParams(
            dimension_semantics=("arbitrary", "arbitrary"), vmem_limit_bytes=V7X_VMEM_LIMIT_BYTES),
        name="s5_glu_postmix",
    )(y, y, u, h, mods, mods, mods, row(d_skip), wa, wb, row(post_w), row(pre_w), rw_pad, rb_pad)


def _s5_block_params(a_re, a_im, log_step, b_re, b_im, c_re, c_im):
    gpb = S5_CH_BLOCK // S5_GROUP
    eye = jnp.eye(gpb, dtype=F32)
    bblks, cblks, lams = [], [], []
    for di in range(2):
        lam_re, lam_im, bb_re, bb_im = _s5_discretize(a_re[di], a_im[di], log_step[di], b_re[di], b_im[di])
        n_sb = lam_re.shape[0] // gpb

        def bdiag_in(bb):
            t = bb.reshape(n_sb, gpb, S5_STATE, S5_GROUP)
            return jnp.einsum('sgph,gk->sghkp', t, eye).reshape(n_sb, S5_CH_BLOCK, S5_ST_BLOCK)

        def bdiag_out(cc):
            t = cc.reshape(n_sb, gpb, S5_GROUP, S5_STATE)
            return jnp.einsum('sghp,gk->sgpkh', t, eye).reshape(n_sb, S5_ST_BLOCK, S5_CH_BLOCK)

        bblks.append(jnp.concatenate([bdiag_in(bb_re), bdiag_in(bb_im)], axis=-1))
        cblks.append(jnp.concatenate([bdiag_out(c_re[di]), -bdiag_out(c_im[di])], axis=-2))
        lams.append(jnp.stack([lam_re.reshape(n_sb, S5_ST_BLOCK), lam_im.reshape(n_sb, S5_ST_BLOCK)], axis=1))
    return jnp.stack(bblks).astype(BF16), jnp.stack(cblks).astype(BF16), jnp.stack(lams)


def _pad_router(router_w, router_b):
    e = router_w.shape[1]
    rw = jnp.pad(router_w, ((0, 0), (0, 128 - e)))
    rb = jnp.pad(router_b, (0, 128 - e), constant_values=-1e30).reshape(1, 128)
    return rw, rb


CONV_CH = GDN_CONV_CH + SSD_CONV_CH
Z_CH = GDN_V_WIDTH + SSD_INNER
IN0_TILE = 256
IN0_CONV_TILES = CONV_CH // IN0_TILE
IN0_Z_TILES = Z_CH // IN0_TILE
IN0_TILES = IN0_CONV_TILES + IN0_Z_TILES + 1
CONV_ROW_CHUNK = 256


def _conv_halo(cols):
    return ((cols + 1 + 7) // 8) * 8


def _in0_body(h_ref, sh_ref, sc_ref, nw_ref, w_ref, cw_ref, cb_ref, gp_ref,
              conv_ref, z_ref, g_ref, gt_ref, a_s, p0_s, pm_s, pp_s, *, n_rows, n_cols):
    n = pl.program_id(1)
    l = h_ref.shape[1]
    halo = _conv_halo(n_cols)
    rc = min(CONV_ROW_CHUNK, l)

    @pl.when(n == 0)
    def _():
        def norm_rows(i, carry):
            r0 = pl.multiple_of(i * rc, rc)
            a = _k_rms(h_ref[0, pl.ds(r0, rc), :], nw_ref[...]) * (1.0 + sc_ref[0]) + sh_ref[0]
            a_s[pl.ds(r0, rc), :] = a.astype(BF16)
            return carry
        lax.fori_loop(0, l // rc, norm_rows, 0)

    p = jnp.dot(a_s[...], w_ref[...], preferred_element_type=F32)

    @pl.when(n < IN0_CONV_TILES)
    def _():
        zero_halo = jnp.zeros((halo, IN0_TILE), F32)
        for s in (p0_s, pm_s, pp_s):
            s[pl.ds(0, halo), :] = zero_halo
            s[pl.ds(halo + l, halo), :] = zero_halo
        p0_s[pl.ds(halo, l), :] = p
        tcol = lax.broadcasted_iota(jnp.int32, (l, IN0_TILE), 0) % n_cols
        pm_s[pl.ds(halo, l), :] = jnp.where(tcol != 0, p0_s[pl.ds(halo - 1, l), :], 0.0)
        pp_s[pl.ds(halo, l), :] = jnp.where(tcol != n_cols - 1, p0_s[pl.ds(halo + 1, l), :], 0.0)
        dys = (0,) if n_rows == 1 else (-1, 0, 1)

        def conv_rows(i, carry):
            r0 = pl.multiple_of(i * rc, rc)
            acc = jnp.zeros((rc, IN0_TILE), F32) + cb_ref[...]
            for dy in dys:
                base = halo + dy * n_cols
                for dx, src in ((0, pm_s), (1, p0_s), (2, pp_s)):
                    tap = (dy + 1) * 3 + dx
                    acc = acc + cw_ref[tap:tap + 1, :] * src[pl.ds(r0 + base, rc), :]
            conv_ref[0, pl.ds(r0, rc), :] = (acc * jax.nn.sigmoid(acc)).astype(conv_ref.dtype)
            return carry
        lax.fori_loop(0, l // rc, conv_rows, 0)

    @pl.when((n >= IN0_CONV_TILES) & (n < IN0_CONV_TILES + IN0_Z_TILES))
    def _():
        z_ref[0] = (p * jax.nn.sigmoid(p)).astype(z_ref.dtype)

    @pl.when(n == IN0_TILES - 1)
    def _():
        pg = p[:, :128]
        lane = lax.broadcasted_iota(jnp.int32, pg.shape, 1)
        xb = pg + gp_ref[0:1, :]
        sp = jnp.maximum(xb, 0.0) + jnp.log(1.0 + jnp.exp(-jnp.abs(xb)))
        neg_a_sp = -jnp.exp(gp_ref[1:2, :]) * sp
        gates = jnp.where(lane < 2 * GDN_HEADS, neg_a_sp,
                          jnp.where(lane < GATE_COL_DT, jax.nn.sigmoid(pg),
                                    jnp.where(lane < GATE_COL_DA, sp,
                                              jnp.where(lane < GATE_COL_DA + 2 * SSD_HEADS, neg_a_sp, 0.0))))
        g_ref[0] = gates
        for ci in range(l // SCAN_C):
            gt_ref[0, ci] = gates[ci * SCAN_C:(ci + 1) * SCAN_C, :].T


def _in0_features(h, mods, norm_w, w_all, conv_w, conv_b, gate_params, n_rows, n_cols):
    b, l, d = h.shape
    nb = mods.shape[0]
    bsel = (lambda bi: bi) if nb > 1 else (lambda bi: 0)
    halo = _conv_halo(n_cols)
    nct, nzt = IN0_CONV_TILES, IN0_Z_TILES
    pad_rows = l + 2 * halo
    return pl.pallas_call(
        functools.partial(_in0_body, n_rows=n_rows, n_cols=n_cols),
        grid=(b, IN0_TILES),
        in_specs=[
            pl.BlockSpec((1, l, d), lambda bi, ni: (bi, 0, 0)),
            pl.BlockSpec((1, 1, d), lambda bi, ni: (bsel(bi), 0, 0)),
            pl.BlockSpec((1, 1, d), lambda bi, ni: (bsel(bi), 0, 1)),
            pl.BlockSpec((1, d), lambda bi, ni: (0, 0)),
            pl.BlockSpec((d, IN0_TILE), lambda bi, ni: (0, ni)),
            pl.BlockSpec((9, IN0_TILE), lambda bi, ni: (0, jnp.minimum(ni, nct - 1))),
            pl.BlockSpec((1, IN0_TILE), lambda bi, ni: (0, jnp.minimum(ni, nct - 1))),
            pl.BlockSpec((2, 128), lambda bi, ni: (0, 0)),
        ],
        out_specs=[
            pl.BlockSpec((1, l, IN0_TILE), lambda bi, ni: (bi, 0, jnp.minimum(ni, nct - 1))),
            pl.BlockSpec((1, l, IN0_TILE), lambda bi, ni: (bi, 0, jnp.clip(ni - nct, 0, nzt - 1))),
            pl.BlockSpec((1, l, 128), lambda bi, ni: (bi, 0, 0)),
            pl.BlockSpec((1, l // SCAN_C, 128, SCAN_C), lambda bi, ni: (bi, 0, 0, 0)),
        ],
        out_shape=[
            jax.ShapeDtypeStruct((b, l, CONV_CH), BF16),
            jax.ShapeDtypeStruct((b, l, Z_CH), BF16),
            jax.ShapeDtypeStruct((b, l, 128), F32),
            jax.ShapeDtypeStruct((b, l // SCAN_C, 128, SCAN_C), F32),
        ],
        scratch_shapes=[
            pltpu.VMEM((l, d), BF16),
            pltpu.VMEM((pad_rows, IN0_TILE), F32),
            pltpu.VMEM((pad_rows, IN0_TILE), F32),
            pltpu.VMEM((pad_rows, IN0_TILE), F32),
        ],
        compiler_params=pltpu.CompilerParams(
            dimension_semantics=("arbitrary", "arbitrary"), vmem_limit_bytes=V7X_VMEM_LIMIT_BYTES),
        name="in0_features",
    )(h, mods, mods, norm_w.reshape(1, d), w_all, conv_w, conv_b, gate_params)


def _in0_params(w_in, gdn_conv_w, ssd_conv_w, ssd_conv_b, gdn_a_log, gdn_dt_bias, ssd_a_log, ssd_dt_bias):
    splits = np.cumsum(IN0_SIZES)[:-1].tolist()
    w_ca, w_cb, w_za, w_zb, w_a, w_b, w_dt = jnp.split(w_in, splits, axis=1)
    d = w_in.shape[0]
    w_gate = jnp.concatenate([w_a, w_b, w_dt, w_dt], axis=1)
    w_gate = jnp.pad(w_gate, ((0, 0), (0, IN0_TILE - w_gate.shape[1])))
    w_all = jnp.concatenate([w_ca, w_cb, w_za, w_zb, w_gate], axis=1).astype(BF16)
    conv_w = jnp.concatenate([gdn_conv_w, ssd_conv_w], axis=-1).reshape(9, CONV_CH)
    conv_b = jnp.concatenate([jnp.zeros((GDN_CONV_CH,), F32), ssd_conv_b]).reshape(1, CONV_CH)
    z8 = jnp.zeros((2 * GDN_HEADS,), F32)
    z16 = jnp.zeros((2 * SSD_HEADS,), F32)
    tail = jnp.zeros((128 - GATE_COL_DA - 2 * SSD_HEADS,), F32)
    bias = jnp.concatenate([gdn_dt_bias.reshape(-1), z8, ssd_dt_bias.reshape(-1), ssd_dt_bias.reshape(-1), tail])
    alog = jnp.concatenate([gdn_a_log.reshape(-1), z8, z16, ssd_a_log.reshape(-1), tail])
    return w_all, conv_w, conv_b, jnp.stack([bias, alog])


def _merge0_body(o_ref, y_ref, xs_ref, z_ref, h_ref, gate_ref, sh_ref, sc_ref, gnw_ref, dsk_ref, snw_ref, wo_ref,
                 postw_ref, prew_ref, rw_ref, rb_ref, h1_ref, t_ref, idx_ref, p_ref):
    z = z_ref[0].astype(F32)
    parts = []
    for hd in range(GDN_HEADS):
        cs = slice(hd * GDN_DV, (hd + 1) * GDN_DV)
        parts.append(_k_rms(o_ref[0, :, cs], gnw_ref[...]) * z[:, cs])
    gw = SSD_INNER // SSD_GROUPS
    for g in range(SSD_GROUPS):
        cs = slice(g * gw, (g + 1) * gw)
        y2 = (y_ref[0, :, cs] + dsk_ref[:, cs] * xs_ref[0, :, cs].astype(F32)) * z[:, GDN_V_WIDTH + g * gw:GDN_V_WIDTH + (g + 1) * gw]
        parts.append(_k_rms(y2, snw_ref[:, cs]))
    mixed = jnp.concatenate(parts, axis=-1).astype(BF16)
    ym = jnp.dot(mixed, wo_ref[...], preferred_element_type=F32)
    h1, t, idx, probs = _k_post_mix(ym, h_ref[0], gate_ref[0], postw_ref[...], prew_ref[...],
                                    sh_ref[0], sc_ref[0], rw_ref[...], rb_ref[...])
    h1_ref[0] = h1
    t_ref[0] = t.astype(BF16)
    idx_ref[0] = idx
    p_ref[0] = probs


def _merge0_postmix(o, y, conv, z, h, mods, gdn_norm_w, ssd_d, ssd_norm_w, w_out, post_w, pre_w, rw_pad, rb_pad, tl):
    b, l, d = h.shape
    nb = mods.shape[0]
    bsel = (lambda bi: bi) if nb > 1 else (lambda bi: 0)
    row = lambda v: v.reshape(1, -1)
    full = lambda shape: pl.BlockSpec(shape, lambda bi, ti: (0,) * len(shape))
    mod = lambda k: pl.BlockSpec((1, 1, d), lambda bi, ti: (bsel(bi), 0, k))
    tok = lambda w: pl.BlockSpec((1, tl, w), lambda bi, ti: (bi, ti, 0))
    xs_block = GDN_CONV_CH // SSD_INNER
    dsk = jnp.repeat(ssd_d, SSD_HEADDIM)
    return pl.pallas_call(
        _merge0_body,
        grid=(b, l // tl),
        in_specs=[
            tok(GDN_V_WIDTH), tok(SSD_INNER),
            pl.BlockSpec((1, tl, SSD_INNER), lambda bi, ti: (bi, ti, xs_block)),
            tok(Z_CH), tok(d), mod(2), mod(3), mod(4),
            full((1, GDN_DV)), full((1, SSD_INNER)), full((1, SSD_INNER)), full((Z_CH, d)),
            full((1, d)), full((1, d)), full((d, 128)), full((1, 128)),
        ],
        out_specs=[tok(d), tok(d), tok(128), tok(128)],
        out_shape=[
            jax.ShapeDtypeStruct((b, l, d), F32),
            jax.ShapeDtypeStruct((b, l, d), BF16),
            jax.ShapeDtypeStruct((b, l, 128), jnp.int32),
            jax.ShapeDtypeStruct((b, l, 128), F32),
        ],
        compiler_params=pltpu.CompilerParams(
            dimension_semantics=("arbitrary", "arbitrary"), vmem_limit_bytes=V7X_VMEM_LIMIT_BYTES),
        name="merge0_postmix",
    )(o, y, conv, z, h, mods, mods, mods, row(gdn_norm_w), row(dsk), row(ssd_norm_w), w_out,
      row(post_w), row(pre_w), rw_pad, rb_pad)


SCAN_C = 128
GDN_INV_BLOCK = 16


def _dot(a, b):
    return jnp.dot(a, b, preferred_element_type=F32)


def _dot_nt(a, b):
    return lax.dot_general(a, b, (((1,), (1,)), ((), ())), preferred_element_type=F32)


def _dot_tn(a, b):
    return lax.dot_general(a, b, (((0,), (0,)), ((), ())), preferred_element_type=F32)


def _bdot(a, b):
    return _dot(a.astype(BF16), b.astype(BF16))


def _bdot_nt(a, b):
    return _dot_nt(a.astype(BF16), b.astype(BF16))


def _bdot_tn(a, b):
    return _dot_tn(a.astype(BF16), b.astype(BF16))


def _scan_masks(fwd):
    row = lax.broadcasted_iota(jnp.int32, (SCAN_C, SCAN_C), 0)
    col = lax.broadcasted_iota(jnp.int32, (SCAN_C, SCAN_C), 1)
    lead = (row - col) * jnp.where(fwd, 1, -1)
    return row, col, lead >= 0, lead <= 0, lead > 0


def _cumsum_col_row(g_col, g_row, incl, incl_t):
    gc_col = jnp.sum(jnp.where(incl, g_row, 0.0), axis=1, keepdims=True)
    gc_row = jnp.sum(jnp.where(incl_t, g_col, 0.0), axis=0, keepdims=True)
    return gc_col, gc_row


def _unit_tri_inverse(ms, row, col):
    eye = (row == col).astype(F32)
    same = (row // GDN_INV_BLOCK) == (col // GDN_INV_BLOCK)
    mds = [jnp.where(same, m, 0.0) for m in ms]
    mos = [m - md for m, md in zip(ms, mds)]
    ps = mds
    tds = [eye - md for md in mds]
    for _ in range(int(math.log2(GDN_INV_BLOCK)) - 1):
        ps = [_dot(p, p) for p in ps]
        tds = [td + _dot(td, p) for td, p in zip(tds, ps)]
    ps = [_dot(td, mo) for td, mo in zip(tds, mos)]
    qs = [eye - n for n in ps]
    for _ in range(int(math.log2(SCAN_C // GDN_INV_BLOCK)) - 1):
        ps = [_dot(p, p) for p in ps]
        qs = [q + _dot(q, p) for q, p in zip(qs, ps)]
    return [_dot(q, td) for q, td in zip(qs, tds)]


GDN_PREP_GROUP = 8


def _gdn_body(ql_ref, kl_ref, vl_ref, gl_ref, gtl_ref, qc_ref, kc_ref, vc_ref, gc_ref, gtc_ref,
              ol_ref, oc_ref, u_s, w_s, qk_s, qd_s, kd_s, dec_s, obl_s, obc_s):
    h = pl.program_id(1)
    c = SCAN_C
    ncc = qc_ref.shape[1] // c
    ncl = ql_ref.shape[1] // c
    grp = math.gcd(ncl, GDN_PREP_GROUP)
    lane = lax.broadcasted_iota(jnp.int32, (c, 128), 1)

    def order(d, i, n):
        return i if d == 0 else n - 1 - i

    def prep(d, q_ref, k_ref, v_ref, g_ref, gt_ref, cis, ps):
        row, col, incl, incl_t, strict = _scan_masks(d == 0)
        col_g = d * GDN_HEADS + h
        col_b = 2 * GDN_HEADS + col_g
        loaded = []
        for ci in cis:
            t0 = pl.multiple_of(ci * c, c)
            loaded.append((q_ref[0, pl.ds(t0, c), :].astype(F32), k_ref[0, pl.ds(t0, c), :].astype(F32),
                           v_ref[0, pl.ds(t0, c), :].astype(F32), g_ref[0, pl.ds(t0, c), :],
                           gt_ref[0, ci, pl.ds(col_g, 1), :]))
        parts, ms = [], []
        for q, k, v, gch, g_row in loaded:
            q = q * lax.rsqrt(jnp.sum(q * q, axis=-1, keepdims=True) + RMS_EPS) * (GDN_DK ** -0.5)
            k = k * lax.rsqrt(jnp.sum(k * k, axis=-1, keepdims=True) + RMS_EPS)
            g_col = jnp.sum(jnp.where(lane == col_g, gch, 0.0), axis=1, keepdims=True)
            b_col = jnp.sum(jnp.where(lane == col_b, gch, 0.0), axis=1, keepdims=True)
            gc_col, gc_row = _cumsum_col_row(g_col, g_row, incl, incl_t)
            g_tot = jnp.sum(g_col, axis=0, keepdims=True)
            decay = jnp.where(incl, jnp.exp(jnp.where(incl, gc_col - gc_row, 0.0)), 0.0)
            kb = k * b_col
            ms.append(jnp.where(strict, _bdot_nt(kb, k) * decay, 0.0))
            egc = jnp.exp(gc_col)
            parts.append((v * b_col, kb * egc, _bdot_nt(q, k) * decay, q * egc, k * jnp.exp(g_tot - gc_col),
                          jnp.broadcast_to(jnp.exp(g_tot), (1, 128))))
        t_invs = _unit_tri_inverse(ms, row, col)
        uw = [(_bdot(t_inv, part[0]), _bdot(t_inv, part[1])) for t_inv, part in zip(t_invs, parts)]
        for p, (u, w), (_, _, qk, qd, kd, dec) in zip(ps, uw, parts):
            p0 = pl.multiple_of(p * c, c)
            u_s[d, pl.ds(p0, c), :] = u
            w_s[d, pl.ds(p0, c), :] = w
            qk_s[d, pl.ds(p0, c), :] = qk
            qd_s[d, pl.ds(p0, c), :] = qd
            kd_s[d, pl.ds(p0, c), :] = kd
            dec_s[d, pl.ds(p, 1), :] = dec

    def advance(of_ref, ob_ref, i, n, p, states):
        p0 = pl.multiple_of(p * c, c)
        outs, new_states = [], []
        for d, s in enumerate(states):
            v_new = u_s[d, pl.ds(p0, c), :] - _bdot(w_s[d, pl.ds(p0, c), :], s)
            outs.append(_bdot(qd_s[d, pl.ds(p0, c), :], s) + _bdot(qk_s[d, pl.ds(p0, c), :], v_new))
            new_states.append(s * dec_s[d, pl.ds(p, 1), :] + _bdot_tn(kd_s[d, pl.ds(p0, c), :], v_new))
        of_ref[0, pl.ds(pl.multiple_of(order(0, i, n) * c, c), c), :] = outs[0]
        ob_ref[pl.ds(pl.multiple_of(order(1, i, n) * c, c), c), :] = outs[1]
        return tuple(new_states)

    for d in range(2):
        prep(d, qc_ref, kc_ref, vc_ref, gc_ref, gtc_ref, [order(d, i, ncc) for i in range(ncc)], list(range(ncc)))

        def prep_lat(gi, carry, d=d):
            base = gi * grp
            prep(d, ql_ref, kl_ref, vl_ref, gl_ref, gtl_ref,
                 [order(d, base + j, ncl) for j in range(grp)], [ncc + base + j for j in range(grp)])
            return carry

        lax.fori_loop(0, ncl // grp, prep_lat, 0)

    states = (jnp.zeros((GDN_DK, GDN_DV), F32), jnp.zeros((GDN_DK, GDN_DV), F32))
    for i in range(ncc):
        states = advance(oc_ref, obc_s, i, ncc, i, states)
    lax.fori_loop(0, ncl, lambda i, st: advance(ol_ref, obl_s, i, ncl, ncc + i, st), states)
    ol_ref[0] = ol_ref[0] + obl_s[...]
    oc_ref[0] = oc_ref[0] + obc_s[...]


SSD_R = SSD_HEADS // SSD_GROUPS
SSD_PAIRS = SSD_R * SSD_HEADDIM // 128
GATE_COL_DT = 4 * GDN_HEADS
GATE_COL_DA = GATE_COL_DT + 2 * SSD_HEADS


def _ssd_body(xl_ref, bl_ref, cl_ref, gl_ref, gtl_ref, xc_ref, bc_ref, cc_ref, gc_ref, gtc_ref,
              yl_ref, yc_ref, st_ref):
    g = pl.program_id(1)
    d = pl.program_id(2)
    fwd = d == 0
    c = SCAN_C
    ncc = xc_ref.shape[1] // c
    ncl = xl_ref.shape[1] // c
    _, _, incl, incl_t, _ = _scan_masks(fwd)
    lane = lax.broadcasted_iota(jnp.int32, (c, 128), 1)
    low = lane < SSD_HEADDIM

    @pl.when(fwd)
    def _():
        yl_ref[...] = jnp.zeros(yl_ref.shape, F32)
        yc_ref[...] = jnp.zeros(yc_ref.shape, F32)

    st_ref[...] = jnp.zeros(st_ref.shape, F32)

    def chunk(x_ref, b_ref, c_ref, g_ref, gt_ref, y_ref, ci):
        t0 = pl.multiple_of(ci * c, c)
        bm = b_ref[0, pl.ds(t0, c), :].astype(BF16)
        cm = c_ref[0, pl.ds(t0, c), :].astype(BF16)
        gch = g_ref[0, pl.ds(t0, c), :]
        cb = _dot_nt(cm, bm)
        for pr in range(SSD_PAIRS):
            per_head = []
            for s in range(2):
                hh = g * SSD_R + 2 * pr + s
                col_dt = GATE_COL_DT + d * SSD_HEADS + hh
                col_da = GATE_COL_DA + d * SSD_HEADS + hh
                dt_col = jnp.sum(jnp.where(lane == col_dt, gch, 0.0), axis=1, keepdims=True)
                da_col = jnp.sum(jnp.where(lane == col_da, gch, 0.0), axis=1, keepdims=True)
                da_row = gt_ref[0, ci, pl.ds(col_da, 1), :]
                acs_col, acs_row = _cumsum_col_row(da_col, da_row, incl, incl_t)
                a_tot = jnp.sum(da_col, axis=0, keepdims=True)
                lmat = jnp.where(incl, jnp.exp(jnp.where(incl, acs_col - acs_row, 0.0)), 0.0)
                per_head.append((dt_col, acs_col, a_tot, (cb * lmat).astype(BF16)))
            pick = lambda k: jnp.where(low, per_head[0][k], per_head[1][k])
            x = x_ref[0, pl.ds(t0, c), pr * 128:(pr + 1) * 128].astype(F32)
            xdt = x * pick(0)
            acs = pick(1)
            a_tot = pick(2)
            xdt_b = xdt.astype(BF16)
            y_diag = jnp.where(low, _dot(per_head[0][3], xdt_b), _dot(per_head[1][3], xdt_b))
            st = st_ref[pr]
            y_off = _dot(cm, st.astype(BF16)) * jnp.exp(acs)
            y_ref[0, pl.ds(t0, c), pr * 128:(pr + 1) * 128] += y_diag + y_off
            st_ref[pr] = st * jnp.exp(a_tot) + _dot_tn(bm, (xdt * jnp.exp(a_tot - acs)).astype(BF16))

    def order(i, n):
        return jnp.where(fwd, i, n - 1 - i)

    for i in range(ncc):
        chunk(xc_ref, bc_ref, cc_ref, gc_ref, gtc_ref, yc_ref, order(i, ncc))

    def lat(i, carry):
        chunk(xl_ref, bl_ref, cl_ref, gl_ref, gtl_ref, yl_ref, order(i, ncl))
        return carry

    lax.fori_loop(0, ncl, lat, 0)


def _ssd_scan(xbc_l, g_l, gt_l, xbc_c, g_c, gt_c, col0=0):
    b, l, _ = xbc_l.shape
    lc = xbc_c.shape[1]
    gw = SSD_R * SSD_HEADDIM
    x0 = col0 // gw
    nxb = (col0 + SSD_INNER) // 128

    def stream(n):
        return [pl.BlockSpec((1, n, gw), lambda bi, gi, di: (bi, 0, x0 + gi)),
                pl.BlockSpec((1, n, 128), lambda bi, gi, di: (bi, 0, nxb + gi)),
                pl.BlockSpec((1, n, 128), lambda bi, gi, di: (bi, 0, nxb + SSD_GROUPS + gi)),
                pl.BlockSpec((1, n, 128), lambda bi, gi, di: (bi, 0, 0)),
                pl.BlockSpec((1, n // SCAN_C, 128, SCAN_C), lambda bi, gi, di: (bi, 0, 0, 0))]

    return pl.pallas_call(
        _ssd_body,
        grid=(b, SSD_GROUPS, 2),
        in_specs=stream(l) + stream(lc),
        out_specs=[pl.BlockSpec((1, l, gw), lambda bi, gi, di: (bi, 0, gi)),
                   pl.BlockSpec((1, lc, gw), lambda bi, gi, di: (bi, 0, gi))],
        out_shape=[jax.ShapeDtypeStruct((b, l, SSD_INNER), F32),
                   jax.ShapeDtypeStruct((b, lc, SSD_INNER), F32)],
        scratch_shapes=[pltpu.VMEM((SSD_PAIRS, SSD_STATE, 128), F32)],
        compiler_params=pltpu.CompilerParams(
            dimension_semantics=("arbitrary", "arbitrary", "arbitrary"), vmem_limit_bytes=V7X_VMEM_LIMIT_BYTES),
        name="ssd_scan",
    )(xbc_l, xbc_l, xbc_l, g_l, gt_l, xbc_c, xbc_c, xbc_c, g_c, gt_c)


def _chunk_rows(gt):
    b, w, l = gt.shape
    return gt.reshape(b, w, l // SCAN_C, SCAN_C).transpose(0, 2, 1, 3)


def _gdn_scan(qkv_l, g_l, gt_l, qkv_c, g_c, gt_c):
    b, l, _ = qkv_l.shape
    lc = qkv_c.shape[1]
    hd = GDN_HEADS

    def stream(n):
        tok = lambda off: pl.BlockSpec((1, n, 128), lambda bi, hi: (bi, 0, off + hi))
        return [tok(0), tok(hd), tok(2 * hd),
                pl.BlockSpec((1, n, 128), lambda bi, hi: (bi, 0, 0)),
                pl.BlockSpec((1, n // SCAN_C, 128, SCAN_C), lambda bi, hi: (bi, 0, 0, 0))]

    nt = l + lc
    return pl.pallas_call(
        _gdn_body,
        grid=(b, hd),
        in_specs=stream(l) + stream(lc),
        out_specs=[pl.BlockSpec((1, l, 128), lambda bi, hi: (bi, 0, hi)),
                   pl.BlockSpec((1, lc, 128), lambda bi, hi: (bi, 0, hi))],
        out_shape=[jax.ShapeDtypeStruct((b, l, hd * GDN_DV), F32),
                   jax.ShapeDtypeStruct((b, lc, hd * GDN_DV), F32)],
        scratch_shapes=[pltpu.VMEM((2, nt, 128), F32) for _ in range(5)]
        + [pltpu.VMEM((2, nt // SCAN_C, 128), F32), pltpu.VMEM((l, 128), F32), pltpu.VMEM((lc, 128), F32)],
        compiler_params=pltpu.CompilerParams(
            dimension_semantics=("arbitrary", "arbitrary"), vmem_limit_bytes=V7X_VMEM_LIMIT_BYTES),
        name="gdn_scan",
    )(qkv_l, qkv_l, qkv_l, g_l, gt_l, qkv_c, qkv_c, qkv_c, g_c, gt_c)


def _ada_body(c_ref, w_ref, b_ref, o_ref):
    cc = c_ref[...]
    o_ref[0] = jnp.dot(cc * jax.nn.sigmoid(cc), w_ref[0], preferred_element_type=F32) + b_ref[0]


def _ada_mods(cond, ada_w, ada_b):
    r, d = cond.shape
    depth, _, w6 = ada_w.shape
    rp = ((r + 7) // 8) * 8
    cond = jnp.pad(cond, ((0, rp - r), (0, 0)))
    out = pl.pallas_call(
        _ada_body,
        grid=(depth, w6 // d),
        in_specs=[pl.BlockSpec((rp, d), lambda li, ni: (0, 0)),
                  pl.BlockSpec((1, d, d), lambda li, ni: (li, 0, ni)),
                  pl.BlockSpec((1, 1, d), lambda li, ni: (li, 0, ni))],
        out_specs=pl.BlockSpec((1, rp, d), lambda li, ni: (li, 0, ni)),
        out_shape=jax.ShapeDtypeStruct((depth, rp, w6), F32),
        compiler_params=pltpu.CompilerParams(dimension_semantics=("arbitrary", "arbitrary")),
        name="ada_mods",
    )(cond, ada_w, ada_b.reshape(depth, 1, w6))
    return out[:, :r]


def _s5_discretize(a_re, a_im, log_step, b_re, b_im):
    step = jnp.exp(log_step)[:, None]
    mag = jnp.exp(a_re * step)
    lam_re, lam_im = mag * jnp.cos(a_im * step), mag * jnp.sin(a_im * step)
    den = a_re * a_re + a_im * a_im
    f_re = ((lam_re - 1.0) * a_re + lam_im * a_im) / den
    f_im = (lam_im * a_re - (lam_re - 1.0) * a_im) / den
    bb_re = f_re[..., None] * b_re - f_im[..., None] * b_im
    bb_im = f_re[..., None] * b_im + f_im[..., None] * b_re
    return lam_re, lam_im, bb_re, bb_im


def kernel(x, c, ctx, c_ctx, ada_w, ada_b, mix_norm_pre, mix_norm_post, ffn_norm_pre, ffn_norm_post, router_w, router_b, moe_w_gate, moe_b_gate, moe_w_up, moe_b_up, moe_w_down, moe_b_down, hy_w_in, gdn_conv_w, gdn_a_log, gdn_dt_bias, gdn_norm_w, ssd_conv_w, ssd_conv_b, ssd_a_log, ssd_dt_bias, ssd_d, ssd_norm_w, hy_w_out, s5_w_in, s5_a_re, s5_a_im, s5_log_step, s5_b_re, s5_b_im, s5_c_re, s5_c_im, s5_d, s5_w_glu_a, s5_w_glu_b):
    depth = ada_w.shape[0]
    rows = x.shape[1] // GRID_W
    h_lat, h_ctx = x, ctx
    nb = c.shape[0]
    mods_all = _ada_mods(jnp.concatenate([c, c_ctx[None, :]], axis=0), ada_w, ada_b)
    for i in range(depth):
        j = i // 2
        need_ctx = i < depth - 1
        mods_lat = mods_all[i, :nb, None, :]
        mods_ctx = mods_all[i, nb:nb + 1, None, :]
        ffn = (i, moe_w_gate, moe_b_gate, moe_w_up, moe_b_up, moe_w_down, moe_b_down, ffn_norm_post[i])
        rw_pad, rb_pad = _pad_router(router_w[i], router_b[i])
        bsz, l, d = h_lat.shape
        lc = h_ctx.shape[1]
        tl_lat, tl_ctx = 512, lc
        if i % 2 == 1:
            assert not need_ctx
            nbg = bsz // S5_BATCH_SUB
            w_in = s5_w_in[j].astype(BF16)
            u_lat = _s5_inproj(h_lat, mods_lat, mix_norm_pre[i], w_in, S5_TL)
            u_ctx = _s5_inproj(h_ctx, mods_ctx, mix_norm_pre[i], w_in, S5_TL)
            bblk, cblk, lam = _s5_block_params(s5_a_re[j], s5_a_im[j], s5_log_step[j], s5_b_re[j], s5_b_im[j],
                                               s5_c_re[j], s5_c_im[j])
            y = _s5_scan(u_ctx.reshape(nbg, lc, S5_BATCH_SUB, d), u_lat.reshape(nbg, l, S5_BATCH_SUB, d),
                         bblk, cblk, lam)
            h1, t_lat, idx, probs = _s5_glu_postmix(
                y.reshape(2, nbg, l * S5_BATCH_SUB, d), u_lat, h_lat, mods_lat, s5_d[j],
                s5_w_glu_a[j].astype(BF16), s5_w_glu_b[j].astype(BF16),
                mix_norm_post[i], ffn_norm_pre[i], rw_pad, rb_pad, S5_GLU_TL)
            (h_lat,) = _moe_layer([(t_lat, idx, probs, h1, mods_lat, tl_lat)], *ffn)
            continue
        w_all, conv_w, conv_b, gate_params = _in0_params(hy_w_in[j], gdn_conv_w[j], ssd_conv_w[j], ssd_conv_b[j],
                                                         gdn_a_log[j], gdn_dt_bias[j], ssd_a_log[j], ssd_dt_bias[j])
        conv_l, z_l, g_l, gt_l = _in0_features(h_lat, mods_lat, mix_norm_pre[i], w_all, conv_w, conv_b, gate_params,
                                               rows, GRID_W)
        conv_c, z_c, g_c, gt_c = _in0_features(h_ctx, mods_ctx, mix_norm_pre[i], w_all, conv_w, conv_b, gate_params,
                                               1, lc)
        o_l, o_c = _gdn_scan(conv_l, g_l, gt_l, conv_c, g_c, gt_c)
        y_l, y_c = _ssd_scan(conv_l, g_l, gt_l, conv_c, g_c, gt_c, col0=GDN_CONV_CH)
        w_out = hy_w_out[j].astype(BF16)
        merge = lambda o, y, conv, z, h, mods, tl: _merge0_postmix(
            o, y, conv, z, h, mods, gdn_norm_w[j], ssd_d[j], ssd_norm_w[j], w_out,
            mix_norm_post[i], ffn_norm_pre[i], rw_pad, rb_pad, tl)
        streams = [merge(o_l, y_l, conv_l, z_l, h_lat, mods_lat, tl_lat) + (mods_lat, tl_lat)]
        if need_ctx:
            streams.append(merge(o_c, y_c, conv_c, z_c, h_ctx, mods_ctx, tl_ctx) + (mods_ctx, tl_ctx))
        outs = _moe_layer([(t, idx, probs, h1, mods, tl) for h1, t, idx, probs, mods, tl in streams], *ffn)
        h_lat = outs[0]
        if need_ctx:
            h_ctx = outs[1]
    return h_lat
```

```python
import functools
import math

import jax
import jax.numpy as jnp
import numpy as np
from jax import lax
from jax.experimental import pallas as pl
from jax.experimental.pallas import tpu as pltpu

F32 = jnp.float32
BF16 = jnp.bfloat16

D_MODEL = 1024
GRID_W = 64
RMS_EPS = 1e-6

GDN_HEADS = 4
GDN_DK = 128
GDN_DV = 128
GDN_CHUNK = 64
SSD_HEADS = 8
SSD_HEADDIM = 64
SSD_GROUPS = 2
SSD_STATE = 128
SSD_CHUNK = 128
S5_GROUP = 16
S5_GROUPS = D_MODEL // S5_GROUP
S5_STATE = 64
N_EXPERTS = 32
TOP_K = 4
SWIGLU_LIMIT = 7.0
SWIGLU_ALPHA = 1.702

GDN_V_WIDTH = GDN_HEADS * GDN_DV
SSD_INNER = SSD_HEADS * SSD_HEADDIM
GDN_CONV_CH = 2 * GDN_HEADS * GDN_DK + GDN_V_WIDTH
SSD_CONV_CH = SSD_INNER + 2 * SSD_GROUPS * SSD_STATE
IN0_SIZES = (GDN_CONV_CH, SSD_CONV_CH, GDN_V_WIDTH, SSD_INNER, 2 * GDN_HEADS, 2 * GDN_HEADS, 2 * SSD_HEADS)

V7X_VMEM_LIMIT_BYTES = 56 * 1024 * 1024
MOE_TILE_M = 512
MOE_TILE_F = 512
MOE_BATCH_SLICES = 2


def _moe_ffn_body(tile_e_ref, tile_ok_ref, x_ref, wg_ref, bg_ref, wu_ref, bu_ref, wd_ref, bd_ref, o_ref,
                  wg_s, wu_s, wd_s):
    i = pl.program_id(0)
    n_f = wg_ref.shape[3] // MOE_TILE_F

    @pl.when((i == 0) | (tile_e_ref[i] != tile_e_ref[jnp.maximum(i - 1, 0)]))
    def _():
        for c in range(n_f):
            cs = slice(c * MOE_TILE_F, (c + 1) * MOE_TILE_F)
            wg_s[:, cs] = wg_ref[0, 0, :, cs].astype(BF16)
            wu_s[:, cs] = wu_ref[0, 0, :, cs].astype(BF16)
            wd_s[cs, :] = wd_ref[0, 0, cs, :].astype(BF16)

    @pl.when(tile_ok_ref[i] > 0)
    def _():
        x = x_ref[...]
        acc = jnp.zeros(o_ref.shape, F32)
        for c in range(n_f):
            cs = slice(c * MOE_TILE_F, (c + 1) * MOE_TILE_F)
            gl = jnp.dot(x, wg_s[:, cs], preferred_element_type=F32) + bg_ref[0, 0, :, cs]
            lin = jnp.dot(x, wu_s[:, cs], preferred_element_type=F32) + bu_ref[0, 0, :, cs]
            gl = jnp.minimum(gl, SWIGLU_LIMIT)
            lin = jnp.clip(lin, -SWIGLU_LIMIT, SWIGLU_LIMIT)
            act = gl * jax.nn.sigmoid(SWIGLU_ALPHA * gl) * (lin + 1.0)
            acc = acc + jnp.dot(act.astype(BF16), wd_s[cs, :], preferred_element_type=F32)
        o_ref[...] = (acc + bd_ref[0, 0]).astype(o_ref.dtype)

    @pl.when(tile_ok_ref[i] == 0)
    def _():
        o_ref[...] = jnp.zeros(o_ref.shape, o_ref.dtype)


def _moe_grouped_ffn(xs, tile_e, tile_ok, layer, wg, bg, wu, bu, wd, bd):
    p, d = xs.shape
    nl, e, _, f = wg.shape
    n_tiles = p // MOE_TILE_M
    grid_spec = pltpu.PrefetchScalarGridSpec(
        num_scalar_prefetch=2,
        grid=(n_tiles,),
        in_specs=[
            pl.BlockSpec((MOE_TILE_M, d), lambda i, te, ok: (i, 0)),
            pl.BlockSpec((1, 1, d, f), lambda i, te, ok: (layer, te[i], 0, 0)),
            pl.BlockSpec((1, 1, 1, f), lambda i, te, ok: (layer, te[i], 0, 0)),
            pl.BlockSpec((1, 1, d, f), lambda i, te, ok: (layer, te[i], 0, 0)),
            pl.BlockSpec((1, 1, 1, f), lambda i, te, ok: (layer, te[i], 0, 0)),
            pl.BlockSpec((1, 1, f, d), lambda i, te, ok: (layer, te[i], 0, 0)),
            pl.BlockSpec((1, 1, 1, d), lambda i, te, ok: (layer, te[i], 0, 0)),
        ],
        out_specs=pl.BlockSpec((MOE_TILE_M, d), lambda i, te, ok: (i, 0)),
        scratch_shapes=[pltpu.VMEM((d, f), BF16), pltpu.VMEM((d, f), BF16), pltpu.VMEM((f, d), BF16)],
    )
    return pl.pallas_call(
        _moe_ffn_body,
        grid_spec=grid_spec,
        out_shape=jax.ShapeDtypeStruct((p, d), BF16),
        compiler_params=pltpu.CompilerParams(
            dimension_semantics=("arbitrary",), vmem_limit_bytes=V7X_VMEM_LIMIT_BYTES),
        name="moe_grouped_ffn",
    )(tile_e, tile_ok, xs, wg, bg.reshape(nl, e, 1, f), wu, bu.reshape(nl, e, 1, f), wd, bd.reshape(nl, e, 1, d))


def _moe_combine_body(y0_ref, y1_ref, y2_ref, y3_ref, p_ref, h_ref, gate_ref, nw_ref, *rest):
    o_ref = rest[-1]
    p = p_ref[0]
    acc = jnp.zeros(o_ref.shape[1:], F32)
    for k, y_ref in enumerate((y0_ref, y1_ref, y2_ref, y3_ref)):
        acc = acc + p[:, k:k + 1] * y_ref[0, 0].astype(F32)
    o_ref[0] = h_ref[0] + gate_ref[0] * _k_rms(acc, nw_ref[...])


def _moe_combine(yg, probs, h, mods, norm_w, tl, b0, prev):
    b, l, d = h.shape
    nb = yg.shape[1]
    bsel = (lambda bi: bi + b0) if mods.shape[0] > 1 else (lambda bi: 0)
    yk = lambda k: pl.BlockSpec((1, 1, tl, d), lambda bi, ti: (k, bi, ti, 0))
    in_specs = [yk(0), yk(1), yk(2), yk(3),
                pl.BlockSpec((1, tl, 128), lambda bi, ti: (bi + b0, ti, 0)),
                pl.BlockSpec((1, tl, d), lambda bi, ti: (bi + b0, ti, 0)),
                pl.BlockSpec((1, 1, d), lambda bi, ti: (bsel(bi), 0, 5)),
                pl.BlockSpec((1, d), lambda bi, ti: (0, 0))]
    args = [yg, yg, yg, yg, probs, h, mods, norm_w.reshape(1, -1)]
    aliases = {}
    if prev is not None:
        in_specs.append(pl.BlockSpec(memory_space=pl.ANY))
        args.append(prev)
        aliases = {len(args) - 1: 0}
    return pl.pallas_call(
        _moe_combine_body,
        grid=(nb, l // tl),
        in_specs=in_specs,
        out_specs=pl.BlockSpec((1, tl, d), lambda bi, ti: (bi + b0, ti, 0)),
        out_shape=jax.ShapeDtypeStruct((b, l, d), F32),
        input_output_aliases=aliases,
        compiler_params=pltpu.CompilerParams(
            dimension_semantics=("arbitrary", "arbitrary"), vmem_limit_bytes=V7X_VMEM_LIMIT_BYTES),
        name="moe_combine",
    )(*args)


def _moe_route(top_idx):
    t = top_idx.shape[0]
    a = t * TOP_K
    tm = MOE_TILE_M
    eid = top_idx.reshape(a)
    order = jnp.argsort(eid, stable=True).astype(jnp.int32)
    inv = jnp.argsort(order).astype(jnp.int32)
    counts = jnp.sum(jax.nn.one_hot(eid, N_EXPERTS, dtype=jnp.int32), axis=0)
    off = jnp.cumsum(counts) - counts
    pcounts = ((counts + tm - 1) // tm) * tm
    pend = jnp.cumsum(pcounts)
    poff = pend - pcounts
    n_tiles = a // tm + N_EXPERTS
    tile_start = jnp.arange(n_tiles, dtype=jnp.int32) * tm
    n_done = jnp.sum((tile_start[:, None] >= pend[None, :]).astype(jnp.int32), axis=1)
    tile_e = jnp.minimum(n_done, N_EXPERTS - 1)
    tile_ok = (tile_start < pend[-1]).astype(jnp.int32)
    ppos = jnp.arange(n_tiles * tm, dtype=jnp.int32)
    pe = jnp.repeat(tile_e, tm)
    r = ppos - poff[pe]
    src_rank = jnp.clip(off[pe] + jnp.minimum(r, counts[pe] - 1), 0, a - 1)
    src_tok = order[src_rank] // TOP_K
    pos = poff[eid] + (inv - off[eid])
    return src_tok, pos, tile_e, tile_ok


def _moe_layer(streams, layer, wg, bg, wu, bu, wd, bd, norm_w):
    d = streams[0][0].shape[-1]
    outs = [None] * len(streams)
    for mb in range(MOE_BATCH_SLICES):
        cuts = [(mb * (s[0].shape[0] // MOE_BATCH_SLICES), s[0].shape[0] // MOE_BATCH_SLICES) for s in streams]
        t_all = jnp.concatenate([s[0][b0:b0 + nb].reshape(-1, d) for s, (b0, nb) in zip(streams, cuts)], axis=0)
        idx_all = jnp.concatenate([s[1][b0:b0 + nb].reshape(-1, 128)[:, :TOP_K]
                                   for s, (b0, nb) in zip(streams, cuts)], axis=0)
        src_tok, pos, tile_e, tile_ok = _moe_route(idx_all)
        xs = t_all.at[src_tok].get(mode="promise_in_bounds")
        ys = _moe_grouped_ffn(xs, tile_e, tile_ok, layer, wg, bg, wu, bu, wd, bd)
        pos_k = pos.reshape(-1, TOP_K).T
        start = 0
        for si, ((t, _, probs, h, mods, tl), (b0, nb)) in enumerate(zip(streams, cuts)):
            l = t.shape[1]
            n = nb * l
            yg = ys.at[pos_k[:, start:start + n].reshape(-1)].get(mode="promise_in_bounds")
            outs[si] = _moe_combine(yg.reshape(TOP_K, nb, l, d), probs, h, mods, norm_w, tl, b0, outs[si])
            start += n
    return outs


def _k_rms(t, w):
    return t * lax.rsqrt(jnp.mean(t * t, axis=-1, keepdims=True) + RMS_EPS) * w


def _k_post_mix(y, h, gate, post_w, pre_w, shift, scale, rw, rb):
    h1 = h + gate * _k_rms(y, post_w)
    t = _k_rms(h1, pre_w) * (1.0 + scale) + shift
    logits = jnp.dot(t, rw, preferred_element_type=F32) + rb
    lane = lax.broadcasted_iota(jnp.int32, logits.shape, 1)
    idx_out = jnp.zeros(logits.shape, jnp.int32)
    val_out = jnp.zeros(logits.shape, F32)
    work = logits
    m0 = None
    for k in range(TOP_K):
        m = jnp.max(work, axis=-1, keepdims=True)
        sel = jnp.min(jnp.where(work == m, lane, 128), axis=-1, keepdims=True)
        if k == 0:
            m0 = m
        idx_out = jnp.where(lane == k, sel, idx_out)
        val_out = jnp.where(lane == k, jnp.exp(m - m0), val_out)
        work = jnp.where(lane == sel, -jnp.inf, work)
    probs = val_out / jnp.sum(val_out, axis=-1, keepdims=True)
    return h1, t, idx_out, probs


S5_BATCH_SUB = 8
S5_CH_BLOCK = 128
S5_ST_BLOCK = (S5_CH_BLOCK // S5_GROUP) * S5_STATE
S5_TL = 256
S5_GLU_TL = 64


def _s5_inproj_body(h_ref, sh_ref, sc_ref, nw_ref, w_ref, u_ref, il_s):
    tl = h_ref.shape[1]
    ncb = il_s.shape[0]
    for b in range(S5_BATCH_SUB):
        mb = b if sh_ref.shape[0] > 1 else 0
        a = _k_rms(h_ref[b], nw_ref[...]) * (1.0 + sc_ref[mb]) + sh_ref[mb]
        u = jnp.dot(a.astype(BF16), w_ref[...], preferred_element_type=F32)
        for cb in range(ncb):
            il_s[cb, pl.ds(b, tl, stride=S5_BATCH_SUB), :] = u[:, cb * 128:(cb + 1) * 128]
    for cb in range(ncb):
        u_ref[0, :, cb * 128:(cb + 1) * 128] = il_s[cb]


def _s5_inproj(h, mods, norm_w, w_bf16, tl):
    b, l, d = h.shape
    nbg = b // S5_BATCH_SUB
    nm = S5_BATCH_SUB if mods.shape[0] > 1 else 1
    msel = (lambda gi: gi) if mods.shape[0] > 1 else (lambda gi: 0)
    return pl.pallas_call(
        _s5_inproj_body,
        grid=(nbg, l // tl),
        in_specs=[
            pl.BlockSpec((S5_BATCH_SUB, tl, d), lambda gi, ti: (gi, ti, 0)),
            pl.BlockSpec((nm, 1, d), lambda gi, ti: (msel(gi), 0, 0)),
            pl.BlockSpec((nm, 1, d), lambda gi, ti: (msel(gi), 0, 1)),
            pl.BlockSpec((1, d), lambda gi, ti: (0, 0)),
            pl.BlockSpec((d, d), lambda gi, ti: (0, 0)),
        ],
        out_specs=pl.BlockSpec((1, tl * S5_BATCH_SUB, d), lambda gi, ti: (gi, ti, 0)),
        out_shape=jax.ShapeDtypeStruct((nbg, l * S5_BATCH_SUB, d), F32),
        scratch_shapes=[pltpu.VMEM((d // 128, tl * S5_BATCH_SUB, 128), F32)],
        compiler_params=pltpu.CompilerParams(
            dimension_semantics=("arbitrary", "arbitrary"), vmem_limit_bytes=V7X_VMEM_LIMIT_BYTES),
        name="s5_inproj",
    )(h, mods, mods, norm_w.reshape(1, d), w_bf16)


def _s5_scan_body(uc_ref, ul_ref, b_ref, c_ref, lam_ref, y_ref, buf_ref, st_ref):
    d = pl.program_id(0)
    tc = pl.program_id(3)
    tl = ul_ref.shape[1]
    rows = tl * S5_BATCH_SUB
    ns = S5_ST_BLOCK

    @pl.when(tc == 0)
    def _():
        st_ref[...] = jnp.zeros(st_ref.shape, F32)
        u2 = uc_ref[0].reshape(rows, S5_CH_BLOCK).astype(BF16)
        buf_ref[...] = jnp.dot(u2, b_ref[0, 0], preferred_element_type=F32)

    @pl.when(tc > 0)
    def _():
        u2 = ul_ref[0].reshape(rows, S5_CH_BLOCK).astype(BF16)
        buf_ref[...] = jnp.dot(u2, b_ref[0, 0], preferred_element_type=F32)

    lam = lam_ref[0, 0]
    lr = jnp.broadcast_to(lam[0:1], (S5_BATCH_SUB, ns))
    li = jnp.broadcast_to(lam[1:2], (S5_BATCH_SUB, ns))

    def step(i, carry):
        xr, xi = carry
        t = jnp.where(d == 0, i, tl - 1 - i)
        r0 = pl.multiple_of(t * S5_BATCH_SUB, S5_BATCH_SUB)
        br = buf_ref[pl.ds(r0, S5_BATCH_SUB), 0:ns]
        bi = buf_ref[pl.ds(r0, S5_BATCH_SUB), ns:2 * ns]
        nr = lr * xr - li * xi + br
        ni = lr * xi + li * xr + bi
        buf_ref[pl.ds(r0, S5_BATCH_SUB), 0:ns] = nr
        buf_ref[pl.ds(r0, S5_BATCH_SUB), ns:2 * ns] = ni
        return nr, ni

    xr, xi = lax.fori_loop(0, tl, step, (st_ref[0], st_ref[1]), unroll=8)
    st_ref[0] = xr
    st_ref[1] = xi

    @pl.when(tc > 0)
    def _():
        y = jnp.dot(buf_ref[...].astype(BF16), c_ref[0, 0], preferred_element_type=F32)
        y_ref[0, 0] = y.reshape(tl, S5_BATCH_SUB, S5_CH_BLOCK)


def _s5_scan(u_ctx, u_lat, bblk, cblk, lam):
    nbg, l_lat, _, d = u_lat.shape
    assert u_ctx.shape[1] == S5_TL and l_lat % S5_TL == 0
    n_lc = l_lat // S5_TL
    n_sb = d // S5_CH_BLOCK

    def lat_chunk(di, tc):
        j = jnp.maximum(tc - 1, 0)
        return jnp.where(di == 0, j, n_lc - 1 - j)

    return pl.pallas_call(
        _s5_scan_body,
        grid=(2, nbg, n_sb, n_lc + 1),
        in_specs=[
            pl.BlockSpec((1, S5_TL, S5_BATCH_SUB, S5_CH_BLOCK), lambda di, bg, sb, tc: (bg, 0, 0, sb)),
            pl.BlockSpec((1, S5_TL, S5_BATCH_SUB, S5_CH_BLOCK), lambda di, bg, sb, tc: (bg, lat_chunk(di, tc), 0, sb)),
            pl.BlockSpec((1, 1, S5_CH_BLOCK, 2 * S5_ST_BLOCK), lambda di, bg, sb, tc: (di, sb, 0, 0)),
            pl.BlockSpec((1, 1, 2 * S5_ST_BLOCK, S5_CH_BLOCK), lambda di, bg, sb, tc: (di, sb, 0, 0)),
            pl.BlockSpec((1, 1, 2, S5_ST_BLOCK), lambda di, bg, sb, tc: (di, sb, 0, 0)),
        ],
        out_specs=pl.BlockSpec((1, 1, S5_TL, S5_BATCH_SUB, S5_CH_BLOCK),
                               lambda di, bg, sb, tc: (di, bg, lat_chunk(di, tc), 0, sb)),
        out_shape=jax.ShapeDtypeStruct((2, nbg, l_lat, S5_BATCH_SUB, d), F32),
        scratch_shapes=[
            pltpu.VMEM((S5_TL * S5_BATCH_SUB, 2 * S5_ST_BLOCK), F32),
            pltpu.VMEM((2, S5_BATCH_SUB, S5_ST_BLOCK), F32),
        ],
        compiler_params=pltpu.CompilerParams(
            dimension_semantics=("arbitrary", "arbitrary", "arbitrary", "arbitrary"),
            vmem_limit_bytes=V7X_VMEM_LIMIT_BYTES),
        name="s5_scan",
    )(u_ctx, u_lat, bblk, cblk, lam)


def _s5_glu_body(y0_ref, y1_ref, u_ref, h_ref, gate_ref, sh_ref, sc_ref, dsk_ref, wa_ref, wb_ref,
                 postw_ref, prew_ref, rw_ref, rb_ref, h1_ref, t_ref, idx_ref, p_ref, ym_s):
    tl = h_ref.shape[1]
    y = dsk_ref[...] * u_ref[0] + y0_ref[0, 0] + y1_ref[0, 0]
    o = (0.5 * y * (1.0 + jnp.tanh(math.sqrt(2.0 / math.pi) * (y + 0.044715 * (y * y * y))))).astype(BF16)
    ga = jnp.dot(o, wa_ref[...], preferred_element_type=F32)
    gb = jnp.dot(o, wb_ref[...], preferred_element_type=F32)
    ym_all = ga * jax.nn.sigmoid(gb)
    ncb = ym_s.shape[0]
    for cb in range(ncb):
        ym_s[cb] = ym_all[:, cb * 128:(cb + 1) * 128]
    for b in range(S5_BATCH_SUB):
        ym = jnp.concatenate([ym_s[cb, pl.ds(b, tl, stride=S5_BATCH_SUB), :] for cb in range(ncb)],
                             axis=-1)
        h1, t, idx, probs = _k_post_mix(ym, h_ref[b], gate_ref[b], postw_ref[...], prew_ref[...],
                                        sh_ref[b], sc_ref[b], rw_ref[...], rb_ref[...])
        h1_ref[b] = h1
        t_ref[b] = t.astype(BF16)
        idx_ref[b] = idx
        p_ref[b] = probs


def _s5_glu_postmix(y, u, h, mods, d_skip, wa, wb, post_w, pre_w, rw_pad, rb_pad, tl):
    b, l, d = h.shape
    nbg = b // S5_BATCH_SUB
    nsub = S5_BATCH_SUB
    row = lambda v: v.reshape(1, -1)
    full = lambda shape: pl.BlockSpec(shape, lambda gi, ti: (0,) * len(shape))
    mod = lambda k: pl.BlockSpec((nsub, 1, d), lambda gi, ti: (gi, 0, k))
    tok = lambda w: pl.BlockSpec((nsub, tl, w), lambda gi, ti: (gi, ti, 0))
    return pl.pallas_call(
        _s5_glu_body,
        grid=(nbg, l // tl),
        in_specs=[
            pl.BlockSpec((1, 1, tl * nsub, d), lambda gi, ti: (0, gi, ti, 0)),
            pl.BlockSpec((1, 1, tl * nsub, d), lambda gi, ti: (1, gi, ti, 0)),
            pl.BlockSpec((1, tl * nsub, d), lambda gi, ti: (gi, ti, 0)),
            tok(d), mod(2), mod(3), mod(4),
            full((1, d)), full((d, d)), full((d, d)), full((1, d)), full((1, d)), full((d, 128)), full((1, 128)),
        ],
        out_specs=[tok(d), tok(d), tok(128), tok(128)],
        scratch_shapes=[pltpu.VMEM((d // 128, tl * nsub, 128), F32)],
        out_shape=[
            jax.ShapeDtypeStruct((b, l, d), F32),
            jax.ShapeDtypeStruct((b, l, d), BF16),
            jax.ShapeDtypeStruct((b, l, 128), jnp.int32),
            jax.ShapeDtypeStruct((b, l, 128), F32),
        ],
        compiler_params=pltpu.CompilerParams(
            dimension_semantics=("arbitrary", "arbitrary"), vmem_limit_bytes=V7X_VMEM_LIMIT_BYTES),
        name="s5_glu_postmix",
    )(y, y, u, h, mods, mods, mods, row(d_skip), wa, wb, row(post_w), row(pre_w), rw_pad, rb_pad)


def _s5_block_params(a_re, a_im, log_step, b_re, b_im, c_re, c_im):
    gpb = S5_CH_BLOCK // S5_GROUP
    eye = jnp.eye(gpb, dtype=F32)
    bblks, cblks, lams = [], [], []
    for di in range(2):
        lam_re, lam_im, bb_re, bb_im = _s5_discretize(a_re[di], a_im[di], log_step[di], b_re[di], b_im[di])
        n_sb = lam_re.shape[0] // gpb

        def bdiag_in(bb):
            t = bb.reshape(n_sb, gpb, S5_STATE, S5_GROUP)
            return jnp.einsum('sgph,gk->sghkp', t, eye).reshape(n_sb, S5_CH_BLOCK, S5_ST_BLOCK)

        def bdiag_out(cc):
            t = cc.reshape(n_sb, gpb, S5_GROUP, S5_STATE)
            return jnp.einsum('sghp,gk->sgpkh', t, eye).reshape(n_sb, S5_ST_BLOCK, S5_CH_BLOCK)

        bblks.append(jnp.concatenate([bdiag_in(bb_re), bdiag_in(bb_im)], axis=-1))
        cblks.append(jnp.concatenate([bdiag_out(c_re[di]), -bdiag_out(c_im[di])], axis=-2))
        lams.append(jnp.stack([lam_re.reshape(n_sb, S5_ST_BLOCK), lam_im.reshape(n_sb, S5_ST_BLOCK)], axis=1))
    return jnp.stack(bblks).astype(BF16), jnp.stack(cblks).astype(BF16), jnp.stack(lams)


def _pad_router(router_w, router_b):
    e = router_w.shape[1]
    rw = jnp.pad(router_w, ((0, 0), (0, 128 - e)))
    rb = jnp.pad(router_b, (0, 128 - e), constant_values=-1e30).reshape(1, 128)
    return rw, rb


CONV_CH = GDN_CONV_CH + SSD_CONV_CH
Z_CH = GDN_V_WIDTH + SSD_INNER
IN0_TILE = 256
IN0_CONV_TILES = CONV_CH // IN0_TILE
IN0_Z_TILES = Z_CH // IN0_TILE
IN0_TILES = IN0_CONV_TILES + IN0_Z_TILES + 1
CONV_ROW_CHUNK = 256


def _conv_halo(cols):
    return ((cols + 1 + 7) // 8) * 8


def _in0_body(h_ref, sh_ref, sc_ref, nw_ref, w_ref, cw_ref, cb_ref, gp_ref,
              conv_ref, z_ref, g_ref, gt_ref, a_s, p0_s, pm_s, pp_s, *, n_rows, n_cols):
    n = pl.program_id(1)
    l = h_ref.shape[1]
    halo = _conv_halo(n_cols)
    rc = min(CONV_ROW_CHUNK, l)

    @pl.when(n == 0)
    def _():
        def norm_rows(i, carry):
            r0 = pl.multiple_of(i * rc, rc)
            a = _k_rms(h_ref[0, pl.ds(r0, rc), :], nw_ref[...]) * (1.0 + sc_ref[0]) + sh_ref[0]
            a_s[pl.ds(r0, rc), :] = a.astype(BF16)
            return carry
        lax.fori_loop(0, l // rc, norm_rows, 0)

    p = jnp.dot(a_s[...], w_ref[...], preferred_element_type=F32)

    @pl.when(n < IN0_CONV_TILES)
    def _():
        zero_halo = jnp.zeros((halo, IN0_TILE), F32)
        for s in (p0_s, pm_s, pp_s):
            s[pl.ds(0, halo), :] = zero_halo
            s[pl.ds(halo + l, halo), :] = zero_halo
        p0_s[pl.ds(halo, l), :] = p
        tcol = lax.broadcasted_iota(jnp.int32, (l, IN0_TILE), 0) % n_cols
        pm_s[pl.ds(halo, l), :] = jnp.where(tcol != 0, p0_s[pl.ds(halo - 1, l), :], 0.0)
        pp_s[pl.ds(halo, l), :] = jnp.where(tcol != n_cols - 1, p0_s[pl.ds(halo + 1, l), :], 0.0)
        dys = (0,) if n_rows == 1 else (-1, 0, 1)

        def conv_rows(i, carry):
            r0 = pl.multiple_of(i * rc, rc)
            acc = jnp.zeros((rc, IN0_TILE), F32) + cb_ref[...]
            for dy in dys:
                base = halo + dy * n_cols
                for dx, src in ((0, pm_s), (1, p0_s), (2, pp_s)):
                    tap = (dy + 1) * 3 + dx
                    acc = acc + cw_ref[tap:tap + 1, :] * src[pl.ds(r0 + base, rc), :]
            conv_ref[0, pl.ds(r0, rc), :] = (acc * jax.nn.sigmoid(acc)).astype(conv_ref.dtype)
            return carry
        lax.fori_loop(0, l // rc, conv_rows, 0)

    @pl.when((n >= IN0_CONV_TILES) & (n < IN0_CONV_TILES + IN0_Z_TILES))
    def _():
        z_ref[0] = (p * jax.nn.sigmoid(p)).astype(z_ref.dtype)

    @pl.when(n == IN0_TILES - 1)
    def _():
        pg = p[:, :128]
        lane = lax.broadcasted_iota(jnp.int32, pg.shape, 1)
        xb = pg + gp_ref[0:1, :]
        sp = jnp.maximum(xb, 0.0) + jnp.log(1.0 + jnp.exp(-jnp.abs(xb)))
        neg_a_sp = -jnp.exp(gp_ref[1:2, :]) * sp
        gates = jnp.where(lane < 2 * GDN_HEADS, neg_a_sp,
                          jnp.where(lane < GATE_COL_DT, jax.nn.sigmoid(pg),
                                    jnp.where(lane < GATE_COL_DA, sp,
                                              jnp.where(lane < GATE_COL_DA + 2 * SSD_HEADS, neg_a_sp, 0.0))))
        g_ref[0] = gates
        for ci in range(l // SCAN_C):
            gt_ref[0, ci] = gates[ci * SCAN_C:(ci + 1) * SCAN_C, :].T


def _in0_features(h, mods, norm_w, w_all, conv_w, conv_b, gate_params, n_rows, n_cols):
    b, l, d = h.shape
    nb = mods.shape[0]
    bsel = (lambda bi: bi) if nb > 1 else (lambda bi: 0)
    halo = _conv_halo(n_cols)
    nct, nzt = IN0_CONV_TILES, IN0_Z_TILES
    pad_rows = l + 2 * halo
    return pl.pallas_call(
        functools.partial(_in0_body, n_rows=n_rows, n_cols=n_cols),
        grid=(b, IN0_TILES),
        in_specs=[
            pl.BlockSpec((1, l, d), lambda bi, ni: (bi, 0, 0)),
            pl.BlockSpec((1, 1, d), lambda bi, ni: (bsel(bi), 0, 0)),
            pl.BlockSpec((1, 1, d), lambda bi, ni: (bsel(bi), 0, 1)),
            pl.BlockSpec((1, d), lambda bi, ni: (0, 0)),
            pl.BlockSpec((d, IN0_TILE), lambda bi, ni: (0, ni)),
            pl.BlockSpec((9, IN0_TILE), lambda bi, ni: (0, jnp.minimum(ni, nct - 1))),
            pl.BlockSpec((1, IN0_TILE), lambda bi, ni: (0, jnp.minimum(ni, nct - 1))),
            pl.BlockSpec((2, 128), lambda bi, ni: (0, 0)),
        ],
        out_specs=[
            pl.BlockSpec((1, l, IN0_TILE), lambda bi, ni: (bi, 0, jnp.minimum(ni, nct - 1))),
            pl.BlockSpec((1, l, IN0_TILE), lambda bi, ni: (bi, 0, jnp.clip(ni - nct, 0, nzt - 1))),
            pl.BlockSpec((1, l, 128), lambda bi, ni: (bi, 0, 0)),
            pl.BlockSpec((1, l // SCAN_C, 128, SCAN_C), lambda bi, ni: (bi, 0, 0, 0)),
        ],
        out_shape=[
            jax.ShapeDtypeStruct((b, l, CONV_CH), BF16),
            jax.ShapeDtypeStruct((b, l, Z_CH), BF16),
            jax.ShapeDtypeStruct((b, l, 128), F32),
            jax.ShapeDtypeStruct((b, l // SCAN_C, 128, SCAN_C), F32),
        ],
        scratch_shapes=[
            pltpu.VMEM((l, d), BF16),
            pltpu.VMEM((pad_rows, IN0_TILE), F32),
            pltpu.VMEM((pad_rows, IN0_TILE), F32),
            pltpu.VMEM((pad_rows, IN0_TILE), F32),
        ],
        compiler_params=pltpu.CompilerParams(
            dimension_semantics=("arbitrary", "arbitrary"), vmem_limit_bytes=V7X_VMEM_LIMIT_BYTES),
        name="in0_features",
    )(h, mods, mods, norm_w.reshape(1, d), w_all, conv_w, conv_b, gate_params)


def _in0_params(w_in, gdn_conv_w, ssd_conv_w, ssd_conv_b, gdn_a_log, gdn_dt_bias, ssd_a_log, ssd_dt_bias):
    splits = np.cumsum(IN0_SIZES)[:-1].tolist()
    w_ca, w_cb, w_za, w_zb, w_a, w_b, w_dt = jnp.split(w_in, splits, axis=1)
    d = w_in.shape[0]
    w_gate = jnp.concatenate([w_a, w_b, w_dt, w_dt], axis=1)
    w_gate = jnp.pad(w_gate, ((0, 0), (0, IN0_TILE - w_gate.shape[1])))
    w_all = jnp.concatenate([w_ca, w_cb, w_za, w_zb, w_gate], axis=1).astype(BF16)
    conv_w = jnp.concatenate([gdn_conv_w, ssd_conv_w], axis=-1).reshape(9, CONV_CH)
    conv_b = jnp.concatenate([jnp.zeros((GDN_CONV_CH,), F32), ssd_conv_b]).reshape(1, CONV_CH)
    z8 = jnp.zeros((2 * GDN_HEADS,), F32)
    z16 = jnp.zeros((2 * SSD_HEADS,), F32)
    tail = jnp.zeros((128 - GATE_COL_DA - 2 * SSD_HEADS,), F32)
    bias = jnp.concatenate([gdn_dt_bias.reshape(-1), z8, ssd_dt_bias.reshape(-1), ssd_dt_bias.reshape(-1), tail])
    alog = jnp.concatenate([gdn_a_log.reshape(-1), z8, z16, ssd_a_log.reshape(-1), tail])
    return w_all, conv_w, conv_b, jnp.stack([bias, alog])


def _merge0_body(o_ref, y_ref, xs_ref, z_ref, h_ref, gate_ref, sh_ref, sc_ref, gnw_ref, dsk_ref, snw_ref, wo_ref,
                 postw_ref, prew_ref, rw_ref, rb_ref, h1_ref, t_ref, idx_ref, p_ref):
    z = z_ref[0].astype(F32)
    parts = []
    for hd in range(GDN_HEADS):
        cs = slice(hd * GDN_DV, (hd + 1) * GDN_DV)
        parts.append(_k_rms(o_ref[0, :, cs], gnw_ref[...]) * z[:, cs])
    gw = SSD_INNER // SSD_GROUPS
    for g in range(SSD_GROUPS):
        cs = slice(g * gw, (g + 1) * gw)
        y2 = (y_ref[0, :, cs] + dsk_ref[:, cs] * xs_ref[0, :, cs].astype(F32)) * z[:, GDN_V_WIDTH + g * gw:GDN_V_WIDTH + (g + 1) * gw]
        parts.append(_k_rms(y2, snw_ref[:, cs]))
    mixed = jnp.concatenate(parts, axis=-1).astype(BF16)
    ym = jnp.dot(mixed, wo_ref[...], preferred_element_type=F32)
    h1, t, idx, probs = _k_post_mix(ym, h_ref[0], gate_ref[0], postw_ref[...], prew_ref[...],
                                    sh_ref[0], sc_ref[0], rw_ref[...], rb_ref[...])
    h1_ref[0] = h1
    t_ref[0] = t.astype(BF16)
    idx_ref[0] = idx
    p_ref[0] = probs


def _merge0_postmix(o, y, conv, z, h, mods, gdn_norm_w, ssd_d, ssd_norm_w, w_out, post_w, pre_w, rw_pad, rb_pad, tl):
    b, l, d = h.shape
    nb = mods.shape[0]
    bsel = (lambda bi: bi) if nb > 1 else (lambda bi: 0)
    row = lambda v: v.reshape(1, -1)
    full = lambda shape: pl.BlockSpec(shape, lambda bi, ti: (0,) * len(shape))
    mod = lambda k: pl.BlockSpec((1, 1, d), lambda bi, ti: (bsel(bi), 0, k))
    tok = lambda w: pl.BlockSpec((1, tl, w), lambda bi, ti: (bi, ti, 0))
    xs_block = GDN_CONV_CH // SSD_INNER
    dsk = jnp.repeat(ssd_d, SSD_HEADDIM)
    return pl.pallas_call(
        _merge0_body,
        grid=(b, l // tl),
        in_specs=[
            tok(GDN_V_WIDTH), tok(SSD_INNER),
            pl.BlockSpec((1, tl, SSD_INNER), lambda bi, ti: (bi, ti, xs_block)),
            tok(Z_CH), tok(d), mod(2), mod(3), mod(4),
            full((1, GDN_DV)), full((1, SSD_INNER)), full((1, SSD_INNER)), full((Z_CH, d)),
            full((1, d)), full((1, d)), full((d, 128)), full((1, 128)),
        ],
        out_specs=[tok(d), tok(d), tok(128), tok(128)],
        out_shape=[
            jax.ShapeDtypeStruct((b, l, d), F32),
            jax.ShapeDtypeStruct((b, l, d), BF16),
            jax.ShapeDtypeStruct((b, l, 128), jnp.int32),
            jax.ShapeDtypeStruct((b, l, 128), F32),
        ],
        compiler_params=pltpu.CompilerParams(
            dimension_semantics=("arbitrary", "arbitrary"), vmem_limit_bytes=V7X_VMEM_LIMIT_BYTES),
        name="merge0_postmix",
    )(o, y, conv, z, h, mods, mods, mods, row(gdn_norm_w), row(dsk), row(ssd_norm_w), w_out,
      row(post_w), row(pre_w), rw_pad, rb_pad)


SCAN_C = 128
GDN_INV_BLOCK = 16


def _dot(a, b):
    return jnp.dot(a, b, preferred_element_type=F32)


def _dot_nt(a, b):
    return lax.dot_general(a, b, (((1,), (1,)), ((), ())), preferred_element_type=F32)


def _dot_tn(a, b):
    return lax.dot_general(a, b, (((0,), (0,)), ((), ())), preferred_element_type=F32)


def _bdot(a, b):
    return _dot(a.astype(BF16), b.astype(BF16))


def _bdot_nt(a, b):
    return _dot_nt(a.astype(BF16), b.astype(BF16))


def _bdot_tn(a, b):
    return _dot_tn(a.astype(BF16), b.astype(BF16))


def _scan_masks(fwd):
    row = lax.broadcasted_iota(jnp.int32, (SCAN_C, SCAN_C), 0)
    col = lax.broadcasted_iota(jnp.int32, (SCAN_C, SCAN_C), 1)
    lead = (row - col) * jnp.where(fwd, 1, -1)
    return row, col, lead >= 0, lead <= 0, lead > 0


def _cumsum_col_row(g_col, g_row, incl, incl_t):
    gc_col = jnp.sum(jnp.where(incl, g_row, 0.0), axis=1, keepdims=True)
    gc_row = jnp.sum(jnp.where(incl_t, g_col, 0.0), axis=0, keepdims=True)
    return gc_col, gc_row


def _unit_tri_inverse(ms, row, col):
    eye = (row == col).astype(F32)
    same = (row // GDN_INV_BLOCK) == (col // GDN_INV_BLOCK)
    mds = [jnp.where(same, m, 0.0) for m in ms]
    mos = [m - md for m, md in zip(ms, mds)]
    ps = mds
    tds = [eye - md for md in mds]
    for _ in range(int(math.log2(GDN_INV_BLOCK)) - 1):
        ps = [_dot(p, p) for p in ps]
        tds = [td + _dot(td, p) for td, p in zip(tds, ps)]
    ps = [_dot(td, mo) for td, mo in zip(tds, mos)]
    qs = [eye - n for n in ps]
    for _ in range(int(math.log2(SCAN_C // GDN_INV_BLOCK)) - 1):
        ps = [_dot(p, p) for p in ps]
        qs = [q + _dot(q, p) for q, p in zip(qs, ps)]
    return [_dot(q, td) for q, td in zip(qs, tds)]


GDN_PREP_GROUP = 8
GDN_HEADS_PER_STEP = 2


def _gdn_body(ql_ref, kl_ref, vl_ref, gl_ref, gtl_ref, qc_ref, kc_ref, vc_ref, gc_ref, gtc_ref,
              ol_ref, oc_ref, u_s, w_s, qk_s, qd_s, kd_s, dec_s, obl_s, obc_s):
    hp = pl.program_id(1)
    nh = GDN_HEADS_PER_STEP
    c = SCAN_C
    ncc = qc_ref.shape[1] // c
    ncl = ql_ref.shape[1] // c
    grp = math.gcd(ncl, GDN_PREP_GROUP)
    lane = lax.broadcasted_iota(jnp.int32, (c, 128), 1)

    def order(d, i, n):
        return i if d == 0 else n - 1 - i

    def prep(s, d, q_ref, k_ref, v_ref, g_ref, gt_ref, cis, ps):
        row, col, incl, incl_t, strict = _scan_masks(d == 0)
        col_g = d * GDN_HEADS + hp * nh + s
        col_b = 2 * GDN_HEADS + col_g
        hs = slice(s * 128, (s + 1) * 128)
        cid = s * 2 + d
        loaded = []
        for ci in cis:
            t0 = pl.multiple_of(ci * c, c)
            loaded.append((q_ref[0, pl.ds(t0, c), hs].astype(F32), k_ref[0, pl.ds(t0, c), hs].astype(F32),
                           v_ref[0, pl.ds(t0, c), hs].astype(F32), g_ref[0, pl.ds(t0, c), :],
                           gt_ref[0, ci, pl.ds(col_g, 1), :]))
        parts, ms = [], []
        for q, k, v, gch, g_row in loaded:
            q = q * lax.rsqrt(jnp.sum(q * q, axis=-1, keepdims=True) + RMS_EPS) * (GDN_DK ** -0.5)
            k = k * lax.rsqrt(jnp.sum(k * k, axis=-1, keepdims=True) + RMS_EPS)
            g_col = jnp.sum(jnp.where(lane == col_g, gch, 0.0), axis=1, keepdims=True)
            b_col = jnp.sum(jnp.where(lane == col_b, gch, 0.0), axis=1, keepdims=True)
            gc_col, gc_row = _cumsum_col_row(g_col, g_row, incl, incl_t)
            g_tot = jnp.sum(g_col, axis=0, keepdims=True)
            decay = jnp.where(incl, jnp.exp(jnp.where(incl, gc_col - gc_row, 0.0)), 0.0)
            kb = k * b_col
            ms.append(jnp.where(strict, _bdot_nt(kb, k) * decay, 0.0))
            egc = jnp.exp(gc_col)
            parts.append((v * b_col, kb * egc, _bdot_nt(q, k) * decay, q * egc, k * jnp.exp(g_tot - gc_col),
                          jnp.broadcast_to(jnp.exp(g_tot), (1, 128))))
        t_invs = _unit_tri_inverse(ms, row, col)
        uw = [(_bdot(t_inv, part[0]), _bdot(t_inv, part[1])) for t_inv, part in zip(t_invs, parts)]
        for p, (u, w), (_, _, qk, qd, kd, dec) in zip(ps, uw, parts):
            p0 = pl.multiple_of(p * c, c)
            u_s[cid, pl.ds(p0, c), :] = u
            w_s[cid, pl.ds(p0, c), :] = w
            qk_s[cid, pl.ds(p0, c), :] = qk
            qd_s[cid, pl.ds(p0, c), :] = qd
            kd_s[cid, pl.ds(p0, c), :] = kd
            dec_s[cid, pl.ds(p, 1), :] = dec

    def advance(of_ref, ob_ref, i, n, p, states):
        p0 = pl.multiple_of(p * c, c)
        outs, new_states = [], []
        for cid, st in enumerate(states):
            v_new = u_s[cid, pl.ds(p0, c), :] - _bdot(w_s[cid, pl.ds(p0, c), :], st)
            outs.append(_bdot(qd_s[cid, pl.ds(p0, c), :], st) + _bdot(qk_s[cid, pl.ds(p0, c), :], v_new))
            new_states.append(st * dec_s[cid, pl.ds(p, 1), :] + _bdot_tn(kd_s[cid, pl.ds(p0, c), :], v_new))
        for s in range(nh):
            of_ref[0, pl.ds(pl.multiple_of(order(0, i, n) * c, c), c), s * 128:(s + 1) * 128] = outs[s * 2]
            ob_ref[s, pl.ds(pl.multiple_of(order(1, i, n) * c, c), c), :] = outs[s * 2 + 1]
        return tuple(new_states)

    for s, d in [(s, d) for s in range(nh) for d in range(2)]:
        prep(s, d, qc_ref, kc_ref, vc_ref, gc_ref, gtc_ref, [order(d, i, ncc) for i in range(ncc)],
             list(range(ncc)))

        def prep_lat(gi, carry, s=s, d=d):
            base = gi * grp
            prep(s, d, ql_ref, kl_ref, vl_ref, gl_ref, gtl_ref,
                 [order(d, base + j, ncl) for j in range(grp)], [ncc + base + j for j in range(grp)])
            return carry

        lax.fori_loop(0, ncl // grp, prep_lat, 0)

    states = tuple(jnp.zeros((GDN_DK, GDN_DV), F32) for _ in range(2 * nh))
    for i in range(ncc):
        states = advance(oc_ref, obc_s, i, ncc, i, states)
    lax.fori_loop(0, ncl, lambda i, st: advance(ol_ref, obl_s, i, ncl, ncc + i, st), states)
    for s in range(nh):
        hs = slice(s * 128, (s + 1) * 128)
        ol_ref[0, :, hs] = ol_ref[0, :, hs] + obl_s[s]
        oc_ref[0, :, hs] = oc_ref[0, :, hs] + obc_s[s]


SSD_R = SSD_HEADS // SSD_GROUPS
SSD_PAIRS = SSD_R * SSD_HEADDIM // 128
GATE_COL_DT = 4 * GDN_HEADS
GATE_COL_DA = GATE_COL_DT + 2 * SSD_HEADS


def _ssd_body(xl_ref, bl_ref, cl_ref, gl_ref, gtl_ref, xc_ref, bc_ref, cc_ref, gc_ref, gtc_ref,
              yl_ref, yc_ref, st_ref):
    g = pl.program_id(1)
    d = pl.program_id(2)
    fwd = d == 0
    c = SCAN_C
    ncc = xc_ref.shape[1] // c
    ncl = xl_ref.shape[1] // c
    _, _, incl, incl_t, _ = _scan_masks(fwd)
    lane = lax.broadcasted_iota(jnp.int32, (c, 128), 1)
    low = lane < SSD_HEADDIM

    @pl.when(fwd)
    def _():
        yl_ref[...] = jnp.zeros(yl_ref.shape, F32)
        yc_ref[...] = jnp.zeros(yc_ref.shape, F32)

    st_ref[...] = jnp.zeros(st_ref.shape, F32)

    def chunk(x_ref, b_ref, c_ref, g_ref, gt_ref, y_ref, ci):
        t0 = pl.multiple_of(ci * c, c)
        bm = b_ref[0, pl.ds(t0, c), :].astype(BF16)
        cm = c_ref[0, pl.ds(t0, c), :].astype(BF16)
        gch = g_ref[0, pl.ds(t0, c), :]
        cb = _dot_nt(cm, bm)
        for pr in range(SSD_PAIRS):
            per_head = []
            for s in range(2):
                hh = g * SSD_R + 2 * pr + s
                col_dt = GATE_COL_DT + d * SSD_HEADS + hh
                col_da = GATE_COL_DA + d * SSD_HEADS + hh
                dt_col = jnp.sum(jnp.where(lane == col_dt, gch, 0.0), axis=1, keepdims=True)
                da_col = jnp.sum(jnp.where(lane == col_da, gch, 0.0), axis=1, keepdims=True)
                da_row = gt_ref[0, ci, pl.ds(col_da, 1), :]
                acs_col, acs_row = _cumsum_col_row(da_col, da_row, incl, incl_t)
                a_tot = jnp.sum(da_col, axis=0, keepdims=True)
                lmat = jnp.where(incl, jnp.exp(jnp.where(incl, acs_col - acs_row, 0.0)), 0.0)
                per_head.append((dt_col, acs_col, a_tot, (cb * lmat).astype(BF16)))
            pick = lambda k: jnp.where(low, per_head[0][k], per_head[1][k])
            x = x_ref[0, pl.ds(t0, c), pr * 128:(pr + 1) * 128].astype(F32)
            xdt = x * pick(0)
            acs = pick(1)
            a_tot = pick(2)
            xdt_b = xdt.astype(BF16)
            y_diag = jnp.where(low, _dot(per_head[0][3], xdt_b), _dot(per_head[1][3], xdt_b))
            st = st_ref[pr]
            y_off = _dot(cm, st.astype(BF16)) * jnp.exp(acs)
            y_ref[0, pl.ds(t0, c), pr * 128:(pr + 1) * 128] += y_diag + y_off
            st_ref[pr] = st * jnp.exp(a_tot) + _dot_tn(bm, (xdt * jnp.exp(a_tot - acs)).astype(BF16))

    def order(i, n):
        return jnp.where(fwd, i, n - 1 - i)

    for i in range(ncc):
        chunk(xc_ref, bc_ref, cc_ref, gc_ref, gtc_ref, yc_ref, order(i, ncc))

    def lat(i, carry):
        chunk(xl_ref, bl_ref, cl_ref, gl_ref, gtl_ref, yl_ref, order(i, ncl))
        return carry

    lax.fori_loop(0, ncl, lat, 0)


def _ssd_scan(xbc_l, g_l, gt_l, xbc_c, g_c, gt_c, col0=0):
    b, l, _ = xbc_l.shape
    lc = xbc_c.shape[1]
    gw = SSD_R * SSD_HEADDIM
    x0 = col0 // gw
    nxb = (col0 + SSD_INNER) // 128

    def stream(n):
        return [pl.BlockSpec((1, n, gw), lambda bi, gi, di: (bi, 0, x0 + gi)),
                pl.BlockSpec((1, n, 128), lambda bi, gi, di: (bi, 0, nxb + gi)),
                pl.BlockSpec((1, n, 128), lambda bi, gi, di: (bi, 0, nxb + SSD_GROUPS + gi)),
                pl.BlockSpec((1, n, 128), lambda bi, gi, di: (bi, 0, 0)),
                pl.BlockSpec((1, n // SCAN_C, 128, SCAN_C), lambda bi, gi, di: (bi, 0, 0, 0))]

    return pl.pallas_call(
        _ssd_body,
        grid=(b, SSD_GROUPS, 2),
        in_specs=stream(l) + stream(lc),
        out_specs=[pl.BlockSpec((1, l, gw), lambda bi, gi, di: (bi, 0, gi)),
                   pl.BlockSpec((1, lc, gw), lambda bi, gi, di: (bi, 0, gi))],
        out_shape=[jax.ShapeDtypeStruct((b, l, SSD_INNER), F32),
                   jax.ShapeDtypeStruct((b, lc, SSD_INNER), F32)],
        scratch_shapes=[pltpu.VMEM((SSD_PAIRS, SSD_STATE, 128), F32)],
        compiler_params=pltpu.CompilerParams(
            dimension_semantics=("arbitrary", "arbitrary", "arbitrary"), vmem_limit_bytes=V7X_VMEM_LIMIT_BYTES),
        name="ssd_scan",
    )(xbc_l, xbc_l, xbc_l, g_l, gt_l, xbc_c, xbc_c, xbc_c, g_c, gt_c)


def _chunk_rows(gt):
    b, w, l = gt.shape
    return gt.reshape(b, w, l // SCAN_C, SCAN_C).transpose(0, 2, 1, 3)


def _gdn_scan(qkv_l, g_l, gt_l, qkv_c, g_c, gt_c):
    b, l, _ = qkv_l.shape
    lc = qkv_c.shape[1]
    hd = GDN_HEADS

    nh = GDN_HEADS_PER_STEP
    steps = hd // nh
    hw = nh * 128

    def stream(n):
        tok = lambda off: pl.BlockSpec((1, n, hw), lambda bi, hi: (bi, 0, off + hi))
        return [tok(0), tok(steps), tok(2 * steps),
                pl.BlockSpec((1, n, 128), lambda bi, hi: (bi, 0, 0)),
                pl.BlockSpec((1, n // SCAN_C, 128, SCAN_C), lambda bi, hi: (bi, 0, 0, 0))]

    nt = l + lc
    return pl.pallas_call(
        _gdn_body,
        grid=(b, steps),
        in_specs=stream(l) + stream(lc),
        out_specs=[pl.BlockSpec((1, l, hw), lambda bi, hi: (bi, 0, hi)),
                   pl.BlockSpec((1, lc, hw), lambda bi, hi: (bi, 0, hi))],
        out_shape=[jax.ShapeDtypeStruct((b, l, hd * GDN_DV), F32),
                   jax.ShapeDtypeStruct((b, lc, hd * GDN_DV), F32)],
        scratch_shapes=[pltpu.VMEM((2 * nh, nt, 128), F32) for _ in range(5)]
        + [pltpu.VMEM((2 * nh, nt // SCAN_C, 128), F32), pltpu.VMEM((nh, l, 128), F32),
           pltpu.VMEM((nh, lc, 128), F32)],
        compiler_params=pltpu.CompilerParams(
            dimension_semantics=("arbitrary", "arbitrary"), vmem_limit_bytes=V7X_VMEM_LIMIT_BYTES),
        name="gdn_scan",
    )(qkv_l, qkv_l, qkv_l, g_l, gt_l, qkv_c, qkv_c, qkv_c, g_c, gt_c)


def _ada_body(c_ref, w_ref, b_ref, o_ref):
    cc = c_ref[...]
    o_ref[0] = jnp.dot(cc * jax.nn.sigmoid(cc), w_ref[0], preferred_element_type=F32) + b_ref[0]


def _ada_mods(cond, ada_w, ada_b):
    r, d = cond.shape
    depth, _, w6 = ada_w.shape
    rp = ((r + 7) // 8) * 8
    cond = jnp.pad(cond, ((0, rp - r), (0, 0)))
    out = pl.pallas_call(
        _ada_body,
        grid=(depth, w6 // d),
        in_specs=[pl.BlockSpec((rp, d), lambda li, ni: (0, 0)),
                  pl.BlockSpec((1, d, d), lambda li, ni: (li, 0, ni)),
                  pl.BlockSpec((1, 1, d), lambda li, ni: (li, 0, ni))],
        out_specs=pl.BlockSpec((1, rp, d), lambda li, ni: (li, 0, ni)),
        out_shape=jax.ShapeDtypeStruct((depth, rp, w6), F32),
        compiler_params=pltpu.CompilerParams(dimension_semantics=("arbitrary", "arbitrary")),
        name="ada_mods",
    )(cond, ada_w, ada_b.reshape(depth, 1, w6))
    return out[:, :r]


def _s5_discretize(a_re, a_im, log_step, b_re, b_im):
    step = jnp.exp(log_step)[:, None]
    mag = jnp.exp(a_re * step)
    lam_re, lam_im = mag * jnp.cos(a_im * step), mag * jnp.sin(a_im * step)
    den = a_re * a_re + a_im * a_im
    f_re = ((lam_re - 1.0) * a_re + lam_im * a_im) / den
    f_im = (lam_im * a_re - (lam_re - 1.0) * a_im) / den
    bb_re = f_re[..., None] * b_re - f_im[..., None] * b_im
    bb_im = f_re[..., None] * b_im + f_im[..., None] * b_re
    return lam_re, lam_im, bb_re, bb_im


def kernel(x, c, ctx, c_ctx, ada_w, ada_b, mix_norm_pre, mix_norm_post, ffn_norm_pre, ffn_norm_post, router_w, router_b, moe_w_gate, moe_b_gate, moe_w_up, moe_b_up, moe_w_down, moe_b_down, hy_w_in, gdn_conv_w, gdn_a_log, gdn_dt_bias, gdn_norm_w, ssd_conv_w, ssd_conv_b, ssd_a_log, ssd_dt_bias, ssd_d, ssd_norm_w, hy_w_out, s5_w_in, s5_a_re, s5_a_im, s5_log_step, s5_b_re, s5_b_im, s5_c_re, s5_c_im, s5_d, s5_w_glu_a, s5_w_glu_b):
    depth = ada_w.shape[0]
    rows = x.shape[1] // GRID_W
    h_lat, h_ctx = x, ctx
    nb = c.shape[0]
    mods_all = _ada_mods(jnp.concatenate([c, c_ctx[None, :]], axis=0), ada_w, ada_b)
    for i in range(depth):
        j = i // 2
        need_ctx = i < depth - 1
        mods_lat = mods_all[i, :nb, None, :]
        mods_ctx = mods_all[i, nb:nb + 1, None, :]
        ffn = (i, moe_w_gate, moe_b_gate, moe_w_up, moe_b_up, moe_w_down, moe_b_down, ffn_norm_post[i])
        rw_pad, rb_pad = _pad_router(router_w[i], router_b[i])
        bsz, l, d = h_lat.shape
        lc = h_ctx.shape[1]
        tl_lat, tl_ctx = 512, lc
        if i % 2 == 1:
            assert not need_ctx
            nbg = bsz // S5_BATCH_SUB
            w_in = s5_w_in[j].astype(BF16)
            u_lat = _s5_inproj(h_lat, mods_lat, mix_norm_pre[i], w_in, S5_TL)
            u_ctx = _s5_inproj(h_ctx, mods_ctx, mix_norm_pre[i], w_in, S5_TL)
            bblk, cblk, lam = _s5_block_params(s5_a_re[j], s5_a_im[j], s5_log_step[j], s5_b_re[j], s5_b_im[j],
                                               s5_c_re[j], s5_c_im[j])
            y = _s5_scan(u_ctx.reshape(nbg, lc, S5_BATCH_SUB, d), u_lat.reshape(nbg, l, S5_BATCH_SUB, d),
                         bblk, cblk, lam)
            h1, t_lat, idx, probs = _s5_glu_postmix(
                y.reshape(2, nbg, l * S5_BATCH_SUB, d), u_lat, h_lat, mods_lat, s5_d[j],
                s5_w_glu_a[j].astype(BF16), s5_w_glu_b[j].astype(BF16),
                mix_norm_post[i], ffn_norm_pre[i], rw_pad, rb_pad, S5_GLU_TL)
            (h_lat,) = _moe_layer([(t_lat, idx, probs, h1, mods_lat, tl_lat)], *ffn)
            continue
        w_all, conv_w, conv_b, gate_params = _in0_params(hy_w_in[j], gdn_conv_w[j], ssd_conv_w[j], ssd_conv_b[j],
                                                         gdn_a_log[j], gdn_dt_bias[j], ssd_a_log[j], ssd_dt_bias[j])
        conv_l, z_l, g_l, gt_l = _in0_features(h_lat, mods_lat, mix_norm_pre[i], w_all, conv_w, conv_b, gate_params,
                                               rows, GRID_W)
        conv_c, z_c, g_c, gt_c = _in0_features(h_ctx, mods_ctx, mix_norm_pre[i], w_all, conv_w, conv_b, gate_params,
                                               1, lc)
        o_l, o_c = _gdn_scan(conv_l, g_l, gt_l, conv_c, g_c, gt_c)
        y_l, y_c = _ssd_scan(conv_l, g_l, gt_l, conv_c, g_c, gt_c, col0=GDN_CONV_CH)
        w_out = hy_w_out[j].astype(BF16)
        merge = lambda o, y, conv, z, h, mods, tl: _merge0_postmix(
            o, y, conv, z, h, mods, gdn_norm_w[j], ssd_d[j], ssd_norm_w[j], w_out,
            mix_norm_post[i], ffn_norm_pre[i], rw_pad, rb_pad, tl)
        streams = [merge(o_l, y_l, conv_l, z_l, h_lat, mods_lat, tl_lat) + (mods_lat, tl_lat)]
        if need_ctx:
            streams.append(merge(o_c, y_c, conv_c, z_c, h_ctx, mods_ctx, tl_ctx) + (mods_ctx, tl_ctx))
        outs = _moe_layer([(t, idx, probs, h1, mods, tl) for h1, t, idx, probs, mods, tl in streams], *ffn)
        h_lat = outs[0]
        if need_ctx:
            h_ctx = outs[1]
    return h_lat
```

```python
import functools
import math

import jax
import jax.numpy as jnp
import numpy as np
from jax import lax
from jax.experimental import pallas as pl
from jax.experimental.pallas import tpu as pltpu

F32 = jnp.float32
BF16 = jnp.bfloat16

D_MODEL = 1024
GRID_W = 64
RMS_EPS = 1e-6

GDN_HEADS = 4
GDN_DK = 128
GDN_DV = 128
GDN_CHUNK = 64
SSD_HEADS = 8
SSD_HEADDIM = 64
SSD_GROUPS = 2
SSD_STATE = 128
SSD_CHUNK = 128
S5_GROUP = 16
S5_GROUPS = D_MODEL // S5_GROUP
S5_STATE = 64
N_EXPERTS = 32
TOP_K = 4
SWIGLU_LIMIT = 7.0
SWIGLU_ALPHA = 1.702

GDN_V_WIDTH = GDN_HEADS * GDN_DV
SSD_INNER = SSD_HEADS * SSD_HEADDIM
GDN_CONV_CH = 2 * GDN_HEADS * GDN_DK + GDN_V_WIDTH
SSD_CONV_CH = SSD_INNER + 2 * SSD_GROUPS * SSD_STATE
IN0_SIZES = (GDN_CONV_CH, SSD_CONV_CH, GDN_V_WIDTH, SSD_INNER, 2 * GDN_HEADS, 2 * GDN_HEADS, 2 * SSD_HEADS)

V7X_VMEM_LIMIT_BYTES = 56 * 1024 * 1024
MOE_TILE_M = 512
MOE_TILE_F = 512
MOE_BATCH_SLICES = 2


def _moe_ffn_body(tile_e_ref, tile_ok_ref, x_ref, wg_ref, bg_ref, wu_ref, bu_ref, wd_ref, bd_ref, o_ref,
                  wg_s, wu_s, wd_s):
    i = pl.program_id(0)
    n_f = wg_ref.shape[3] // MOE_TILE_F

    @pl.when((i == 0) | (tile_e_ref[i] != tile_e_ref[jnp.maximum(i - 1, 0)]))
    def _():
        for c in range(n_f):
            cs = slice(c * MOE_TILE_F, (c + 1) * MOE_TILE_F)
            wg_s[:, cs] = wg_ref[0, 0, :, cs].astype(BF16)
            wu_s[:, cs] = wu_ref[0, 0, :, cs].astype(BF16)
            wd_s[cs, :] = wd_ref[0, 0, cs, :].astype(BF16)

    @pl.when(tile_ok_ref[i] > 0)
    def _():
        x = x_ref[...]
        acc = jnp.zeros(o_ref.shape, F32)
        for c in range(n_f):
            cs = slice(c * MOE_TILE_F, (c + 1) * MOE_TILE_F)
            gl = jnp.dot(x, wg_s[:, cs], preferred_element_type=F32) + bg_ref[0, 0, :, cs]
            lin = jnp.dot(x, wu_s[:, cs], preferred_element_type=F32) + bu_ref[0, 0, :, cs]
            gl = jnp.minimum(gl, SWIGLU_LIMIT)
            lin = jnp.clip(lin, -SWIGLU_LIMIT, SWIGLU_LIMIT)
            act = gl * jax.nn.sigmoid(SWIGLU_ALPHA * gl) * (lin + 1.0)
            acc = acc + jnp.dot(act.astype(BF16), wd_s[cs, :], preferred_element_type=F32)
        o_ref[...] = (acc + bd_ref[0, 0]).astype(o_ref.dtype)

    @pl.when(tile_ok_ref[i] == 0)
    def _():
        o_ref[...] = jnp.zeros(o_ref.shape, o_ref.dtype)


def _moe_grouped_ffn(xs, tile_e, tile_ok, layer, wg, bg, wu, bu, wd, bd):
    p, d = xs.shape
    nl, e, _, f = wg.shape
    n_tiles = p // MOE_TILE_M
    grid_spec = pltpu.PrefetchScalarGridSpec(
        num_scalar_prefetch=2,
        grid=(n_tiles,),
        in_specs=[
            pl.BlockSpec((MOE_TILE_M, d), lambda i, te, ok: (i, 0)),
            pl.BlockSpec((1, 1, d, f), lambda i, te, ok: (layer, te[i], 0, 0)),
            pl.BlockSpec((1, 1, 1, f), lambda i, te, ok: (layer, te[i], 0, 0)),
            pl.BlockSpec((1, 1, d, f), lambda i, te, ok: (layer, te[i], 0, 0)),
            pl.BlockSpec((1, 1, 1, f), lambda i, te, ok: (layer, te[i], 0, 0)),
            pl.BlockSpec((1, 1, f, d), lambda i, te, ok: (layer, te[i], 0, 0)),
            pl.BlockSpec((1, 1, 1, d), lambda i, te, ok: (layer, te[i], 0, 0)),
        ],
        out_specs=pl.BlockSpec((MOE_TILE_M, d), lambda i, te, ok: (i, 0)),
        scratch_shapes=[pltpu.VMEM((d, f), BF16), pltpu.VMEM((d, f), BF16), pltpu.VMEM((f, d), BF16)],
    )
    return pl.pallas_call(
        _moe_ffn_body,
        grid_spec=grid_spec,
        out_shape=jax.ShapeDtypeStruct((p, d), BF16),
        compiler_params=pltpu.CompilerParams(
            dimension_semantics=("arbitrary",), vmem_limit_bytes=V7X_VMEM_LIMIT_BYTES),
        name="moe_grouped_ffn",
    )(tile_e, tile_ok, xs, wg, bg.reshape(nl, e, 1, f), wu, bu.reshape(nl, e, 1, f), wd, bd.reshape(nl, e, 1, d))


def _moe_combine_body(y0_ref, y1_ref, y2_ref, y3_ref, p_ref, h_ref, gate_ref, nw_ref, o_ref):
    p = p_ref[0]
    acc = jnp.zeros(o_ref.shape[1:], F32)
    for k, y_ref in enumerate((y0_ref, y1_ref, y2_ref, y3_ref)):
        acc = acc + p[:, k:k + 1] * y_ref[0, 0].astype(F32)
    o_ref[0] = h_ref[0] + gate_ref[0] * _k_rms(acc, nw_ref[...])


def _moe_combine(yg, probs, h, mods, norm_w, tl, b0):
    b, l, d = h.shape
    nb = yg.shape[1]
    bsel = (lambda bi: bi + b0) if mods.shape[0] > 1 else (lambda bi: 0)
    yk = lambda k: pl.BlockSpec((1, 1, tl, d), lambda bi, ti: (k, bi, ti, 0))
    in_specs = [yk(0), yk(1), yk(2), yk(3),
                pl.BlockSpec((1, tl, 128), lambda bi, ti: (bi + b0, ti, 0)),
                pl.BlockSpec((1, tl, d), lambda bi, ti: (bi + b0, ti, 0)),
                pl.BlockSpec((1, 1, d), lambda bi, ti: (bsel(bi), 0, 5)),
                pl.BlockSpec((1, d), lambda bi, ti: (0, 0))]
    args = [yg, yg, yg, yg, probs, h, mods, norm_w.reshape(1, -1)]
    aliases = {5: 0}
    return pl.pallas_call(
        _moe_combine_body,
        grid=(nb, l // tl),
        in_specs=in_specs,
        out_specs=pl.BlockSpec((1, tl, d), lambda bi, ti: (bi + b0, ti, 0)),
        out_shape=jax.ShapeDtypeStruct((b, l, d), F32),
        input_output_aliases=aliases,
        compiler_params=pltpu.CompilerParams(
            dimension_semantics=("arbitrary", "arbitrary"), vmem_limit_bytes=V7X_VMEM_LIMIT_BYTES),
        name="moe_combine",
    )(*args)


def _moe_route(top_idx):
    t = top_idx.shape[0]
    a = t * TOP_K
    tm = MOE_TILE_M
    eid = top_idx.reshape(a)
    order = jnp.argsort(eid, stable=True).astype(jnp.int32)
    inv = jnp.argsort(order).astype(jnp.int32)
    counts = jnp.sum(jax.nn.one_hot(eid, N_EXPERTS, dtype=jnp.int32), axis=0)
    off = jnp.cumsum(counts) - counts
    pcounts = ((counts + tm - 1) // tm) * tm
    pend = jnp.cumsum(pcounts)
    poff = pend - pcounts
    n_tiles = a // tm + N_EXPERTS
    tile_start = jnp.arange(n_tiles, dtype=jnp.int32) * tm
    n_done = jnp.sum((tile_start[:, None] >= pend[None, :]).astype(jnp.int32), axis=1)
    tile_e = jnp.minimum(n_done, N_EXPERTS - 1)
    tile_ok = (tile_start < pend[-1]).astype(jnp.int32)
    ppos = jnp.arange(n_tiles * tm, dtype=jnp.int32)
    pe = jnp.repeat(tile_e, tm)
    r = ppos - poff[pe]
    src_rank = jnp.clip(off[pe] + jnp.minimum(r, counts[pe] - 1), 0, a - 1)
    src_tok = order[src_rank] // TOP_K
    pos = poff[eid] + (inv - off[eid])
    return src_tok, pos, tile_e, tile_ok


def _moe_layer(streams, layer, wg, bg, wu, bu, wd, bd, norm_w):
    d = streams[0][0].shape[-1]
    outs = [s[3] for s in streams]
    for mb in range(MOE_BATCH_SLICES):
        cuts = [(mb * (s[0].shape[0] // MOE_BATCH_SLICES), s[0].shape[0] // MOE_BATCH_SLICES) for s in streams]
        t_all = jnp.concatenate([s[0][b0:b0 + nb].reshape(-1, d) for s, (b0, nb) in zip(streams, cuts)], axis=0)
        idx_all = jnp.concatenate([s[1][b0:b0 + nb].reshape(-1, 128)[:, :TOP_K]
                                   for s, (b0, nb) in zip(streams, cuts)], axis=0)
        src_tok, pos, tile_e, tile_ok = _moe_route(idx_all)
        xs = t_all.at[src_tok].get(mode="promise_in_bounds")
        ys = _moe_grouped_ffn(xs, tile_e, tile_ok, layer, wg, bg, wu, bu, wd, bd)
        pos_k = pos.reshape(-1, TOP_K).T
        start = 0
        for si, ((t, _, probs, _, mods, tl), (b0, nb)) in enumerate(zip(streams, cuts)):
            l = t.shape[1]
            n = nb * l
            yg = ys.at[pos_k[:, start:start + n].reshape(-1)].get(mode="promise_in_bounds")
            outs[si] = _moe_combine(yg.reshape(TOP_K, nb, l, d), probs, outs[si], mods, norm_w, tl, b0)
            start += n
    return outs


def _k_rms(t, w):
    return t * lax.rsqrt(jnp.mean(t * t, axis=-1, keepdims=True) + RMS_EPS) * w


def _k_post_mix(y, h, gate, post_w, pre_w, shift, scale, rw, rb):
    h1 = h + gate * _k_rms(y, post_w)
    t = _k_rms(h1, pre_w) * (1.0 + scale) + shift
    logits = jnp.dot(t, rw, preferred_element_type=F32) + rb
    lane = lax.broadcasted_iota(jnp.int32, logits.shape, 1)
    idx_out = jnp.zeros(logits.shape, jnp.int32)
    val_out = jnp.zeros(logits.shape, F32)
    work = logits
    m0 = None
    for k in range(TOP_K):
        m = jnp.max(work, axis=-1, keepdims=True)
        sel = jnp.min(jnp.where(work == m, lane, 128), axis=-1, keepdims=True)
        if k == 0:
            m0 = m
        idx_out = jnp.where(lane == k, sel, idx_out)
        val_out = jnp.where(lane == k, jnp.exp(m - m0), val_out)
        work = jnp.where(lane == sel, -jnp.inf, work)
    probs = val_out / jnp.sum(val_out, axis=-1, keepdims=True)
    return h1, t, idx_out, probs


S5_BATCH_SUB = 8
S5_CH_BLOCK = 128
S5_ST_BLOCK = (S5_CH_BLOCK // S5_GROUP) * S5_STATE
S5_TL = 256
S5_GLU_TL = 64


def _s5_inproj_body(h_ref, sh_ref, sc_ref, nw_ref, w_ref, u_ref, il_s):
    tl = h_ref.shape[1]
    ncb = il_s.shape[0]
    for b in range(S5_BATCH_SUB):
        mb = b if sh_ref.shape[0] > 1 else 0
        a = _k_rms(h_ref[b], nw_ref[...]) * (1.0 + sc_ref[mb]) + sh_ref[mb]
        u = jnp.dot(a.astype(BF16), w_ref[...], preferred_element_type=F32)
        for cb in range(ncb):
            il_s[cb, pl.ds(b, tl, stride=S5_BATCH_SUB), :] = u[:, cb * 128:(cb + 1) * 128]
    for cb in range(ncb):
        u_ref[0, :, cb * 128:(cb + 1) * 128] = il_s[cb]


def _s5_inproj(h, mods, norm_w, w_bf16, tl):
    b, l, d = h.shape
    nbg = b // S5_BATCH_SUB
    nm = S5_BATCH_SUB if mods.shape[0] > 1 else 1
    msel = (lambda gi: gi) if mods.shape[0] > 1 else (lambda gi: 0)
    return pl.pallas_call(
        _s5_inproj_body,
        grid=(nbg, l // tl),
        in_specs=[
            pl.BlockSpec((S5_BATCH_SUB, tl, d), lambda gi, ti: (gi, ti, 0)),
            pl.BlockSpec((nm, 1, d), lambda gi, ti: (msel(gi), 0, 0)),
            pl.BlockSpec((nm, 1, d), lambda gi, ti: (msel(gi), 0, 1)),
            pl.BlockSpec((1, d), lambda gi, ti: (0, 0)),
            pl.BlockSpec((d, d), lambda gi, ti: (0, 0)),
        ],
        out_specs=pl.BlockSpec((1, tl * S5_BATCH_SUB, d), lambda gi, ti: (gi, ti, 0)),
        out_shape=jax.ShapeDtypeStruct((nbg, l * S5_BATCH_SUB, d), F32),
        scratch_shapes=[pltpu.VMEM((d // 128, tl * S5_BATCH_SUB, 128), F32)],
        compiler_params=pltpu.CompilerParams(
            dimension_semantics=("arbitrary", "arbitrary"), vmem_limit_bytes=V7X_VMEM_LIMIT_BYTES),
        name="s5_inproj",
    )(h, mods, mods, norm_w.reshape(1, d), w_bf16)


def _s5_scan_body(uc_ref, ul_ref, b_ref, c_ref, lam_ref, y_ref, buf_ref, st_ref):
    d = pl.program_id(0)
    tc = pl.program_id(3)
    tl = ul_ref.shape[1]
    rows = tl * S5_BATCH_SUB
    ns = S5_ST_BLOCK

    @pl.when(tc == 0)
    def _():
        st_ref[...] = jnp.zeros(st_ref.shape, F32)
        u2 = uc_ref[0].reshape(rows, S5_CH_BLOCK).astype(BF16)
        buf_ref[...] = jnp.dot(u2, b_ref[0, 0], preferred_element_type=F32)

    @pl.when(tc > 0)
    def _():
        u2 = ul_ref[0].reshape(rows, S5_CH_BLOCK).astype(BF16)
        buf_ref[...] = jnp.dot(u2, b_ref[0, 0], preferred_element_type=F32)

    lam = lam_ref[0, 0]
    lr = jnp.broadcast_to(lam[0:1], (S5_BATCH_SUB, ns))
    li = jnp.broadcast_to(lam[1:2], (S5_BATCH_SUB, ns))

    def step(i, carry):
        xr, xi = carry
        t = jnp.where(d == 0, i, tl - 1 - i)
        r0 = pl.multiple_of(t * S5_BATCH_SUB, S5_BATCH_SUB)
        br = buf_ref[pl.ds(r0, S5_BATCH_SUB), 0:ns]
        bi = buf_ref[pl.ds(r0, S5_BATCH_SUB), ns:2 * ns]
        nr = lr * xr - li * xi + br
        ni = lr * xi + li * xr + bi
        buf_ref[pl.ds(r0, S5_BATCH_SUB), 0:ns] = nr
        buf_ref[pl.ds(r0, S5_BATCH_SUB), ns:2 * ns] = ni
        return nr, ni

    xr, xi = lax.fori_loop(0, tl, step, (st_ref[0], st_ref[1]), unroll=8)
    st_ref[0] = xr
    st_ref[1] = xi

    @pl.when(tc > 0)
    def _():
        y = jnp.dot(buf_ref[...].astype(BF16), c_ref[0, 0], preferred_element_type=F32)
        y_ref[0, 0] = y.reshape(tl, S5_BATCH_SUB, S5_CH_BLOCK)


def _s5_scan(u_ctx, u_lat, bblk, cblk, lam):
    nbg, l_lat, _, d = u_lat.shape
    assert u_ctx.shape[1] == S5_TL and l_lat % S5_TL == 0
    n_lc = l_lat // S5_TL
    n_sb = d // S5_CH_BLOCK

    def lat_chunk(di, tc):
        j = jnp.maximum(tc - 1, 0)
        return jnp.where(di == 0, j, n_lc - 1 - j)

    return pl.pallas_call(
        _s5_scan_body,
        grid=(2, nbg, n_sb, n_lc + 1),
        in_specs=[
            pl.BlockSpec((1, S5_TL, S5_BATCH_SUB, S5_CH_BLOCK), lambda di, bg, sb, tc: (bg, 0, 0, sb)),
            pl.BlockSpec((1, S5_TL, S5_BATCH_SUB, S5_CH_BLOCK), lambda di, bg, sb, tc: (bg, lat_chunk(di, tc), 0, sb)),
            pl.BlockSpec((1, 1, S5_CH_BLOCK, 2 * S5_ST_BLOCK), lambda di, bg, sb, tc: (di, sb, 0, 0)),
            pl.BlockSpec((1, 1, 2 * S5_ST_BLOCK, S5_CH_BLOCK), lambda di, bg, sb, tc: (di, sb, 0, 0)),
            pl.BlockSpec((1, 1, 2, S5_ST_BLOCK), lambda di, bg, sb, tc: (di, sb, 0, 0)),
        ],
        out_specs=pl.BlockSpec((1, 1, S5_TL, S5_BATCH_SUB, S5_CH_BLOCK),
                               lambda di, bg, sb, tc: (di, bg, lat_chunk(di, tc), 0, sb)),
        out_shape=jax.ShapeDtypeStruct((2, nbg, l_lat, S5_BATCH_SUB, d), F32),
        scratch_shapes=[
            pltpu.VMEM((S5_TL * S5_BATCH_SUB, 2 * S5_ST_BLOCK), F32),
            pltpu.VMEM((2, S5_BATCH_SUB, S5_ST_BLOCK), F32),
        ],
        compiler_params=pltpu.CompilerParams(
            dimension_semantics=("arbitrary", "arbitrary", "arbitrary", "arbitrary"),
            vmem_limit_bytes=V7X_VMEM_LIMIT_BYTES),
        name="s5_scan",
    )(u_ctx, u_lat, bblk, cblk, lam)


def _s5_glu_body(y0_ref, y1_ref, u_ref, h_ref, gate_ref, sh_ref, sc_ref, dsk_ref, wa_ref, wb_ref,
                 postw_ref, prew_ref, rw_ref, rb_ref, h1_ref, t_ref, idx_ref, p_ref, ym_s):
    tl = h_ref.shape[1]
    y = dsk_ref[...] * u_ref[0] + y0_ref[0, 0] + y1_ref[0, 0]
    o = (0.5 * y * (1.0 + jnp.tanh(math.sqrt(2.0 / math.pi) * (y + 0.044715 * (y * y * y))))).astype(BF16)
    ga = jnp.dot(o, wa_ref[...], preferred_element_type=F32)
    gb = jnp.dot(o, wb_ref[...], preferred_element_type=F32)
    ym_all = ga * jax.nn.sigmoid(gb)
    ncb = ym_s.shape[0]
    for cb in range(ncb):
        ym_s[cb] = ym_all[:, cb * 128:(cb + 1) * 128]
    for b in range(S5_BATCH_SUB):
        ym = jnp.concatenate([ym_s[cb, pl.ds(b, tl, stride=S5_BATCH_SUB), :] for cb in range(ncb)],
                             axis=-1)
        h1, t, idx, probs = _k_post_mix(ym, h_ref[b], gate_ref[b], postw_ref[...], prew_ref[...],
                                        sh_ref[b], sc_ref[b], rw_ref[...], rb_ref[...])
        h1_ref[b] = h1
        t_ref[b] = t.astype(BF16)
        idx_ref[b] = idx
        p_ref[b] = probs


def _s5_glu_postmix(y, u, h, mods, d_skip, wa, wb, post_w, pre_w, rw_pad, rb_pad, tl):
    b, l, d = h.shape
    nbg = b // S5_BATCH_SUB
    nsub = S5_BATCH_SUB
    row = lambda v: v.reshape(1, -1)
    full = lambda shape: pl.BlockSpec(shape, lambda gi, ti: (0,) * len(shape))
    mod = lambda k: pl.BlockSpec((nsub, 1, d), lambda gi, ti: (gi, 0, k))
    tok = lambda w: pl.BlockSpec((nsub, tl, w), lambda gi, ti: (gi, ti, 0))
    return pl.pallas_call(
        _s5_glu_body,
        grid=(nbg, l // tl),
        in_specs=[
            pl.BlockSpec((1, 1, tl * nsub, d), lambda gi, ti: (0, gi, ti, 0)),
            pl.BlockSpec((1, 1, tl * nsub, d), lambda gi, ti: (1, gi, ti, 0)),
            pl.BlockSpec((1, tl * nsub, d), lambda gi, ti: (gi, ti, 0)),
            tok(d), mod(2), mod(3), mod(4),
            full((1, d)), full((d, d)), full((d, d)), full((1, d)), full((1, d)), full((d, 128)), full((1, 128)),
        ],
        out_specs=[tok(d), tok(d), tok(128), tok(128)],
        scratch_shapes=[pltpu.VMEM((d // 128, tl * nsub, 128), F32)],
        out_shape=[
            jax.ShapeDtypeStruct((b, l, d), F32),
            jax.ShapeDtypeStruct((b, l, d), BF16),
            jax.ShapeDtypeStruct((b, l, 128), jnp.int32),
            jax.ShapeDtypeStruct((b, l, 128), F32),
        ],
        compiler_params=pltpu.CompilerParams(
            dimension_semantics=("arbitrary", "arbitrary"), vmem_limit_bytes=V7X_VMEM_LIMIT_BYTES),
        name="s5_glu_postmix",
    )(y, y, u, h, mods, mods, mods, row(d_skip), wa, wb, row(post_w), row(pre_w), rw_pad, rb_pad)


def _s5_block_params(a_re, a_im, log_step, b_re, b_im, c_re, c_im):
    gpb = S5_CH_BLOCK // S5_GROUP
    eye = jnp.eye(gpb, dtype=F32)
    bblks, cblks, lams = [], [], []
    for di in range(2):
        lam_re, lam_im, bb_re, bb_im = _s5_discretize(a_re[di], a_im[di], log_step[di], b_re[di], b_im[di])
        n_sb = lam_re.shape[0] // gpb

        def bdiag_in(bb):
            t = bb.reshape(n_sb, gpb, S5_STATE, S5_GROUP)
            return jnp.einsum('sgph,gk->sghkp', t, eye).reshape(n_sb, S5_CH_BLOCK, S5_ST_BLOCK)

        def bdiag_out(cc):
            t = cc.reshape(n_sb, gpb, S5_GROUP, S5_STATE)
            return jnp.einsum('sghp,gk->sgpkh', t, eye).reshape(n_sb, S5_ST_BLOCK, S5_CH_BLOCK)

        bblks.append(jnp.concatenate([bdiag_in(bb_re), bdiag_in(bb_im)], axis=-1))
        cblks.append(jnp.concatenate([bdiag_out(c_re[di]), -bdiag_out(c_im[di])], axis=-2))
        lams.append(jnp.stack([lam_re.reshape(n_sb, S5_ST_BLOCK), lam_im.reshape(n_sb, S5_ST_BLOCK)], axis=1))
    return jnp.stack(bblks).astype(BF16), jnp.stack(cblks).astype(BF16), jnp.stack(lams)


def _pad_router(router_w, router_b):
    e = router_w.shape[1]
    rw = jnp.pad(router_w, ((0, 0), (0, 128 - e)))
    rb = jnp.pad(router_b, (0, 128 - e), constant_values=-1e30).reshape(1, 128)
    return rw, rb


CONV_CH = GDN_CONV_CH + SSD_CONV_CH
Z_CH = GDN_V_WIDTH + SSD_INNER
IN0_TILE = 256
IN0_CONV_TILES = CONV_CH // IN0_TILE
IN0_Z_TILES = Z_CH // IN0_TILE
IN0_TILES = IN0_CONV_TILES + IN0_Z_TILES + 1
CONV_ROW_CHUNK = 256


def _conv_halo(cols):
    return ((cols + 1 + 7) // 8) * 8


def _in0_body(h_ref, sh_ref, sc_ref, nw_ref, w_ref, cw_ref, cb_ref, gp_ref,
              conv_ref, z_ref, g_ref, gt_ref, a_s, p0_s, pm_s, pp_s, *, n_rows, n_cols):
    n = pl.program_id(1)
    l = h_ref.shape[1]
    halo = _conv_halo(n_cols)
    rc = min(CONV_ROW_CHUNK, l)

    @pl.when(n == 0)
    def _():
        def norm_rows(i, carry):
            r0 = pl.multiple_of(i * rc, rc)
            a = _k_rms(h_ref[0, pl.ds(r0, rc), :], nw_ref[...]) * (1.0 + sc_ref[0]) + sh_ref[0]
            a_s[pl.ds(r0, rc), :] = a.astype(BF16)
            return carry
        lax.fori_loop(0, l // rc, norm_rows, 0)

    p = jnp.dot(a_s[...], w_ref[...], preferred_element_type=F32)

    @pl.when(n < IN0_CONV_TILES)
    def _():
        zero_halo = jnp.zeros((halo, IN0_TILE), F32)
        for s in (p0_s, pm_s, pp_s):
            s[pl.ds(0, halo), :] = zero_halo
            s[pl.ds(halo + l, halo), :] = zero_halo
        p0_s[pl.ds(halo, l), :] = p
        tcol = lax.broadcasted_iota(jnp.int32, (l, IN0_TILE), 0) % n_cols
        pm_s[pl.ds(halo, l), :] = jnp.where(tcol != 0, p0_s[pl.ds(halo - 1, l), :], 0.0)
        pp_s[pl.ds(halo, l), :] = jnp.where(tcol != n_cols - 1, p0_s[pl.ds(halo + 1, l), :], 0.0)
        dys = (0,) if n_rows == 1 else (-1, 0, 1)

        def conv_rows(i, carry):
            r0 = pl.multiple_of(i * rc, rc)
            acc = jnp.zeros((rc, IN0_TILE), F32) + cb_ref[...]
            for dy in dys:
                base = halo + dy * n_cols
                for dx, src in ((0, pm_s), (1, p0_s), (2, pp_s)):
                    tap = (dy + 1) * 3 + dx
                    acc = acc + cw_ref[tap:tap + 1, :] * src[pl.ds(r0 + base, rc), :]
            conv_ref[0, pl.ds(r0, rc), :] = (acc * jax.nn.sigmoid(acc)).astype(conv_ref.dtype)
            return carry
        lax.fori_loop(0, l // rc, conv_rows, 0)

    @pl.when((n >= IN0_CONV_TILES) & (n < IN0_CONV_TILES + IN0_Z_TILES))
    def _():
        z_ref[0] = (p * jax.nn.sigmoid(p)).astype(z_ref.dtype)

    @pl.when(n == IN0_TILES - 1)
    def _():
        pg = p[:, :128]
        lane = lax.broadcasted_iota(jnp.int32, pg.shape, 1)
        xb = pg + gp_ref[0:1, :]
        sp = jnp.maximum(xb, 0.0) + jnp.log(1.0 + jnp.exp(-jnp.abs(xb)))
        neg_a_sp = -jnp.exp(gp_ref[1:2, :]) * sp
        gates = jnp.where(lane < 2 * GDN_HEADS, neg_a_sp,
                          jnp.where(lane < GATE_COL_DT, jax.nn.sigmoid(pg),
                                    jnp.where(lane < GATE_COL_DA, sp,
                                              jnp.where(lane < GATE_COL_DA + 2 * SSD_HEADS, neg_a_sp, 0.0))))
        g_ref[0] = gates
        for ci in range(l // SCAN_C):
            gt_ref[0, ci] = gates[ci * SCAN_C:(ci + 1) * SCAN_C, :].T


def _in0_features(h, mods, norm_w, w_all, conv_w, conv_b, gate_params, n_rows, n_cols):
    b, l, d = h.shape
    nb = mods.shape[0]
    bsel = (lambda bi: bi) if nb > 1 else (lambda bi: 0)
    halo = _conv_halo(n_cols)
    nct, nzt = IN0_CONV_TILES, IN0_Z_TILES
    pad_rows = l + 2 * halo
    return pl.pallas_call(
        functools.partial(_in0_body, n_rows=n_rows, n_cols=n_cols),
        grid=(b, IN0_TILES),
        in_specs=[
            pl.BlockSpec((1, l, d), lambda bi, ni: (bi, 0, 0)),
            pl.BlockSpec((1, 1, d), lambda bi, ni: (bsel(bi), 0, 0)),
            pl.BlockSpec((1, 1, d), lambda bi, ni: (bsel(bi), 0, 1)),
            pl.BlockSpec((1, d), lambda bi, ni: (0, 0)),
            pl.BlockSpec((d, IN0_TILE), lambda bi, ni: (0, ni)),
            pl.BlockSpec((9, IN0_TILE), lambda bi, ni: (0, jnp.minimum(ni, nct - 1))),
            pl.BlockSpec((1, IN0_TILE), lambda bi, ni: (0, jnp.minimum(ni, nct - 1))),
            pl.BlockSpec((2, 128), lambda bi, ni: (0, 0)),
        ],
        out_specs=[
            pl.BlockSpec((1, l, IN0_TILE), lambda bi, ni: (bi, 0, jnp.minimum(ni, nct - 1))),
            pl.BlockSpec((1, l, IN0_TILE), lambda bi, ni: (bi, 0, jnp.clip(ni - nct, 0, nzt - 1))),
            pl.BlockSpec((1, l, 128), lambda bi, ni: (bi, 0, 0)),
            pl.BlockSpec((1, l // SCAN_C, 128, SCAN_C), lambda bi, ni: (bi, 0, 0, 0)),
        ],
        out_shape=[
            jax.ShapeDtypeStruct((b, l, CONV_CH), BF16),
            jax.ShapeDtypeStruct((b, l, Z_CH), BF16),
            jax.ShapeDtypeStruct((b, l, 128), F32),
            jax.ShapeDtypeStruct((b, l // SCAN_C, 128, SCAN_C), F32),
        ],
        scratch_shapes=[
            pltpu.VMEM((l, d), BF16),
            pltpu.VMEM((pad_rows, IN0_TILE), F32),
            pltpu.VMEM((pad_rows, IN0_TILE), F32),
            pltpu.VMEM((pad_rows, IN0_TILE), F32),
        ],
        compiler_params=pltpu.CompilerParams(
            dimension_semantics=("arbitrary", "arbitrary"), vmem_limit_bytes=V7X_VMEM_LIMIT_BYTES),
        name="in0_features",
    )(h, mods, mods, norm_w.reshape(1, d), w_all, conv_w, conv_b, gate_params)


def _in0_params(w_in, gdn_conv_w, ssd_conv_w, ssd_conv_b, gdn_a_log, gdn_dt_bias, ssd_a_log, ssd_dt_bias):
    splits = np.cumsum(IN0_SIZES)[:-1].tolist()
    w_ca, w_cb, w_za, w_zb, w_a, w_b, w_dt = jnp.split(w_in, splits, axis=1)
    d = w_in.shape[0]
    w_gate = jnp.concatenate([w_a, w_b, w_dt, w_dt], axis=1)
    w_gate = jnp.pad(w_gate, ((0, 0), (0, IN0_TILE - w_gate.shape[1])))
    w_all = jnp.concatenate([w_ca, w_cb, w_za, w_zb, w_gate], axis=1).astype(BF16)
    conv_w = jnp.concatenate([gdn_conv_w, ssd_conv_w], axis=-1).reshape(9, CONV_CH)
    conv_b = jnp.concatenate([jnp.zeros((GDN_CONV_CH,), F32), ssd_conv_b]).reshape(1, CONV_CH)
    z8 = jnp.zeros((2 * GDN_HEADS,), F32)
    z16 = jnp.zeros((2 * SSD_HEADS,), F32)
    tail = jnp.zeros((128 - GATE_COL_DA - 2 * SSD_HEADS,), F32)
    bias = jnp.concatenate([gdn_dt_bias.reshape(-1), z8, ssd_dt_bias.reshape(-1), ssd_dt_bias.reshape(-1), tail])
    alog = jnp.concatenate([gdn_a_log.reshape(-1), z8, z16, ssd_a_log.reshape(-1), tail])
    return w_all, conv_w, conv_b, jnp.stack([bias, alog])


def _merge0_body(o_ref, y_ref, xs_ref, z_ref, h_ref, gate_ref, sh_ref, sc_ref, gnw_ref, dsk_ref, snw_ref, wo_ref,
                 postw_ref, prew_ref, rw_ref, rb_ref, h1_ref, t_ref, idx_ref, p_ref):
    z = z_ref[0].astype(F32)
    parts = []
    for hd in range(GDN_HEADS):
        cs = slice(hd * GDN_DV, (hd + 1) * GDN_DV)
        parts.append(_k_rms(o_ref[0, :, cs], gnw_ref[...]) * z[:, cs])
    gw = SSD_INNER // SSD_GROUPS
    for g in range(SSD_GROUPS):
        cs = slice(g * gw, (g + 1) * gw)
        y2 = (y_ref[0, :, cs] + dsk_ref[:, cs] * xs_ref[0, :, cs].astype(F32)) * z[:, GDN_V_WIDTH + g * gw:GDN_V_WIDTH + (g + 1) * gw]
        parts.append(_k_rms(y2, snw_ref[:, cs]))
    mixed = jnp.concatenate(parts, axis=-1).astype(BF16)
    ym = jnp.dot(mixed, wo_ref[...], preferred_element_type=F32)
    h1, t, idx, probs = _k_post_mix(ym, h_ref[0], gate_ref[0], postw_ref[...], prew_ref[...],
                                    sh_ref[0], sc_ref[0], rw_ref[...], rb_ref[...])
    h1_ref[0] = h1
    t_ref[0] = t.astype(BF16)
    idx_ref[0] = idx
    p_ref[0] = probs


def _merge0_postmix(o, y, conv, z, h, mods, gdn_norm_w, ssd_d, ssd_norm_w, w_out, post_w, pre_w, rw_pad, rb_pad, tl):
    b, l, d = h.shape
    nb = mods.shape[0]
    bsel = (lambda bi: bi) if nb > 1 else (lambda bi: 0)
    row = lambda v: v.reshape(1, -1)
    full = lambda shape: pl.BlockSpec(shape, lambda bi, ti: (0,) * len(shape))
    mod = lambda k: pl.BlockSpec((1, 1, d), lambda bi, ti: (bsel(bi), 0, k))
    tok = lambda w: pl.BlockSpec((1, tl, w), lambda bi, ti: (bi, ti, 0))
    xs_block = GDN_CONV_CH // SSD_INNER
    dsk = jnp.repeat(ssd_d, SSD_HEADDIM)
    return pl.pallas_call(
        _merge0_body,
        grid=(b, l // tl),
        in_specs=[
            tok(GDN_V_WIDTH), tok(SSD_INNER),
            pl.BlockSpec((1, tl, SSD_INNER), lambda bi, ti: (bi, ti, xs_block)),
            tok(Z_CH), tok(d), mod(2), mod(3), mod(4),
            full((1, GDN_DV)), full((1, SSD_INNER)), full((1, SSD_INNER)), full((Z_CH, d)),
            full((1, d)), full((1, d)), full((d, 128)), full((1, 128)),
        ],
        out_specs=[tok(d), tok(d), tok(128), tok(128)],
        out_shape=[
            jax.ShapeDtypeStruct((b, l, d), F32),
            jax.ShapeDtypeStruct((b, l, d), BF16),
            jax.ShapeDtypeStruct((b, l, 128), jnp.int32),
            jax.ShapeDtypeStruct((b, l, 128), F32),
        ],
        compiler_params=pltpu.CompilerParams(
            dimension_semantics=("arbitrary", "arbitrary"), vmem_limit_bytes=V7X_VMEM_LIMIT_BYTES),
        name="merge0_postmix",
    )(o, y, conv, z, h, mods, mods, mods, row(gdn_norm_w), row(dsk), row(ssd_norm_w), w_out,
      row(post_w), row(pre_w), rw_pad, rb_pad)


SCAN_C = 128
GDN_INV_BLOCK = 16


def _dot(a, b):
    return jnp.dot(a, b, preferred_element_type=F32)


def _dot_nt(a, b):
    return lax.dot_general(a, b, (((1,), (1,)), ((), ())), preferred_element_type=F32)


def _dot_tn(a, b):
    return lax.dot_general(a, b, (((0,), (0,)), ((), ())), preferred_element_type=F32)


def _bdot(a, b):
    return _dot(a.astype(BF16), b.astype(BF16))


def _bdot_nt(a, b):
    return _dot_nt(a.astype(BF16), b.astype(BF16))


def _bdot_tn(a, b):
    return _dot_tn(a.astype(BF16), b.astype(BF16))


def _scan_masks(fwd):
    row = lax.broadcasted_iota(jnp.int32, (SCAN_C, SCAN_C), 0)
    col = lax.broadcasted_iota(jnp.int32, (SCAN_C, SCAN_C), 1)
    lead = (row - col) * jnp.where(fwd, 1, -1)
    return row, col, lead >= 0, lead <= 0, lead > 0


def _cumsum_col_row(g_col, g_row, incl, incl_t):
    gc_col = jnp.sum(jnp.where(incl, g_row, 0.0), axis=1, keepdims=True)
    gc_row = jnp.sum(jnp.where(incl_t, g_col, 0.0), axis=0, keepdims=True)
    return gc_col, gc_row


def _unit_tri_inverse(ms, row, col):
    eye = (row == col).astype(F32)
    same = (row // GDN_INV_BLOCK) == (col // GDN_INV_BLOCK)
    mds = [jnp.where(same, m, 0.0) for m in ms]
    mos = [m - md for m, md in zip(ms, mds)]
    ps = mds
    tds = [eye - md for md in mds]
    for _ in range(int(math.log2(GDN_INV_BLOCK)) - 1):
        ps = [_dot(p, p) for p in ps]
        tds = [td + _dot(td, p) for td, p in zip(tds, ps)]
    ps = [_dot(td, mo) for td, mo in zip(tds, mos)]
    qs = [eye - n for n in ps]
    for _ in range(int(math.log2(SCAN_C // GDN_INV_BLOCK)) - 1):
        ps = [_dot(p, p) for p in ps]
        qs = [q + _dot(q, p) for q, p in zip(qs, ps)]
    return [_dot(q, td) for q, td in zip(qs, tds)]


GDN_PREP_GROUP = 8
GDN_HEADS_PER_STEP = 2


def _gdn_body(ql_ref, kl_ref, vl_ref, gl_ref, gtl_ref, qc_ref, kc_ref, vc_ref, gc_ref, gtc_ref,
              ol_ref, oc_ref, u_s, w_s, qk_s, qd_s, kd_s, dec_s, obl_s, obc_s):
    hp = pl.program_id(1)
    nh = GDN_HEADS_PER_STEP
    c = SCAN_C
    ncc = qc_ref.shape[1] // c
    ncl = ql_ref.shape[1] // c
    grp = math.gcd(ncl, GDN_PREP_GROUP)
    lane = lax.broadcasted_iota(jnp.int32, (c, 128), 1)

    def order(d, i, n):
        return i if d == 0 else n - 1 - i

    def prep(s, d, q_ref, k_ref, v_ref, g_ref, gt_ref, cis, ps):
        row, col, incl, incl_t, strict = _scan_masks(d == 0)
        col_g = d * GDN_HEADS + hp * nh + s
        col_b = 2 * GDN_HEADS + col_g
        hs = slice(s * 128, (s + 1) * 128)
        cid = s * 2 + d
        loaded = []
        for ci in cis:
            t0 = pl.multiple_of(ci * c, c)
            loaded.append((q_ref[0, pl.ds(t0, c), hs].astype(F32), k_ref[0, pl.ds(t0, c), hs].astype(F32),
                           v_ref[0, pl.ds(t0, c), hs].astype(F32), g_ref[0, pl.ds(t0, c), :],
                           gt_ref[0, ci, pl.ds(col_g, 1), :]))
        parts, ms = [], []
        for q, k, v, gch, g_row in loaded:
            q = q * lax.rsqrt(jnp.sum(q * q, axis=-1, keepdims=True) + RMS_EPS) * (GDN_DK ** -0.5)
            k = k * lax.rsqrt(jnp.sum(k * k, axis=-1, keepdims=True) + RMS_EPS)
            g_col = jnp.sum(jnp.where(lane == col_g, gch, 0.0), axis=1, keepdims=True)
            b_col = jnp.sum(jnp.where(lane == col_b, gch, 0.0), axis=1, keepdims=True)
            gc_col, gc_row = _cumsum_col_row(g_col, g_row, incl, incl_t)
            g_tot = jnp.sum(g_col, axis=0, keepdims=True)
            decay = jnp.where(incl, jnp.exp(jnp.where(incl, gc_col - gc_row, 0.0)), 0.0)
            kb = k * b_col
            ms.append(jnp.where(strict, _bdot_nt(kb, k) * decay, 0.0))
            egc = jnp.exp(gc_col)
            parts.append((v * b_col, kb * egc, _bdot_nt(q, k) * decay, q * egc, k * jnp.exp(g_tot - gc_col),
                          jnp.broadcast_to(jnp.exp(g_tot), (1, 128))))
        t_invs = _unit_tri_inverse(ms, row, col)
        uw = [(_bdot(t_inv, part[0]), _bdot(t_inv, part[1])) for t_inv, part in zip(t_invs, parts)]
        for p, (u, w), (_, _, qk, qd, kd, dec) in zip(ps, uw, parts):
            p0 = pl.multiple_of(p * c, c)
            u_s[cid, pl.ds(p0, c), :] = u
            w_s[cid, pl.ds(p0, c), :] = w
            qk_s[cid, pl.ds(p0, c), :] = qk
            qd_s[cid, pl.ds(p0, c), :] = qd
            kd_s[cid, pl.ds(p0, c), :] = kd
            dec_s[cid, pl.ds(p, 1), :] = dec

    def advance(of_ref, ob_ref, i, n, p, states):
        p0 = pl.multiple_of(p * c, c)
        outs, new_states = [], []
        for cid, st in enumerate(states):
            v_new = u_s[cid, pl.ds(p0, c), :] - _bdot(w_s[cid, pl.ds(p0, c), :], st)
            outs.append(_bdot(qd_s[cid, pl.ds(p0, c), :], st) + _bdot(qk_s[cid, pl.ds(p0, c), :], v_new))
            new_states.append(st * dec_s[cid, pl.ds(p, 1), :] + _bdot_tn(kd_s[cid, pl.ds(p0, c), :], v_new))
        for s in range(nh):
            of_ref[0, pl.ds(pl.multiple_of(order(0, i, n) * c, c), c), s * 128:(s + 1) * 128] = outs[s * 2]
            ob_ref[s, pl.ds(pl.multiple_of(order(1, i, n) * c, c), c), :] = outs[s * 2 + 1]
        return tuple(new_states)

    for s, d in [(s, d) for s in range(nh) for d in range(2)]:
        prep(s, d, qc_ref, kc_ref, vc_ref, gc_ref, gtc_ref, [order(d, i, ncc) for i in range(ncc)],
             list(range(ncc)))

        def prep_lat(gi, carry, s=s, d=d):
            base = gi * grp
            prep(s, d, ql_ref, kl_ref, vl_ref, gl_ref, gtl_ref,
                 [order(d, base + j, ncl) for j in range(grp)], [ncc + base + j for j in range(grp)])
            return carry

        lax.fori_loop(0, ncl // grp, prep_lat, 0)

    states = tuple(jnp.zeros((GDN_DK, GDN_DV), F32) for _ in range(2 * nh))
    for i in range(ncc):
        states = advance(oc_ref, obc_s, i, ncc, i, states)
    lax.fori_loop(0, ncl, lambda i, st: advance(ol_ref, obl_s, i, ncl, ncc + i, st), states)
    for s in range(nh):
        hs = slice(s * 128, (s + 1) * 128)
        ol_ref[0, :, hs] = ol_ref[0, :, hs] + obl_s[s]
        oc_ref[0, :, hs] = oc_ref[0, :, hs] + obc_s[s]


SSD_R = SSD_HEADS // SSD_GROUPS
SSD_PAIRS = SSD_R * SSD_HEADDIM // 128
GATE_COL_DT = 4 * GDN_HEADS
GATE_COL_DA = GATE_COL_DT + 2 * SSD_HEADS


def _ssd_body(xl_ref, bl_ref, cl_ref, gl_ref, gtl_ref, xc_ref, bc_ref, cc_ref, gc_ref, gtc_ref,
              yl_ref, yc_ref, st_ref):
    g = pl.program_id(1)
    d = pl.program_id(2)
    fwd = d == 0
    c = SCAN_C
    ncc = xc_ref.shape[1] // c
    ncl = xl_ref.shape[1] // c
    _, _, incl, incl_t, _ = _scan_masks(fwd)
    lane = lax.broadcasted_iota(jnp.int32, (c, 128), 1)
    low = lane < SSD_HEADDIM

    @pl.when(fwd)
    def _():
        yl_ref[...] = jnp.zeros(yl_ref.shape, F32)
        yc_ref[...] = jnp.zeros(yc_ref.shape, F32)

    st_ref[...] = jnp.zeros(st_ref.shape, F32)

    def chunk(x_ref, b_ref, c_ref, g_ref, gt_ref, y_ref, ci):
        t0 = pl.multiple_of(ci * c, c)
        bm = b_ref[0, pl.ds(t0, c), :].astype(BF16)
        cm = c_ref[0, pl.ds(t0, c), :].astype(BF16)
        gch = g_ref[0, pl.ds(t0, c), :]
        cb = _dot_nt(cm, bm)
        for pr in range(SSD_PAIRS):
            per_head = []
            for s in range(2):
                hh = g * SSD_R + 2 * pr + s
                col_dt = GATE_COL_DT + d * SSD_HEADS + hh
                col_da = GATE_COL_DA + d * SSD_HEADS + hh
                dt_col = jnp.sum(jnp.where(lane == col_dt, gch, 0.0), axis=1, keepdims=True)
                da_col = jnp.sum(jnp.where(lane == col_da, gch, 0.0), axis=1, keepdims=True)
                da_row = gt_ref[0, ci, pl.ds(col_da, 1), :]
                acs_col, acs_row = _cumsum_col_row(da_col, da_row, incl, incl_t)
                a_tot = jnp.sum(da_col, axis=0, keepdims=True)
                lmat = jnp.where(incl, jnp.exp(jnp.where(incl, acs_col - acs_row, 0.0)), 0.0)
                per_head.append((dt_col, acs_col, a_tot, (cb * lmat).astype(BF16)))
            pick = lambda k: jnp.where(low, per_head[0][k], per_head[1][k])
            x = x_ref[0, pl.ds(t0, c), pr * 128:(pr + 1) * 128].astype(F32)
            xdt = x * pick(0)
            acs = pick(1)
            a_tot = pick(2)
            xdt_b = xdt.astype(BF16)
            y_diag = jnp.where(low, _dot(per_head[0][3], xdt_b), _dot(per_head[1][3], xdt_b))
            st = st_ref[pr]
            y_off = _dot(cm, st.astype(BF16)) * jnp.exp(acs)
            y_ref[0, pl.ds(t0, c), pr * 128:(pr + 1) * 128] += y_diag + y_off
            st_ref[pr] = st * jnp.exp(a_tot) + _dot_tn(bm, (xdt * jnp.exp(a_tot - acs)).astype(BF16))

    def order(i, n):
        return jnp.where(fwd, i, n - 1 - i)

    for i in range(ncc):
        chunk(xc_ref, bc_ref, cc_ref, gc_ref, gtc_ref, yc_ref, order(i, ncc))

    def lat(i, carry):
        chunk(xl_ref, bl_ref, cl_ref, gl_ref, gtl_ref, yl_ref, order(i, ncl))
        return carry

    lax.fori_loop(0, ncl, lat, 0)


def _ssd_scan(xbc_l, g_l, gt_l, xbc_c, g_c, gt_c, col0=0):
    b, l, _ = xbc_l.shape
    lc = xbc_c.shape[1]
    gw = SSD_R * SSD_HEADDIM
    x0 = col0 // gw
    nxb = (col0 + SSD_INNER) // 128

    def stream(n):
        return [pl.BlockSpec((1, n, gw), lambda bi, gi, di: (bi, 0, x0 + gi)),
                pl.BlockSpec((1, n, 128), lambda bi, gi, di: (bi, 0, nxb + gi)),
                pl.BlockSpec((1, n, 128), lambda bi, gi, di: (bi, 0, nxb + SSD_GROUPS + gi)),
                pl.BlockSpec((1, n, 128), lambda bi, gi, di: (bi, 0, 0)),
                pl.BlockSpec((1, n // SCAN_C, 128, SCAN_C), lambda bi, gi, di: (bi, 0, 0, 0))]

    return pl.pallas_call(
        _ssd_body,
        grid=(b, SSD_GROUPS, 2),
        in_specs=stream(l) + stream(lc),
        out_specs=[pl.BlockSpec((1, l, gw), lambda bi, gi, di: (bi, 0, gi)),
                   pl.BlockSpec((1, lc, gw), lambda bi, gi, di: (bi, 0, gi))],
        out_shape=[jax.ShapeDtypeStruct((b, l, SSD_INNER), F32),
                   jax.ShapeDtypeStruct((b, lc, SSD_INNER), F32)],
        scratch_shapes=[pltpu.VMEM((SSD_PAIRS, SSD_STATE, 128), F32)],
        compiler_params=pltpu.CompilerParams(
            dimension_semantics=("arbitrary", "arbitrary", "arbitrary"), vmem_limit_bytes=V7X_VMEM_LIMIT_BYTES),
        name="ssd_scan",
    )(xbc_l, xbc_l, xbc_l, g_l, gt_l, xbc_c, xbc_c, xbc_c, g_c, gt_c)


def _chunk_rows(gt):
    b, w, l = gt.shape
    return gt.reshape(b, w, l // SCAN_C, SCAN_C).transpose(0, 2, 1, 3)


def _gdn_scan(qkv_l, g_l, gt_l, qkv_c, g_c, gt_c):
    b, l, _ = qkv_l.shape
    lc = qkv_c.shape[1]
    hd = GDN_HEADS

    nh = GDN_HEADS_PER_STEP
    steps = hd // nh
    hw = nh * 128

    def stream(n):
        tok = lambda off: pl.BlockSpec((1, n, hw), lambda bi, hi: (bi, 0, off + hi))
        return [tok(0), tok(steps), tok(2 * steps),
                pl.BlockSpec((1, n, 128), lambda bi, hi: (bi, 0, 0)),
                pl.BlockSpec((1, n // SCAN_C, 128, SCAN_C), lambda bi, hi: (bi, 0, 0, 0))]

    nt = l + lc
    return pl.pallas_call(
        _gdn_body,
        grid=(b, steps),
        in_specs=stream(l) + stream(lc),
        out_specs=[pl.BlockSpec((1, l, hw), lambda bi, hi: (bi, 0, hi)),
                   pl.BlockSpec((1, lc, hw), lambda bi, hi: (bi, 0, hi))],
        out_shape=[jax.ShapeDtypeStruct((b, l, hd * GDN_DV), F32),
                   jax.ShapeDtypeStruct((b, lc, hd * GDN_DV), F32)],
        scratch_shapes=[pltpu.VMEM((2 * nh, nt, 128), F32) for _ in range(5)]
        + [pltpu.VMEM((2 * nh, nt // SCAN_C, 128), F32), pltpu.VMEM((nh, l, 128), F32),
           pltpu.VMEM((nh, lc, 128), F32)],
        compiler_params=pltpu.CompilerParams(
            dimension_semantics=("arbitrary", "arbitrary"), vmem_limit_bytes=V7X_VMEM_LIMIT_BYTES),
        name="gdn_scan",
    )(qkv_l, qkv_l, qkv_l, g_l, gt_l, qkv_c, qkv_c, qkv_c, g_c, gt_c)


def _ada_body(c_ref, w_ref, b_ref, o_ref):
    cc = c_ref[...]
    o_ref[0] = jnp.dot(cc * jax.nn.sigmoid(cc), w_ref[0], preferred_element_type=F32) + b_ref[0]


def _ada_mods(cond, ada_w, ada_b):
    r, d = cond.shape
    depth, _, w6 = ada_w.shape
    rp = ((r + 7) // 8) * 8
    cond = jnp.pad(cond, ((0, rp - r), (0, 0)))
    out = pl.pallas_call(
        _ada_body,
        grid=(depth, w6 // d),
        in_specs=[pl.BlockSpec((rp, d), lambda li, ni: (0, 0)),
                  pl.BlockSpec((1, d, d), lambda li, ni: (li, 0, ni)),
                  pl.BlockSpec((1, 1, d), lambda li, ni: (li, 0, ni))],
        out_specs=pl.BlockSpec((1, rp, d), lambda li, ni: (li, 0, ni)),
        out_shape=jax.ShapeDtypeStruct((depth, rp, w6), F32),
        compiler_params=pltpu.CompilerParams(dimension_semantics=("arbitrary", "arbitrary")),
        name="ada_mods",
    )(cond, ada_w, ada_b.reshape(depth, 1, w6))
    return out[:, :r]


def _s5_discretize(a_re, a_im, log_step, b_re, b_im):
    step = jnp.exp(log_step)[:, None]
    mag = jnp.exp(a_re * step)
    lam_re, lam_im = mag * jnp.cos(a_im * step), mag * jnp.sin(a_im * step)
    den = a_re * a_re + a_im * a_im
    f_re = ((lam_re - 1.0) * a_re + lam_im * a_im) / den
    f_im = (lam_im * a_re - (lam_re - 1.0) * a_im) / den
    bb_re = f_re[..., None] * b_re - f_im[..., None] * b_im
    bb_im = f_re[..., None] * b_im + f_im[..., None] * b_re
    return lam_re, lam_im, bb_re, bb_im


def kernel(x, c, ctx, c_ctx, ada_w, ada_b, mix_norm_pre, mix_norm_post, ffn_norm_pre, ffn_norm_post, router_w, router_b, moe_w_gate, moe_b_gate, moe_w_up, moe_b_up, moe_w_down, moe_b_down, hy_w_in, gdn_conv_w, gdn_a_log, gdn_dt_bias, gdn_norm_w, ssd_conv_w, ssd_conv_b, ssd_a_log, ssd_dt_bias, ssd_d, ssd_norm_w, hy_w_out, s5_w_in, s5_a_re, s5_a_im, s5_log_step, s5_b_re, s5_b_im, s5_c_re, s5_c_im, s5_d, s5_w_glu_a, s5_w_glu_b):
    depth = ada_w.shape[0]
    rows = x.shape[1] // GRID_W
    h_lat, h_ctx = x, ctx
    nb = c.shape[0]
    mods_all = _ada_mods(jnp.concatenate([c, c_ctx[None, :]], axis=0), ada_w, ada_b)
    for i in range(depth):
        j = i // 2
        need_ctx = i < depth - 1
        mods_lat = mods_all[i, :nb, None, :]
        mods_ctx = mods_all[i, nb:nb + 1, None, :]
        ffn = (i, moe_w_gate, moe_b_gate, moe_w_up, moe_b_up, moe_w_down, moe_b_down, ffn_norm_post[i])
        rw_pad, rb_pad = _pad_router(router_w[i], router_b[i])
        bsz, l, d = h_lat.shape
        lc = h_ctx.shape[1]
        tl_lat, tl_ctx = 512, lc
        if i % 2 == 1:
            assert not need_ctx
            nbg = bsz // S5_BATCH_SUB
            w_in = s5_w_in[j].astype(BF16)
            u_lat = _s5_inproj(h_lat, mods_lat, mix_norm_pre[i], w_in, S5_TL)
            u_ctx = _s5_inproj(h_ctx, mods_ctx, mix_norm_pre[i], w_in, S5_TL)
            bblk, cblk, lam = _s5_block_params(s5_a_re[j], s5_a_im[j], s5_log_step[j], s5_b_re[j], s5_b_im[j],
                                               s5_c_re[j], s5_c_im[j])
            y = _s5_scan(u_ctx.reshape(nbg, lc, S5_BATCH_SUB, d), u_lat.reshape(nbg, l, S5_BATCH_SUB, d),
                         bblk, cblk, lam)
            h1, t_lat, idx, probs = _s5_glu_postmix(
                y.reshape(2, nbg, l * S5_BATCH_SUB, d), u_lat, h_lat, mods_lat, s5_d[j],
                s5_w_glu_a[j].astype(BF16), s5_w_glu_b[j].astype(BF16),
                mix_norm_post[i], ffn_norm_pre[i], rw_pad, rb_pad, S5_GLU_TL)
            (h_lat,) = _moe_layer([(t_lat, idx, probs, h1, mods_lat, tl_lat)], *ffn)
            continue
        w_all, conv_w, conv_b, gate_params = _in0_params(hy_w_in[j], gdn_conv_w[j], ssd_conv_w[j], ssd_conv_b[j],
                                                         gdn_a_log[j], gdn_dt_bias[j], ssd_a_log[j], ssd_dt_bias[j])
        conv_l, z_l, g_l, gt_l = _in0_features(h_lat, mods_lat, mix_norm_pre[i], w_all, conv_w, conv_b, gate_params,
                                               rows, GRID_W)
        conv_c, z_c, g_c, gt_c = _in0_features(h_ctx, mods_ctx, mix_norm_pre[i], w_all, conv_w, conv_b, gate_params,
                                               1, lc)
        o_l, o_c = _gdn_scan(conv_l, g_l, gt_l, conv_c, g_c, gt_c)
        y_l, y_c = _ssd_scan(conv_l, g_l, gt_l, conv_c, g_c, gt_c, col0=GDN_CONV_CH)
        w_out = hy_w_out[j].astype(BF16)
        merge = lambda o, y, conv, z, h, mods, tl: _merge0_postmix(
            o, y, conv, z, h, mods, gdn_norm_w[j], ssd_d[j], ssd_norm_w[j], w_out,
            mix_norm_post[i], ffn_norm_pre[i], rw_pad, rb_pad, tl)
        streams = [merge(o_l, y_l, conv_l, z_l, h_lat, mods_lat, tl_lat) + (mods_lat, tl_lat)]
        if need_ctx:
            streams.append(merge(o_c, y_c, conv_c, z_c, h_ctx, mods_ctx, tl_ctx) + (mods_ctx, tl_ctx))
        outs = _moe_layer([(t, idx, probs, h1, mods, tl) for h1, t, idx, probs, mods, tl in streams], *ffn)
        h_lat = outs[0]
        if need_ctx:
            h_ctx = outs[1]
    return h_lat
```

```python
import functools
import math

import jax
import jax.numpy as jnp
import numpy as np
from jax import lax
from jax.experimental import pallas as pl
from jax.experimental.pallas import tpu as pltpu

F32 = jnp.float32
BF16 = jnp.bfloat16

D_MODEL = 1024
GRID_W = 64
RMS_EPS = 1e-6

GDN_HEADS = 4
GDN_DK = 128
GDN_DV = 128
GDN_CHUNK = 64
SSD_HEADS = 8
SSD_HEADDIM = 64
SSD_GROUPS = 2
SSD_STATE = 128
SSD_CHUNK = 128
S5_GROUP = 16
S5_GROUPS = D_MODEL // S5_GROUP
S5_STATE = 64
N_EXPERTS = 32
TOP_K = 4
SWIGLU_LIMIT = 7.0
SWIGLU_ALPHA = 1.702

GDN_V_WIDTH = GDN_HEADS * GDN_DV
SSD_INNER = SSD_HEADS * SSD_HEADDIM
GDN_CONV_CH = 2 * GDN_HEADS * GDN_DK + GDN_V_WIDTH
SSD_CONV_CH = SSD_INNER + 2 * SSD_GROUPS * SSD_STATE
IN0_SIZES = (GDN_CONV_CH, SSD_CONV_CH, GDN_V_WIDTH, SSD_INNER, 2 * GDN_HEADS, 2 * GDN_HEADS, 2 * SSD_HEADS)

V7X_VMEM_LIMIT_BYTES = 56 * 1024 * 1024
MOE_TILE_M = 512
MOE_TILE_F = 512
MOE_BATCH_SLICES = 2


def _moe_ffn_body(tile_e_ref, tile_ok_ref, x_ref, wg_ref, bg_ref, wu_ref, bu_ref, wd_ref, bd_ref, o_ref,
                  wg_s, wu_s, wd_s):
    i = pl.program_id(0)
    n_f = wg_ref.shape[3] // MOE_TILE_F

    @pl.when((i == 0) | (tile_e_ref[i] != tile_e_ref[jnp.maximum(i - 1, 0)]))
    def _():
        for c in range(n_f):
            cs = slice(c * MOE_TILE_F, (c + 1) * MOE_TILE_F)
            wg_s[:, cs] = wg_ref[0, 0, :, cs].astype(BF16)
            wu_s[:, cs] = wu_ref[0, 0, :, cs].astype(BF16)
            wd_s[cs, :] = wd_ref[0, 0, cs, :].astype(BF16)

    @pl.when(tile_ok_ref[i] > 0)
    def _():
        x = x_ref[...]
        acc = jnp.zeros(o_ref.shape, F32)
        for c in range(n_f):
            cs = slice(c * MOE_TILE_F, (c + 1) * MOE_TILE_F)
            gl = jnp.dot(x, wg_s[:, cs], preferred_element_type=F32) + bg_ref[0, 0, :, cs]
            lin = jnp.dot(x, wu_s[:, cs], preferred_element_type=F32) + bu_ref[0, 0, :, cs]
            gl = jnp.minimum(gl, SWIGLU_LIMIT)
            lin = jnp.clip(lin, -SWIGLU_LIMIT, SWIGLU_LIMIT)
            act = gl * jax.nn.sigmoid(SWIGLU_ALPHA * gl) * (lin + 1.0)
            acc = acc + jnp.dot(act.astype(BF16), wd_s[cs, :], preferred_element_type=F32)
        o_ref[...] = (acc + bd_ref[0, 0]).astype(o_ref.dtype)

    @pl.when(tile_ok_ref[i] == 0)
    def _():
        o_ref[...] = jnp.zeros(o_ref.shape, o_ref.dtype)


def _moe_grouped_ffn(xs, tile_e, tile_ok, layer, wg, bg, wu, bu, wd, bd):
    p, d = xs.shape
    nl, e, _, f = wg.shape
    n_tiles = p // MOE_TILE_M
    grid_spec = pltpu.PrefetchScalarGridSpec(
        num_scalar_prefetch=2,
        grid=(n_tiles,),
        in_specs=[
            pl.BlockSpec((MOE_TILE_M, d), lambda i, te, ok: (i, 0)),
            pl.BlockSpec((1, 1, d, f), lambda i, te, ok: (layer, te[i], 0, 0)),
            pl.BlockSpec((1, 1, 1, f), lambda i, te, ok: (layer, te[i], 0, 0)),
            pl.BlockSpec((1, 1, d, f), lambda i, te, ok: (layer, te[i], 0, 0)),
            pl.BlockSpec((1, 1, 1, f), lambda i, te, ok: (layer, te[i], 0, 0)),
            pl.BlockSpec((1, 1, f, d), lambda i, te, ok: (layer, te[i], 0, 0)),
            pl.BlockSpec((1, 1, 1, d), lambda i, te, ok: (layer, te[i], 0, 0)),
        ],
        out_specs=pl.BlockSpec((MOE_TILE_M, d), lambda i, te, ok: (i, 0)),
        scratch_shapes=[pltpu.VMEM((d, f), BF16), pltpu.VMEM((d, f), BF16), pltpu.VMEM((f, d), BF16)],
    )
    return pl.pallas_call(
        _moe_ffn_body,
        grid_spec=grid_spec,
        out_shape=jax.ShapeDtypeStruct((p, d), BF16),
        compiler_params=pltpu.CompilerParams(
            dimension_semantics=("arbitrary",), vmem_limit_bytes=V7X_VMEM_LIMIT_BYTES),
        name="moe_grouped_ffn",
    )(tile_e, tile_ok, xs, wg, bg.reshape(nl, e, 1, f), wu, bu.reshape(nl, e, 1, f), wd, bd.reshape(nl, e, 1, d))


def _moe_combine_body(y0_ref, y1_ref, y2_ref, y3_ref, p_ref, h_ref, gate_ref, nw_ref, o_ref):
    p = p_ref[0]
    acc = jnp.zeros(o_ref.shape[1:], F32)
    for k, y_ref in enumerate((y0_ref, y1_ref, y2_ref, y3_ref)):
        acc = acc + p[:, k:k + 1] * y_ref[0, 0].astype(F32)
    o_ref[0] = h_ref[0] + gate_ref[0] * _k_rms(acc, nw_ref[...])


def _moe_combine(yg, probs, h, mods, norm_w, tl, b0):
    b, l, d = h.shape
    nb = yg.shape[1]
    bsel = (lambda bi: bi + b0) if mods.shape[0] > 1 else (lambda bi: 0)
    yk = lambda k: pl.BlockSpec((1, 1, tl, d), lambda bi, ti: (k, bi, ti, 0))
    in_specs = [yk(0), yk(1), yk(2), yk(3),
                pl.BlockSpec((1, tl, 128), lambda bi, ti: (bi + b0, ti, 0)),
                pl.BlockSpec((1, tl, d), lambda bi, ti: (bi + b0, ti, 0)),
                pl.BlockSpec((1, 1, d), lambda bi, ti: (bsel(bi), 0, 5)),
                pl.BlockSpec((1, d), lambda bi, ti: (0, 0))]
    args = [yg, yg, yg, yg, probs, h, mods, norm_w.reshape(1, -1)]
    aliases = {5: 0}
    return pl.pallas_call(
        _moe_combine_body,
        grid=(nb, l // tl),
        in_specs=in_specs,
        out_specs=pl.BlockSpec((1, tl, d), lambda bi, ti: (bi + b0, ti, 0)),
        out_shape=jax.ShapeDtypeStruct((b, l, d), F32),
        input_output_aliases=aliases,
        compiler_params=pltpu.CompilerParams(
            dimension_semantics=("arbitrary", "arbitrary"), vmem_limit_bytes=V7X_VMEM_LIMIT_BYTES),
        name="moe_combine",
    )(*args)


def _moe_route(top_idx):
    t = top_idx.shape[0]
    a = t * TOP_K
    tm = MOE_TILE_M
    eid = top_idx.reshape(a)
    order = jnp.argsort(eid, stable=True).astype(jnp.int32)
    inv = jnp.argsort(order).astype(jnp.int32)
    counts = jnp.sum(jax.nn.one_hot(eid, N_EXPERTS, dtype=jnp.int32), axis=0)
    off = jnp.cumsum(counts) - counts
    pcounts = ((counts + tm - 1) // tm) * tm
    pend = jnp.cumsum(pcounts)
    poff = pend - pcounts
    n_tiles = a // tm + N_EXPERTS
    tile_start = jnp.arange(n_tiles, dtype=jnp.int32) * tm
    n_done = jnp.sum((tile_start[:, None] >= pend[None, :]).astype(jnp.int32), axis=1)
    tile_e = jnp.minimum(n_done, N_EXPERTS - 1)
    tile_ok = (tile_start < pend[-1]).astype(jnp.int32)
    ppos = jnp.arange(n_tiles * tm, dtype=jnp.int32)
    pe = jnp.repeat(tile_e, tm)
    r = ppos - poff[pe]
    src_rank = jnp.clip(off[pe] + jnp.minimum(r, counts[pe] - 1), 0, a - 1)
    src_tok = order[src_rank] // TOP_K
    pos = poff[eid] + (inv - off[eid])
    return src_tok, pos, tile_e, tile_ok


def _moe_layer(streams, layer, wg, bg, wu, bu, wd, bd, norm_w):
    d = streams[0][0].shape[-1]
    outs = [s[3] for s in streams]
    for mb in range(MOE_BATCH_SLICES):
        cuts = [(mb * (s[0].shape[0] // MOE_BATCH_SLICES), s[0].shape[0] // MOE_BATCH_SLICES) for s in streams]
        t_all = jnp.concatenate([s[0][b0:b0 + nb].reshape(-1, d) for s, (b0, nb) in zip(streams, cuts)], axis=0)
        idx_all = jnp.concatenate([s[1][b0:b0 + nb].reshape(-1, 128)[:, :TOP_K]
                                   for s, (b0, nb) in zip(streams, cuts)], axis=0)
        src_tok, pos, tile_e, tile_ok = _moe_route(idx_all)
        xs = t_all.at[src_tok].get(mode="promise_in_bounds")
        ys = _moe_grouped_ffn(xs, tile_e, tile_ok, layer, wg, bg, wu, bu, wd, bd)
        pos_k = pos.reshape(-1, TOP_K).T
        start = 0
        for si, ((t, _, probs, _, mods, tl), (b0, nb)) in enumerate(zip(streams, cuts)):
            l = t.shape[1]
            n = nb * l
            yg = ys.at[pos_k[:, start:start + n].reshape(-1)].get(mode="promise_in_bounds")
            outs[si] = _moe_combine(yg.reshape(TOP_K, nb, l, d), probs, outs[si], mods, norm_w, tl, b0)
            start += n
    return outs


def _k_rms(t, w):
    return t * lax.rsqrt(jnp.mean(t * t, axis=-1, keepdims=True) + RMS_EPS) * w


def _k_post_mix(y, h, gate, post_w, pre_w, shift, scale, rw, rb):
    h1 = h + gate * _k_rms(y, post_w)
    t = _k_rms(h1, pre_w) * (1.0 + scale) + shift
    logits = jnp.dot(t.astype(BF16), rw.astype(BF16), preferred_element_type=F32) + rb
    lane = lax.broadcasted_iota(jnp.int32, logits.shape, 1)
    idx_out = jnp.zeros(logits.shape, jnp.int32)
    val_out = jnp.zeros(logits.shape, F32)
    work = logits
    m0 = None
    for k in range(TOP_K):
        m = jnp.max(work, axis=-1, keepdims=True)
        sel = jnp.min(jnp.where(work == m, lane, 128), axis=-1, keepdims=True)
        if k == 0:
            m0 = m
        idx_out = jnp.where(lane == k, sel, idx_out)
        val_out = jnp.where(lane == k, jnp.exp(m - m0), val_out)
        work = jnp.where(lane == sel, -jnp.inf, work)
    probs = val_out / jnp.sum(val_out, axis=-1, keepdims=True)
    return h1, t, idx_out, probs


S5_BATCH_SUB = 8
S5_CH_BLOCK = 128
S5_ST_BLOCK = (S5_CH_BLOCK // S5_GROUP) * S5_STATE
S5_TL = 256
S5_GLU_TL = 64


def _s5_inproj_body(h_ref, sh_ref, sc_ref, nw_ref, w_ref, u_ref, il_s):
    tl = h_ref.shape[1]
    ncb = il_s.shape[0]
    for b in range(S5_BATCH_SUB):
        mb = b if sh_ref.shape[0] > 1 else 0
        a = _k_rms(h_ref[b], nw_ref[...]) * (1.0 + sc_ref[mb]) + sh_ref[mb]
        u = jnp.dot(a.astype(BF16), w_ref[...], preferred_element_type=F32)
        for cb in range(ncb):
            il_s[cb, pl.ds(b, tl, stride=S5_BATCH_SUB), :] = u[:, cb * 128:(cb + 1) * 128]
    for cb in range(ncb):
        u_ref[0, :, cb * 128:(cb + 1) * 128] = il_s[cb]


def _s5_inproj(h, mods, norm_w, w_bf16, tl):
    b, l, d = h.shape
    nbg = b // S5_BATCH_SUB
    nm = S5_BATCH_SUB if mods.shape[0] > 1 else 1
    msel = (lambda gi: gi) if mods.shape[0] > 1 else (lambda gi: 0)
    return pl.pallas_call(
        _s5_inproj_body,
        grid=(nbg, l // tl),
        in_specs=[
            pl.BlockSpec((S5_BATCH_SUB, tl, d), lambda gi, ti: (gi, ti, 0)),
            pl.BlockSpec((nm, 1, d), lambda gi, ti: (msel(gi), 0, 0)),
            pl.BlockSpec((nm, 1, d), lambda gi, ti: (msel(gi), 0, 1)),
            pl.BlockSpec((1, d), lambda gi, ti: (0, 0)),
            pl.BlockSpec((d, d), lambda gi, ti: (0, 0)),
        ],
        out_specs=pl.BlockSpec((1, tl * S5_BATCH_SUB, d), lambda gi, ti: (gi, ti, 0)),
        out_shape=jax.ShapeDtypeStruct((nbg, l * S5_BATCH_SUB, d), F32),
        scratch_shapes=[pltpu.VMEM((d // 128, tl * S5_BATCH_SUB, 128), F32)],
        compiler_params=pltpu.CompilerParams(
            dimension_semantics=("arbitrary", "arbitrary"), vmem_limit_bytes=V7X_VMEM_LIMIT_BYTES),
        name="s5_inproj",
    )(h, mods, mods, norm_w.reshape(1, d), w_bf16)


def _s5_scan_body(uc_ref, ul_ref, b_ref, c_ref, lam_ref, y_ref, buf_ref, st_ref):
    d = pl.program_id(0)
    tc = pl.program_id(3)
    tl = ul_ref.shape[1]
    rows = tl * S5_BATCH_SUB
    ns = S5_ST_BLOCK

    @pl.when(tc == 0)
    def _():
        st_ref[...] = jnp.zeros(st_ref.shape, F32)
        u2 = uc_ref[0].reshape(rows, S5_CH_BLOCK).astype(BF16)
        buf_ref[...] = jnp.dot(u2, b_ref[0, 0], preferred_element_type=F32)

    @pl.when(tc > 0)
    def _():
        u2 = ul_ref[0].reshape(rows, S5_CH_BLOCK).astype(BF16)
        buf_ref[...] = jnp.dot(u2, b_ref[0, 0], preferred_element_type=F32)

    lam = lam_ref[0, 0]
    lr = jnp.broadcast_to(lam[0:1], (S5_BATCH_SUB, ns))
    li = jnp.broadcast_to(lam[1:2], (S5_BATCH_SUB, ns))

    def step(i, carry):
        xr, xi = carry
        t = jnp.where(d == 0, i, tl - 1 - i)
        r0 = pl.multiple_of(t * S5_BATCH_SUB, S5_BATCH_SUB)
        br = buf_ref[pl.ds(r0, S5_BATCH_SUB), 0:ns]
        bi = buf_ref[pl.ds(r0, S5_BATCH_SUB), ns:2 * ns]
        nr = lr * xr - li * xi + br
        ni = lr * xi + li * xr + bi
        buf_ref[pl.ds(r0, S5_BATCH_SUB), 0:ns] = nr
        buf_ref[pl.ds(r0, S5_BATCH_SUB), ns:2 * ns] = ni
        return nr, ni

    xr, xi = lax.fori_loop(0, tl, step, (st_ref[0], st_ref[1]), unroll=8)
    st_ref[0] = xr
    st_ref[1] = xi

    @pl.when(tc > 0)
    def _():
        y = jnp.dot(buf_ref[...].astype(BF16), c_ref[0, 0], preferred_element_type=F32)
        y_ref[0, 0] = y.reshape(tl, S5_BATCH_SUB, S5_CH_BLOCK)


def _s5_scan(u_ctx, u_lat, bblk, cblk, lam):
    nbg, l_lat, _, d = u_lat.shape
    assert u_ctx.shape[1] == S5_TL and l_lat % S5_TL == 0
    n_lc = l_lat // S5_TL
    n_sb = d // S5_CH_BLOCK

    def lat_chunk(di, tc):
        j = jnp.maximum(tc - 1, 0)
        return jnp.where(di == 0, j, n_lc - 1 - j)

    return pl.pallas_call(
        _s5_scan_body,
        grid=(2, nbg, n_sb, n_lc + 1),
        in_specs=[
            pl.BlockSpec((1, S5_TL, S5_BATCH_SUB, S5_CH_BLOCK), lambda di, bg, sb, tc: (bg, 0, 0, sb)),
            pl.BlockSpec((1, S5_TL, S5_BATCH_SUB, S5_CH_BLOCK), lambda di, bg, sb, tc: (bg, lat_chunk(di, tc), 0, sb)),
            pl.BlockSpec((1, 1, S5_CH_BLOCK, 2 * S5_ST_BLOCK), lambda di, bg, sb, tc: (di, sb, 0, 0)),
            pl.BlockSpec((1, 1, 2 * S5_ST_BLOCK, S5_CH_BLOCK), lambda di, bg, sb, tc: (di, sb, 0, 0)),
            pl.BlockSpec((1, 1, 2, S5_ST_BLOCK), lambda di, bg, sb, tc: (di, sb, 0, 0)),
        ],
        out_specs=pl.BlockSpec((1, 1, S5_TL, S5_BATCH_SUB, S5_CH_BLOCK),
                               lambda di, bg, sb, tc: (di, bg, lat_chunk(di, tc), 0, sb)),
        out_shape=jax.ShapeDtypeStruct((2, nbg, l_lat, S5_BATCH_SUB, d), F32),
        scratch_shapes=[
            pltpu.VMEM((S5_TL * S5_BATCH_SUB, 2 * S5_ST_BLOCK), F32),
            pltpu.VMEM((2, S5_BATCH_SUB, S5_ST_BLOCK), F32),
        ],
        compiler_params=pltpu.CompilerParams(
            dimension_semantics=("arbitrary", "arbitrary", "arbitrary", "arbitrary"),
            vmem_limit_bytes=V7X_VMEM_LIMIT_BYTES),
        name="s5_scan",
    )(u_ctx, u_lat, bblk, cblk, lam)


def _s5_glu_body(y0_ref, y1_ref, u_ref, h_ref, gate_ref, sh_ref, sc_ref, dsk_ref, wa_ref, wb_ref,
                 postw_ref, prew_ref, rw_ref, rb_ref, h1_ref, t_ref, idx_ref, p_ref, ym_s):
    tl = h_ref.shape[1]
    y = dsk_ref[...] * u_ref[0] + y0_ref[0, 0] + y1_ref[0, 0]
    o = (0.5 * y * (1.0 + jnp.tanh(math.sqrt(2.0 / math.pi) * (y + 0.044715 * (y * y * y))))).astype(BF16)
    ga = jnp.dot(o, wa_ref[...], preferred_element_type=F32)
    gb = jnp.dot(o, wb_ref[...], preferred_element_type=F32)
    ym_all = ga * jax.nn.sigmoid(gb)
    ncb = ym_s.shape[0]
    for cb in range(ncb):
        ym_s[cb] = ym_all[:, cb * 128:(cb + 1) * 128]
    for b in range(S5_BATCH_SUB):
        ym = jnp.concatenate([ym_s[cb, pl.ds(b, tl, stride=S5_BATCH_SUB), :] for cb in range(ncb)],
                             axis=-1)
        h1, t, idx, probs = _k_post_mix(ym, h_ref[b], gate_ref[b], postw_ref[...], prew_ref[...],
                                        sh_ref[b], sc_ref[b], rw_ref[...], rb_ref[...])
        h1_ref[b] = h1
        t_ref[b] = t.astype(BF16)
        idx_ref[b] = idx
        p_ref[b] = probs


def _s5_glu_postmix(y, u, h, mods, d_skip, wa, wb, post_w, pre_w, rw_pad, rb_pad, tl):
    b, l, d = h.shape
    nbg = b // S5_BATCH_SUB
    nsub = S5_BATCH_SUB
    row = lambda v: v.reshape(1, -1)
    full = lambda shape: pl.BlockSpec(shape, lambda gi, ti: (0,) * len(shape))
    mod = lambda k: pl.BlockSpec((nsub, 1, d), lambda gi, ti: (gi, 0, k))
    tok = lambda w: pl.BlockSpec((nsub, tl, w), lambda gi, ti: (gi, ti, 0))
    return pl.pallas_call(
        _s5_glu_body,
        grid=(nbg, l // tl),
        in_specs=[
            pl.BlockSpec((1, 1, tl * nsub, d), lambda gi, ti: (0, gi, ti, 0)),
            pl.BlockSpec((1, 1, tl * nsub, d), lambda gi, ti: (1, gi, ti, 0)),
            pl.BlockSpec((1, tl * nsub, d), lambda gi, ti: (gi, ti, 0)),
            tok(d), mod(2), mod(3), mod(4),
            full((1, d)), full((d, d)), full((d, d)), full((1, d)), full((1, d)), full((d, 128)), full((1, 128)),
        ],
        out_specs=[tok(d), tok(d), tok(128), tok(128)],
        scratch_shapes=[pltpu.VMEM((d // 128, tl * nsub, 128), F32)],
        out_shape=[
            jax.ShapeDtypeStruct((b, l, d), F32),
            jax.ShapeDtypeStruct((b, l, d), BF16),
            jax.ShapeDtypeStruct((b, l, 128), jnp.int32),
            jax.ShapeDtypeStruct((b, l, 128), F32),
        ],
        compiler_params=pltpu.CompilerParams(
            dimension_semantics=("arbitrary", "arbitrary"), vmem_limit_bytes=V7X_VMEM_LIMIT_BYTES),
        name="s5_glu_postmix",
    )(y, y, u, h, mods, mods, mods, row(d_skip), wa, wb, row(post_w), row(pre_w), rw_pad, rb_pad)


def _s5_block_params(a_re, a_im, log_step, b_re, b_im, c_re, c_im):
    gpb = S5_CH_BLOCK // S5_GROUP
    eye = jnp.eye(gpb, dtype=F32)
    bblks, cblks, lams = [], [], []
    for di in range(2):
        lam_re, lam_im, bb_re, bb_im = _s5_discretize(a_re[di], a_im[di], log_step[di], b_re[di], b_im[di])
        n_sb = lam_re.shape[0] // gpb

        def bdiag_in(bb):
            t = bb.reshape(n_sb, gpb, S5_STATE, S5_GROUP)
            return jnp.einsum('sgph,gk->sghkp', t, eye).reshape(n_sb, S5_CH_BLOCK, S5_ST_BLOCK)

        def bdiag_out(cc):
            t = cc.reshape(n_sb, gpb, S5_GROUP, S5_STATE)
            return jnp.einsum('sghp,gk->sgpkh', t, eye).reshape(n_sb, S5_ST_BLOCK, S5_CH_BLOCK)

        bblks.append(jnp.concatenate([bdiag_in(bb_re), bdiag_in(bb_im)], axis=-1))
        cblks.append(jnp.concatenate([bdiag_out(c_re[di]), -bdiag_out(c_im[di])], axis=-2))
        lams.append(jnp.stack([lam_re.reshape(n_sb, S5_ST_BLOCK), lam_im.reshape(n_sb, S5_ST_BLOCK)], axis=1))
    return jnp.stack(bblks).astype(BF16), jnp.stack(cblks).astype(BF16), jnp.stack(lams)


def _pad_router(router_w, router_b):
    e = router_w.shape[1]
    rw = jnp.pad(router_w, ((0, 0), (0, 128 - e)))
    rb = jnp.pad(router_b, (0, 128 - e), constant_values=-1e30).reshape(1, 128)
    return rw, rb


CONV_CH = GDN_CONV_CH + SSD_CONV_CH
Z_CH = GDN_V_WIDTH + SSD_INNER
IN0_TILE = 256
IN0_CONV_TILES = CONV_CH // IN0_TILE
IN0_Z_TILES = Z_CH // IN0_TILE
IN0_TILES = IN0_CONV_TILES + IN0_Z_TILES + 1
CONV_ROW_CHUNK = 256


def _conv_halo(cols):
    return ((cols + 1 + 7) // 8) * 8


def _in0_body(h_ref, sh_ref, sc_ref, nw_ref, w_ref, cw_ref, cb_ref, gp_ref,
              conv_ref, z_ref, g_ref, gt_ref, a_s, p0_s, pm_s, pp_s, *, n_rows, n_cols):
    n = pl.program_id(1)
    l = h_ref.shape[1]
    halo = _conv_halo(n_cols)
    rc = min(CONV_ROW_CHUNK, l)

    @pl.when(n == 0)
    def _():
        def norm_rows(i, carry):
            r0 = pl.multiple_of(i * rc, rc)
            a = _k_rms(h_ref[0, pl.ds(r0, rc), :], nw_ref[...]) * (1.0 + sc_ref[0]) + sh_ref[0]
            a_s[pl.ds(r0, rc), :] = a.astype(BF16)
            return carry
        lax.fori_loop(0, l // rc, norm_rows, 0)

    p = jnp.dot(a_s[...], w_ref[...], preferred_element_type=F32)

    @pl.when(n < IN0_CONV_TILES)
    def _():
        zero_halo = jnp.zeros((halo, IN0_TILE), F32)
        for s in (p0_s, pm_s, pp_s):
            s[pl.ds(0, halo), :] = zero_halo
            s[pl.ds(halo + l, halo), :] = zero_halo
        p0_s[pl.ds(halo, l), :] = p
        tcol = lax.broadcasted_iota(jnp.int32, (l, IN0_TILE), 0) % n_cols
        pm_s[pl.ds(halo, l), :] = jnp.where(tcol != 0, p0_s[pl.ds(halo - 1, l), :], 0.0)
        pp_s[pl.ds(halo, l), :] = jnp.where(tcol != n_cols - 1, p0_s[pl.ds(halo + 1, l), :], 0.0)
        dys = (0,) if n_rows == 1 else (-1, 0, 1)

        def conv_rows(i, carry):
            r0 = pl.multiple_of(i * rc, rc)
            acc = jnp.zeros((rc, IN0_TILE), F32) + cb_ref[...]
            for dy in dys:
                base = halo + dy * n_cols
                for dx, src in ((0, pm_s), (1, p0_s), (2, pp_s)):
                    tap = (dy + 1) * 3 + dx
                    acc = acc + cw_ref[tap:tap + 1, :] * src[pl.ds(r0 + base, rc), :]
            conv_ref[0, pl.ds(r0, rc), :] = (acc * jax.nn.sigmoid(acc)).astype(conv_ref.dtype)
            return carry
        lax.fori_loop(0, l // rc, conv_rows, 0)

    @pl.when((n >= IN0_CONV_TILES) & (n < IN0_CONV_TILES + IN0_Z_TILES))
    def _():
        z_ref[0] = (p * jax.nn.sigmoid(p)).astype(z_ref.dtype)

    @pl.when(n == IN0_TILES - 1)
    def _():
        pg = p[:, :128]
        lane = lax.broadcasted_iota(jnp.int32, pg.shape, 1)
        xb = pg + gp_ref[0:1, :]
        sp = jnp.maximum(xb, 0.0) + jnp.log(1.0 + jnp.exp(-jnp.abs(xb)))
        neg_a_sp = -jnp.exp(gp_ref[1:2, :]) * sp
        gates = jnp.where(lane < 2 * GDN_HEADS, neg_a_sp,
                          jnp.where(lane < GATE_COL_DT, jax.nn.sigmoid(pg),
                                    jnp.where(lane < GATE_COL_DA, sp,
                                              jnp.where(lane < GATE_COL_DA + 2 * SSD_HEADS, neg_a_sp, 0.0))))
        g_ref[0] = gates
        for ci in range(l // SCAN_C):
            gt_ref[0, ci] = gates[ci * SCAN_C:(ci + 1) * SCAN_C, :].T


def _in0_features(h, mods, norm_w, w_all, conv_w, conv_b, gate_params, n_rows, n_cols):
    b, l, d = h.shape
    nb = mods.shape[0]
    bsel = (lambda bi: bi) if nb > 1 else (lambda bi: 0)
    halo = _conv_halo(n_cols)
    nct, nzt = IN0_CONV_TILES, IN0_Z_TILES
    pad_rows = l + 2 * halo
    return pl.pallas_call(
        functools.partial(_in0_body, n_rows=n_rows, n_cols=n_cols),
        grid=(b, IN0_TILES),
        in_specs=[
            pl.BlockSpec((1, l, d), lambda bi, ni: (bi, 0, 0)),
            pl.BlockSpec((1, 1, d), lambda bi, ni: (bsel(bi), 0, 0)),
            pl.BlockSpec((1, 1, d), lambda bi, ni: (bsel(bi), 0, 1)),
            pl.BlockSpec((1, d), lambda bi, ni: (0, 0)),
            pl.BlockSpec((d, IN0_TILE), lambda bi, ni: (0, ni)),
            pl.BlockSpec((9, IN0_TILE), lambda bi, ni: (0, jnp.minimum(ni, nct - 1))),
            pl.BlockSpec((1, IN0_TILE), lambda bi, ni: (0, jnp.minimum(ni, nct - 1))),
            pl.BlockSpec((2, 128), lambda bi, ni: (0, 0)),
        ],
        out_specs=[
            pl.BlockSpec((1, l, IN0_TILE), lambda bi, ni: (bi, 0, jnp.minimum(ni, nct - 1))),
            pl.BlockSpec((1, l, IN0_TILE), lambda bi, ni: (bi, 0, jnp.clip(ni - nct, 0, nzt - 1))),
            pl.BlockSpec((1, l, 128), lambda bi, ni: (bi, 0, 0)),
            pl.BlockSpec((1, l // SCAN_C, 128, SCAN_C), lambda bi, ni: (bi, 0, 0, 0)),
        ],
        out_shape=[
            jax.ShapeDtypeStruct((b, l, CONV_CH), BF16),
            jax.ShapeDtypeStruct((b, l, Z_CH), BF16),
            jax.ShapeDtypeStruct((b, l, 128), F32),
            jax.ShapeDtypeStruct((b, l // SCAN_C, 128, SCAN_C), F32),
        ],
        scratch_shapes=[
            pltpu.VMEM((l, d), BF16),
            pltpu.VMEM((pad_rows, IN0_TILE), F32),
            pltpu.VMEM((pad_rows, IN0_TILE), F32),
            pltpu.VMEM((pad_rows, IN0_TILE), F32),
        ],
        compiler_params=pltpu.CompilerParams(
            dimension_semantics=("arbitrary", "arbitrary"), vmem_limit_bytes=V7X_VMEM_LIMIT_BYTES),
        name="in0_features",
    )(h, mods, mods, norm_w.reshape(1, d), w_all, conv_w, conv_b, gate_params)


def _in0_params(w_in, gdn_conv_w, ssd_conv_w, ssd_conv_b, gdn_a_log, gdn_dt_bias, ssd_a_log, ssd_dt_bias):
    splits = np.cumsum(IN0_SIZES)[:-1].tolist()
    w_ca, w_cb, w_za, w_zb, w_a, w_b, w_dt = jnp.split(w_in, splits, axis=1)
    d = w_in.shape[0]
    w_gate = jnp.concatenate([w_a, w_b, w_dt, w_dt], axis=1)
    w_gate = jnp.pad(w_gate, ((0, 0), (0, IN0_TILE - w_gate.shape[1])))
    w_all = jnp.concatenate([w_ca, w_cb, w_za, w_zb, w_gate], axis=1).astype(BF16)
    conv_w = jnp.concatenate([gdn_conv_w, ssd_conv_w], axis=-1).reshape(9, CONV_CH)
    conv_b = jnp.concatenate([jnp.zeros((GDN_CONV_CH,), F32), ssd_conv_b]).reshape(1, CONV_CH)
    z8 = jnp.zeros((2 * GDN_HEADS,), F32)
    z16 = jnp.zeros((2 * SSD_HEADS,), F32)
    tail = jnp.zeros((128 - GATE_COL_DA - 2 * SSD_HEADS,), F32)
    bias = jnp.concatenate([gdn_dt_bias.reshape(-1), z8, ssd_dt_bias.reshape(-1), ssd_dt_bias.reshape(-1), tail])
    alog = jnp.concatenate([gdn_a_log.reshape(-1), z8, z16, ssd_a_log.reshape(-1), tail])
    return w_all, conv_w, conv_b, jnp.stack([bias, alog])


def _merge0_body(o_ref, y_ref, xs_ref, z_ref, h_ref, gate_ref, sh_ref, sc_ref, gnw_ref, dsk_ref, snw_ref, wo_ref,
                 postw_ref, prew_ref, rw_ref, rb_ref, h1_ref, t_ref, idx_ref, p_ref):
    z = z_ref[0].astype(F32)
    parts = []
    for hd in range(GDN_HEADS):
        cs = slice(hd * GDN_DV, (hd + 1) * GDN_DV)
        parts.append(_k_rms(o_ref[0, :, cs], gnw_ref[...]) * z[:, cs])
    gw = SSD_INNER // SSD_GROUPS
    for g in range(SSD_GROUPS):
        cs = slice(g * gw, (g + 1) * gw)
        y2 = (y_ref[0, :, cs] + dsk_ref[:, cs] * xs_ref[0, :, cs].astype(F32)) * z[:, GDN_V_WIDTH + g * gw:GDN_V_WIDTH + (g + 1) * gw]
        parts.append(_k_rms(y2, snw_ref[:, cs]))
    mixed = jnp.concatenate(parts, axis=-1).astype(BF16)
    ym = jnp.dot(mixed, wo_ref[...], preferred_element_type=F32)
    h1, t, idx, probs = _k_post_mix(ym, h_ref[0], gate_ref[0], postw_ref[...], prew_ref[...],
                                    sh_ref[0], sc_ref[0], rw_ref[...], rb_ref[...])
    h1_ref[0] = h1
    t_ref[0] = t.astype(BF16)
    idx_ref[0] = idx
    p_ref[0] = probs


def _merge0_postmix(o, y, conv, z, h, mods, gdn_norm_w, ssd_d, ssd_norm_w, w_out, post_w, pre_w, rw_pad, rb_pad, tl):
    b, l, d = h.shape
    nb = mods.shape[0]
    bsel = (lambda bi: bi) if nb > 1 else (lambda bi: 0)
    row = lambda v: v.reshape(1, -1)
    full = lambda shape: pl.BlockSpec(shape, lambda bi, ti: (0,) * len(shape))
    mod = lambda k: pl.BlockSpec((1, 1, d), lambda bi, ti: (bsel(bi), 0, k))
    tok = lambda w: pl.BlockSpec((1, tl, w), lambda bi, ti: (bi, ti, 0))
    xs_block = GDN_CONV_CH // SSD_INNER
    dsk = jnp.repeat(ssd_d, SSD_HEADDIM)
    return pl.pallas_call(
        _merge0_body,
        grid=(b, l // tl),
        in_specs=[
            tok(GDN_V_WIDTH), tok(SSD_INNER),
            pl.BlockSpec((1, tl, SSD_INNER), lambda bi, ti: (bi, ti, xs_block)),
            tok(Z_CH), tok(d), mod(2), mod(3), mod(4),
            full((1, GDN_DV)), full((1, SSD_INNER)), full((1, SSD_INNER)), full((Z_CH, d)),
            full((1, d)), full((1, d)), full((d, 128)), full((1, 128)),
        ],
        out_specs=[tok(d), tok(d), tok(128), tok(128)],
        out_shape=[
            jax.ShapeDtypeStruct((b, l, d), F32),
            jax.ShapeDtypeStruct((b, l, d), BF16),
            jax.ShapeDtypeStruct((b, l, 128), jnp.int32),
            jax.ShapeDtypeStruct((b, l, 128), F32),
        ],
        compiler_params=pltpu.CompilerParams(
            dimension_semantics=("arbitrary", "arbitrary"), vmem_limit_bytes=V7X_VMEM_LIMIT_BYTES),
        name="merge0_postmix",
    )(o, y, conv, z, h, mods, mods, mods, row(gdn_norm_w), row(dsk), row(ssd_norm_w), w_out,
      row(post_w), row(pre_w), rw_pad, rb_pad)


SCAN_C = 128
GDN_INV_BLOCK = 16


def _dot(a, b):
    return jnp.dot(a, b, preferred_element_type=F32)


def _dot_nt(a, b):
    return lax.dot_general(a, b, (((1,), (1,)), ((), ())), preferred_element_type=F32)


def _dot_tn(a, b):
    return lax.dot_general(a, b, (((0,), (0,)), ((), ())), preferred_element_type=F32)


def _bdot(a, b):
    return _dot(a.astype(BF16), b.astype(BF16))


def _bdot_nt(a, b):
    return _dot_nt(a.astype(BF16), b.astype(BF16))


def _bdot_tn(a, b):
    return _dot_tn(a.astype(BF16), b.astype(BF16))


def _scan_masks(fwd):
    row = lax.broadcasted_iota(jnp.int32, (SCAN_C, SCAN_C), 0)
    col = lax.broadcasted_iota(jnp.int32, (SCAN_C, SCAN_C), 1)
    lead = (row - col) * jnp.where(fwd, 1, -1)
    return row, col, lead >= 0, lead <= 0, lead > 0


def _cumsum_col_row(g_col, g_row, incl, incl_t):
    gc_col = jnp.sum(jnp.where(incl, g_row, 0.0), axis=1, keepdims=True)
    gc_row = jnp.sum(jnp.where(incl_t, g_col, 0.0), axis=0, keepdims=True)
    return gc_col, gc_row


def _unit_tri_inverse(ms, row, col):
    eye = (row == col).astype(F32)
    same = (row // GDN_INV_BLOCK) == (col // GDN_INV_BLOCK)
    mds = [jnp.where(same, m, 0.0) for m in ms]
    mos = [m - md for m, md in zip(ms, mds)]
    ps = mds
    tds = [eye - md for md in mds]
    for _ in range(int(math.log2(GDN_INV_BLOCK)) - 1):
        ps = [_dot(p, p) for p in ps]
        tds = [td + _dot(td, p) for td, p in zip(tds, ps)]
    ps = [_dot(td, mo) for td, mo in zip(tds, mos)]
    qs = [eye - n for n in ps]
    for _ in range(int(math.log2(SCAN_C // GDN_INV_BLOCK)) - 1):
        ps = [_dot(p, p) for p in ps]
        qs = [q + _dot(q, p) for q, p in zip(qs, ps)]
    return [_dot(q, td) for q, td in zip(qs, tds)]


GDN_PREP_GROUP = 8
GDN_HEADS_PER_STEP = 2


def _gdn_body(ql_ref, kl_ref, vl_ref, gl_ref, gtl_ref, qc_ref, kc_ref, vc_ref, gc_ref, gtc_ref,
              ol_ref, oc_ref, u_s, w_s, qk_s, qd_s, kd_s, dec_s, obl_s, obc_s):
    hp = pl.program_id(1)
    nh = GDN_HEADS_PER_STEP
    c = SCAN_C
    ncc = qc_ref.shape[1] // c
    ncl = ql_ref.shape[1] // c
    grp = math.gcd(ncl, GDN_PREP_GROUP)
    lane = lax.broadcasted_iota(jnp.int32, (c, 128), 1)

    def order(d, i, n):
        return i if d == 0 else n - 1 - i

    def prep(s, d, q_ref, k_ref, v_ref, g_ref, gt_ref, cis, ps):
        row, col, incl, incl_t, strict = _scan_masks(d == 0)
        col_g = d * GDN_HEADS + hp * nh + s
        col_b = 2 * GDN_HEADS + col_g
        hs = slice(s * 128, (s + 1) * 128)
        cid = s * 2 + d
        loaded = []
        for ci in cis:
            t0 = pl.multiple_of(ci * c, c)
            loaded.append((q_ref[0, pl.ds(t0, c), hs].astype(F32), k_ref[0, pl.ds(t0, c), hs].astype(F32),
                           v_ref[0, pl.ds(t0, c), hs].astype(F32), g_ref[0, pl.ds(t0, c), :],
                           gt_ref[0, ci, pl.ds(col_g, 1), :]))
        parts, ms = [], []
        for q, k, v, gch, g_row in loaded:
            q = q * lax.rsqrt(jnp.sum(q * q, axis=-1, keepdims=True) + RMS_EPS) * (GDN_DK ** -0.5)
            k = k * lax.rsqrt(jnp.sum(k * k, axis=-1, keepdims=True) + RMS_EPS)
            g_col = jnp.sum(jnp.where(lane == col_g, gch, 0.0), axis=1, keepdims=True)
            b_col = jnp.sum(jnp.where(lane == col_b, gch, 0.0), axis=1, keepdims=True)
            gc_col, gc_row = _cumsum_col_row(g_col, g_row, incl, incl_t)
            g_tot = jnp.sum(g_col, axis=0, keepdims=True)
            decay = jnp.where(incl, jnp.exp(jnp.where(incl, gc_col - gc_row, 0.0)), 0.0)
            kb = k * b_col
            ms.append(jnp.where(strict, _bdot_nt(kb, k) * decay, 0.0))
            egc = jnp.exp(gc_col)
            parts.append((v * b_col, kb * egc, _bdot_nt(q, k) * decay, q * egc, k * jnp.exp(g_tot - gc_col),
                          jnp.broadcast_to(jnp.exp(g_tot), (1, 128))))
        t_invs = _unit_tri_inverse(ms, row, col)
        uw = [(_bdot(t_inv, part[0]), _bdot(t_inv, part[1])) for t_inv, part in zip(t_invs, parts)]
        for p, (u, w), (_, _, qk, qd, kd, dec) in zip(ps, uw, parts):
            p0 = pl.multiple_of(p * c, c)
            u_s[cid, pl.ds(p0, c), :] = u
            w_s[cid, pl.ds(p0, c), :] = w
            qk_s[cid, pl.ds(p0, c), :] = qk
            qd_s[cid, pl.ds(p0, c), :] = qd
            kd_s[cid, pl.ds(p0, c), :] = kd
            dec_s[cid, pl.ds(p, 1), :] = dec

    def advance(of_ref, ob_ref, i, n, p, states):
        p0 = pl.multiple_of(p * c, c)
        outs, new_states = [], []
        for cid, st in enumerate(states):
            v_new = u_s[cid, pl.ds(p0, c), :] - _bdot(w_s[cid, pl.ds(p0, c), :], st)
            outs.append(_bdot(qd_s[cid, pl.ds(p0, c), :], st) + _bdot(qk_s[cid, pl.ds(p0, c), :], v_new))
            new_states.append(st * dec_s[cid, pl.ds(p, 1), :] + _bdot_tn(kd_s[cid, pl.ds(p0, c), :], v_new))
        for s in range(nh):
            of_ref[0, pl.ds(pl.multiple_of(order(0, i, n) * c, c), c), s * 128:(s + 1) * 128] = outs[s * 2]
            ob_ref[s, pl.ds(pl.multiple_of(order(1, i, n) * c, c), c), :] = outs[s * 2 + 1]
        return tuple(new_states)

    for s, d in [(s, d) for s in range(nh) for d in range(2)]:
        prep(s, d, qc_ref, kc_ref, vc_ref, gc_ref, gtc_ref, [order(d, i, ncc) for i in range(ncc)],
             list(range(ncc)))

        def prep_lat(gi, carry, s=s, d=d):
            base = gi * grp
            prep(s, d, ql_ref, kl_ref, vl_ref, gl_ref, gtl_ref,
                 [order(d, base + j, ncl) for j in range(grp)], [ncc + base + j for j in range(grp)])
            return carry

        lax.fori_loop(0, ncl // grp, prep_lat, 0)

    states = tuple(jnp.zeros((GDN_DK, GDN_DV), F32) for _ in range(2 * nh))
    for i in range(ncc):
        states = advance(oc_ref, obc_s, i, ncc, i, states)
    lax.fori_loop(0, ncl, lambda i, st: advance(ol_ref, obl_s, i, ncl, ncc + i, st), states)
    for s in range(nh):
        hs = slice(s * 128, (s + 1) * 128)
        ol_ref[0, :, hs] = ol_ref[0, :, hs] + obl_s[s]
        oc_ref[0, :, hs] = oc_ref[0, :, hs] + obc_s[s]


SSD_R = SSD_HEADS // SSD_GROUPS
SSD_PAIRS = SSD_R * SSD_HEADDIM // 128
GATE_COL_DT = 4 * GDN_HEADS
GATE_COL_DA = GATE_COL_DT + 2 * SSD_HEADS


def _ssd_body(xl_ref, bl_ref, cl_ref, gl_ref, gtl_ref, xc_ref, bc_ref, cc_ref, gc_ref, gtc_ref,
              yl_ref, yc_ref, st_ref):
    g = pl.program_id(1)
    d = pl.program_id(2)
    fwd = d == 0
    c = SCAN_C
    ncc = xc_ref.shape[1] // c
    ncl = xl_ref.shape[1] // c
    _, _, incl, incl_t, _ = _scan_masks(fwd)
    lane = lax.broadcasted_iota(jnp.int32, (c, 128), 1)
    low = lane < SSD_HEADDIM

    @pl.when(fwd)
    def _():
        yl_ref[...] = jnp.zeros(yl_ref.shape, F32)
        yc_ref[...] = jnp.zeros(yc_ref.shape, F32)

    st_ref[...] = jnp.zeros(st_ref.shape, F32)

    def chunk(x_ref, b_ref, c_ref, g_ref, gt_ref, y_ref, ci):
        t0 = pl.multiple_of(ci * c, c)
        bm = b_ref[0, pl.ds(t0, c), :].astype(BF16)
        cm = c_ref[0, pl.ds(t0, c), :].astype(BF16)
        gch = g_ref[0, pl.ds(t0, c), :]
        cb = _dot_nt(cm, bm)
        for pr in range(SSD_PAIRS):
            per_head = []
            for s in range(2):
                hh = g * SSD_R + 2 * pr + s
                col_dt = GATE_COL_DT + d * SSD_HEADS + hh
                col_da = GATE_COL_DA + d * SSD_HEADS + hh
                dt_col = jnp.sum(jnp.where(lane == col_dt, gch, 0.0), axis=1, keepdims=True)
                da_col = jnp.sum(jnp.where(lane == col_da, gch, 0.0), axis=1, keepdims=True)
                da_row = gt_ref[0, ci, pl.ds(col_da, 1), :]
                acs_col, acs_row = _cumsum_col_row(da_col, da_row, incl, incl_t)
                a_tot = jnp.sum(da_col, axis=0, keepdims=True)
                lmat = jnp.where(incl, jnp.exp(jnp.where(incl, acs_col - acs_row, 0.0)), 0.0)
                per_head.append((dt_col, acs_col, a_tot, (cb * lmat).astype(BF16)))
            pick = lambda k: jnp.where(low, per_head[0][k], per_head[1][k])
            x = x_ref[0, pl.ds(t0, c), pr * 128:(pr + 1) * 128].astype(F32)
            xdt = x * pick(0)
            acs = pick(1)
            a_tot = pick(2)
            xdt_b = xdt.astype(BF16)
            y_diag = jnp.where(low, _dot(per_head[0][3], xdt_b), _dot(per_head[1][3], xdt_b))
            st = st_ref[pr]
            y_off = _dot(cm, st.astype(BF16)) * jnp.exp(acs)
            y_ref[0, pl.ds(t0, c), pr * 128:(pr + 1) * 128] += y_diag + y_off
            st_ref[pr] = st * jnp.exp(a_tot) + _dot_tn(bm, (xdt * jnp.exp(a_tot - acs)).astype(BF16))

    def order(i, n):
        return jnp.where(fwd, i, n - 1 - i)

    for i in range(ncc):
        chunk(xc_ref, bc_ref, cc_ref, gc_ref, gtc_ref, yc_ref, order(i, ncc))

    def lat(i, carry):
        chunk(xl_ref, bl_ref, cl_ref, gl_ref, gtl_ref, yl_ref, order(i, ncl))
        return carry

    lax.fori_loop(0, ncl, lat, 0)


def _ssd_scan(xbc_l, g_l, gt_l, xbc_c, g_c, gt_c, col0=0):
    b, l, _ = xbc_l.shape
    lc = xbc_c.shape[1]
    gw = SSD_R * SSD_HEADDIM
    x0 = col0 // gw
    nxb = (col0 + SSD_INNER) // 128

    def stream(n):
        return [pl.BlockSpec((1, n, gw), lambda bi, gi, di: (bi, 0, x0 + gi)),
                pl.BlockSpec((1, n, 128), lambda bi, gi, di: (bi, 0, nxb + gi)),
                pl.BlockSpec((1, n, 128), lambda bi, gi, di: (bi, 0, nxb + SSD_GROUPS + gi)),
                pl.BlockSpec((1, n, 128), lambda bi, gi, di: (bi, 0, 0)),
                pl.BlockSpec((1, n // SCAN_C, 128, SCAN_C), lambda bi, gi, di: (bi, 0, 0, 0))]

    return pl.pallas_call(
        _ssd_body,
        grid=(b, SSD_GROUPS, 2),
        in_specs=stream(l) + stream(lc),
        out_specs=[pl.BlockSpec((1, l, gw), lambda bi, gi, di: (bi, 0, gi)),
                   pl.BlockSpec((1, lc, gw), lambda bi, gi, di: (bi, 0, gi))],
        out_shape=[jax.ShapeDtypeStruct((b, l, SSD_INNER), F32),
                   jax.ShapeDtypeStruct((b, lc, SSD_INNER), F32)],
        scratch_shapes=[pltpu.VMEM((SSD_PAIRS, SSD_STATE, 128), F32)],
        compiler_params=pltpu.CompilerParams(
            dimension_semantics=("arbitrary", "arbitrary", "arbitrary"), vmem_limit_bytes=V7X_VMEM_LIMIT_BYTES),
        name="ssd_scan",
    )(xbc_l, xbc_l, xbc_l, g_l, gt_l, xbc_c, xbc_c, xbc_c, g_c, gt_c)


def _chunk_rows(gt):
    b, w, l = gt.shape
    return gt.reshape(b, w, l // SCAN_C, SCAN_C).transpose(0, 2, 1, 3)


def _gdn_scan(qkv_l, g_l, gt_l, qkv_c, g_c, gt_c):
    b, l, _ = qkv_l.shape
    lc = qkv_c.shape[1]
    hd = GDN_HEADS

    nh = GDN_HEADS_PER_STEP
    steps = hd // nh
    hw = nh * 128

    def stream(n):
        tok = lambda off: pl.BlockSpec((1, n, hw), lambda bi, hi: (bi, 0, off + hi))
        return [tok(0), tok(steps), tok(2 * steps),
                pl.BlockSpec((1, n, 128), lambda bi, hi: (bi, 0, 0)),
                pl.BlockSpec((1, n // SCAN_C, 128, SCAN_C), lambda bi, hi: (bi, 0, 0, 0))]

    nt = l + lc
    return pl.pallas_call(
        _gdn_body,
        grid=(b, steps),
        in_specs=stream(l) + stream(lc),
        out_specs=[pl.BlockSpec((1, l, hw), lambda bi, hi: (bi, 0, hi)),
                   pl.BlockSpec((1, lc, hw), lambda bi, hi: (bi, 0, hi))],
        out_shape=[jax.ShapeDtypeStruct((b, l, hd * GDN_DV), F32),
                   jax.ShapeDtypeStruct((b, lc, hd * GDN_DV), F32)],
        scratch_shapes=[pltpu.VMEM((2 * nh, nt, 128), F32) for _ in range(5)]
        + [pltpu.VMEM((2 * nh, nt // SCAN_C, 128), F32), pltpu.VMEM((nh, l, 128), F32),
           pltpu.VMEM((nh, lc, 128), F32)],
        compiler_params=pltpu.CompilerParams(
            dimension_semantics=("arbitrary", "arbitrary"), vmem_limit_bytes=V7X_VMEM_LIMIT_BYTES),
        name="gdn_scan",
    )(qkv_l, qkv_l, qkv_l, g_l, gt_l, qkv_c, qkv_c, qkv_c, g_c, gt_c)


def _ada_body(c_ref, w_ref, b_ref, o_ref):
    cc = c_ref[...]
    o_ref[0] = jnp.dot(cc * jax.nn.sigmoid(cc), w_ref[0], preferred_element_type=F32) + b_ref[0]


def _ada_mods(cond, ada_w, ada_b):
    r, d = cond.shape
    depth, _, w6 = ada_w.shape
    rp = ((r + 7) // 8) * 8
    cond = jnp.pad(cond, ((0, rp - r), (0, 0)))
    out = pl.pallas_call(
        _ada_body,
        grid=(depth, w6 // d),
        in_specs=[pl.BlockSpec((rp, d), lambda li, ni: (0, 0)),
                  pl.BlockSpec((1, d, d), lambda li, ni: (li, 0, ni)),
                  pl.BlockSpec((1, 1, d), lambda li, ni: (li, 0, ni))],
        out_specs=pl.BlockSpec((1, rp, d), lambda li, ni: (li, 0, ni)),
        out_shape=jax.ShapeDtypeStruct((depth, rp, w6), F32),
        compiler_params=pltpu.CompilerParams(dimension_semantics=("arbitrary", "arbitrary")),
        name="ada_mods",
    )(cond, ada_w, ada_b.reshape(depth, 1, w6))
    return out[:, :r]


def _s5_discretize(a_re, a_im, log_step, b_re, b_im):
    step = jnp.exp(log_step)[:, None]
    mag = jnp.exp(a_re * step)
    lam_re, lam_im = mag * jnp.cos(a_im * step), mag * jnp.sin(a_im * step)
    den = a_re * a_re + a_im * a_im
    f_re = ((lam_re - 1.0) * a_re + lam_im * a_im) / den
    f_im = (lam_im * a_re - (lam_re - 1.0) * a_im) / den
    bb_re = f_re[..., None] * b_re - f_im[..., None] * b_im
    bb_im = f_re[..., None] * b_im + f_im[..., None] * b_re
    return lam_re, lam_im, bb_re, bb_im


def kernel(x, c, ctx, c_ctx, ada_w, ada_b, mix_norm_pre, mix_norm_post, ffn_norm_pre, ffn_norm_post, router_w, router_b, moe_w_gate, moe_b_gate, moe_w_up, moe_b_up, moe_w_down, moe_b_down, hy_w_in, gdn_conv_w, gdn_a_log, gdn_dt_bias, gdn_norm_w, ssd_conv_w, ssd_conv_b, ssd_a_log, ssd_dt_bias, ssd_d, ssd_norm_w, hy_w_out, s5_w_in, s5_a_re, s5_a_im, s5_log_step, s5_b_re, s5_b_im, s5_c_re, s5_c_im, s5_d, s5_w_glu_a, s5_w_glu_b):
    depth = ada_w.shape[0]
    rows = x.shape[1] // GRID_W
    h_lat, h_ctx = x, ctx
    nb = c.shape[0]
    mods_all = _ada_mods(jnp.concatenate([c, c_ctx[None, :]], axis=0), ada_w, ada_b)
    for i in range(depth):
        j = i // 2
        need_ctx = i < depth - 1
        mods_lat = mods_all[i, :nb, None, :]
        mods_ctx = mods_all[i, nb:nb + 1, None, :]
        ffn = (i, moe_w_gate, moe_b_gate, moe_w_up, moe_b_up, moe_w_down, moe_b_down, ffn_norm_post[i])
        rw_pad, rb_pad = _pad_router(router_w[i], router_b[i])
        bsz, l, d = h_lat.shape
        lc = h_ctx.shape[1]
        tl_lat, tl_ctx = 512, lc
        if i % 2 == 1:
            assert not need_ctx
            nbg = bsz // S5_BATCH_SUB
            w_in = s5_w_in[j].astype(BF16)
            u_lat = _s5_inproj(h_lat, mods_lat, mix_norm_pre[i], w_in, S5_TL)
            u_ctx = _s5_inproj(h_ctx, mods_ctx, mix_norm_pre[i], w_in, S5_TL)
            bblk, cblk, lam = _s5_block_params(s5_a_re[j], s5_a_im[j], s5_log_step[j], s5_b_re[j], s5_b_im[j],
                                               s5_c_re[j], s5_c_im[j])
            y = _s5_scan(u_ctx.reshape(nbg, lc, S5_BATCH_SUB, d), u_lat.reshape(nbg, l, S5_BATCH_SUB, d),
                         bblk, cblk, lam)
            h1, t_lat, idx, probs = _s5_glu_postmix(
                y.reshape(2, nbg, l * S5_BATCH_SUB, d), u_lat, h_lat, mods_lat, s5_d[j],
                s5_w_glu_a[j].astype(BF16), s5_w_glu_b[j].astype(BF16),
                mix_norm_post[i], ffn_norm_pre[i], rw_pad, rb_pad, S5_GLU_TL)
            (h_lat,) = _moe_layer([(t_lat, idx, probs, h1, mods_lat, tl_lat)], *ffn)
            continue
        w_all, conv_w, conv_b, gate_params = _in0_params(hy_w_in[j], gdn_conv_w[j], ssd_conv_w[j], ssd_conv_b[j],
                                                         gdn_a_log[j], gdn_dt_bias[j], ssd_a_log[j], ssd_dt_bias[j])
        conv_l, z_l, g_l, gt_l = _in0_features(h_lat, mods_lat, mix_norm_pre[i], w_all, conv_w, conv_b, gate_params,
                                               rows, GRID_W)
        conv_c, z_c, g_c, gt_c = _in0_features(h_ctx, mods_ctx, mix_norm_pre[i], w_all, conv_w, conv_b, gate_params,
                                               1, lc)
        o_l, o_c = _gdn_scan(conv_l, g_l, gt_l, conv_c, g_c, gt_c)
        y_l, y_c = _ssd_scan(conv_l, g_l, gt_l, conv_c, g_c, gt_c, col0=GDN_CONV_CH)
        w_out = hy_w_out[j].astype(BF16)
        merge = lambda o, y, conv, z, h, mods, tl: _merge0_postmix(
            o, y, conv, z, h, mods, gdn_norm_w[j], ssd_d[j], ssd_norm_w[j], w_out,
            mix_norm_post[i], ffn_norm_pre[i], rw_pad, rb_pad, tl)
        streams = [merge(o_l, y_l, conv_l, z_l, h_lat, mods_lat, tl_lat) + (mods_lat, tl_lat)]
        if need_ctx:
            streams.append(merge(o_c, y_c, conv_c, z_c, h_ctx, mods_ctx, tl_ctx) + (mods_ctx, tl_ctx))
        outs = _moe_layer([(t, idx, probs, h1, mods, tl) for h1, t, idx, probs, mods, tl in streams], *ffn)
        h_lat = outs[0]
        if need_ctx:
            h_ctx = outs[1]
    return h_lat
```
